```python
import jax, jax.numpy as jnp
from jax import lax
import numpy as np

D_MODEL = 1024
BATCH = 8
SEQ = 2048
DEPTH = 4
DEC_BATCH = 32
DEC_SEQ = 4
PAST_LEN = 8192
PAGE_SIZE = 128

D_FF = 2816
P_A = 64
DI_A = D_MODEL
H_A = DI_A // P_A
G_A = 2
N_A = 128
CONV_W = 4
CONV_DIM = DI_A + 2 * G_A * N_A
SSD_CHUNK = 128
HD_F = 64
H_F = D_MODEL // HD_F
KV_F = H_F // 2
GQA_R = H_F // KV_F
Q_BLOCK = 128
ATTN_SCALE = HD_F ** -0.5
N_MOD = 9
RESID_HALF = 0.5
EPS = 1e-6
IN_SIZES = (DI_A, CONV_DIM, H_A, H_F * HD_F, KV_F * HD_F, KV_F * HD_F, H_F, D_MODEL, D_MODEL)
IN_WIDTH = sum(IN_SIZES)

kernel_name = "hybrid_ssd_fox_macaron_adaln_step"


def _split_points():
    return [int(v) for v in np.cumsum(IN_SIZES)[:-1]]


def rms_norm(x, w):
    xf = x.astype(jnp.float32)
    y = xf * lax.rsqrt(jnp.mean(xf * xf, axis=-1, keepdims=True) + EPS)
    return (y * w.astype(jnp.float32)).astype(x.dtype)


def modulate(x, shift, scale):
    return x * (1 + scale[:, None, :]) + shift[:, None, :]


def swiglu(x, w_in, w_out):
    gate, up = jnp.split(x @ w_in, 2, axis=-1)
    return (jax.nn.silu(gate) * up) @ w_out


def ssd_scan(x, dt, a, b_in, c_in, h0):
    f32 = jnp.float32
    bsz, length = x.shape[0], x.shape[1]
    q = min(SSD_CHUNK, length)
    nc = length // q
    e = H_A // G_A
    xdt = (x.astype(f32) * dt[..., None]).reshape(bsz, nc, q, G_A, e, P_A)
    a_cum = jnp.cumsum((dt * a).reshape(bsz, nc, q, G_A, e), axis=2)
    bc = b_in.astype(f32).reshape(bsz, nc, q, G_A, N_A)
    cc = c_in.astype(f32).reshape(bsz, nc, q, G_A, N_A)
    a_t = jnp.moveaxis(a_cum, 2, -1)
    causal = jnp.tril(jnp.ones((q, q), bool))
    seg = jnp.where(causal, a_t[..., :, None] - a_t[..., None, :], -jnp.inf)
    mix = jnp.einsum('bcqgn,bcsgn->bcgqs', cc, bc)[:, :, :, None] * jnp.exp(seg)
    y_diag = jnp.einsum('bcgeqs,bcsgep->bcqgep', mix, xdt)
    decay_to_end = jnp.exp(a_cum[:, :, -1:] - a_cum)
    chunk_states = jnp.einsum('bcsgn,bcsgep->bcgepn', bc, xdt * decay_to_end[..., None])
    chunk_decay = jnp.exp(a_cum[:, :, -1])

    def step(h, inp):
        st, dec = inp
        return dec[..., None, None] * h + st, h

    h_last, h_enter = lax.scan(
        step, h0.astype(f32).reshape(bsz, G_A, e, P_A, N_A),
        (jnp.moveaxis(chunk_states, 1, 0), jnp.moveaxis(chunk_decay, 1, 0)))
    h_enter = jnp.moveaxis(h_enter, 0, 1)
    y_off = jnp.einsum('bcqgn,bcgepn->bcqgep', cc, h_enter) * jnp.exp(a_cum)[..., None]
    y = (y_diag + y_off).reshape(bsz, length, H_A, P_A)
    return y, h_last.reshape(bsz, H_A, P_A, N_A)


def mamba_branch(z, xbc, dt_raw, conv_prev, h0, conv_w, conv_b, dt_bias, a_log, d_skip, gn_w):
    bsz, length = xbc.shape[0], xbc.shape[1]
    full = jnp.concatenate([conv_prev.astype(xbc.dtype), xbc], axis=1)
    conv = conv_b + full[:, 0:length] * conv_w[0]
    for j in range(1, CONV_W):
        conv = conv + full[:, j:j + length] * conv_w[j]
    conv = jax.nn.silu(conv)
    conv_new = full[:, length:]
    xs, b_in, c_in = jnp.split(conv, [DI_A, DI_A + G_A * N_A], axis=-1)
    x4 = xs.reshape(bsz, length, H_A, P_A)
    dt = jax.nn.softplus(dt_raw.astype(jnp.float32) + dt_bias.astype(jnp.float32))
    a = -jnp.exp(a_log.astype(jnp.float32))
    y, h_last = ssd_scan(x4, dt, a, b_in.reshape(bsz, length, G_A, N_A),
                         c_in.reshape(bsz, length, G_A, N_A), h0)
    y = y + d_skip.astype(jnp.float32)[:, None] * x4.astype(jnp.float32)
    y = y.reshape(bsz, length, DI_A).astype(z.dtype)
    return rms_norm(y * jax.nn.silu(z), gn_w), conv_new, h_last


def fox_prompt(q, k, v, logf):
    bsz, s_len = q.shape[0], q.shape[1]
    qb = min(Q_BLOCK, s_len)
    nb = s_len // qb
    f_cum = jnp.cumsum(logf, axis=1).reshape(bsz, s_len, KV_F, GQA_R)
    f_k = f_cum.transpose(0, 2, 3, 1)[:, :, :, None, :]
    q_blocks = q.reshape(bsz, nb, qb, KV_F, GQA_R, HD_F).transpose(1, 0, 2, 3, 4, 5)
    fq_blocks = f_cum.reshape(bsz, nb, qb, KV_F, GQA_R).transpose(1, 0, 3, 4, 2)
    k_pos = jnp.arange(s_len)

    def one_block(args):
        qblk, fq, i = args
        q_pos = i * qb + jnp.arange(qb)
        s = jnp.einsum('bqgrd,bkgd->bgrqk', qblk, k, preferred_element_type=jnp.float32) * ATTN_SCALE
        s = s + (fq[..., None] - f_k)
        s = jnp.where(k_pos[None, :] <= q_pos[:, None], s, -jnp.inf)
        prob = jax.nn.softmax(s, axis=-1).astype(v.dtype)
        return jnp.einsum('bgrqk,bkgd->bqgrd', prob, v)

    out = lax.map(one_block, (q_blocks, fq_blocks, jnp.arange(nb)))
    return out.transpose(1, 0, 2, 3, 4, 5).reshape(bsz, s_len, H_F * HD_F)


def fox_sample(q, k, v, logf, k_past, v_past, lf_past):
    db, t_len = q.shape[0], q.shape[1]
    n_past = k_past.shape[1]
    qg = q.reshape(db, t_len, KV_F, GQA_R, HD_F)
    g_new = jnp.cumsum(logf, axis=1).reshape(db, t_len, KV_F, GQA_R)
    lf_past = lf_past.astype(jnp.float32)
    g_past = (lf_past - lax.cumsum(lf_past, axis=1, reverse=True)).reshape(db, n_past, KV_F, GQA_R)
    gq = g_new.transpose(0, 2, 3, 1)[..., None]
    s_past = jnp.einsum('btgrd,bpgd->bgrtp', qg, k_past, preferred_element_type=jnp.float32) * ATTN_SCALE
    s_past = s_past + (gq - g_past.transpose(0, 2, 3, 1)[:, :, :, None, :])
    s_new = jnp.einsum('btgrd,bsgd->bgrts', qg, k, preferred_element_type=jnp.float32) * ATTN_SCALE
    s_new = s_new + (gq - g_new.transpose(0, 2, 3, 1)[:, :, :, None, :])
    causal = jnp.tril(jnp.ones((t_len, t_len), bool))
    s_new = jnp.where(causal, s_new, -jnp.inf)
    prob = jax.nn.softmax(jnp.concatenate([s_past, s_new], axis=-1), axis=-1).astype(v.dtype)
    out = (jnp.einsum('bgrtp,bpgd->btgrd', prob[..., :n_past], v_past)
           + jnp.einsum('bgrts,bsgd->btgrd', prob[..., n_past:], v))
    return out.reshape(db, t_len, H_F * HD_F)


def run_layer(h, c, l, p, fox_fn, conv_prev, ssm_prev):
    bsz, length = h.shape[0], h.shape[1]
    mod = (jax.nn.silu(c) @ p['w_ada'][l] + p['b_ada'][l]).reshape(bsz, N_MOD, D_MODEL)
    u = modulate(rms_norm(h, p['norm_w'][l, 0]), mod[:, 0], mod[:, 1])
    h = h + RESID_HALF * mod[:, 2][:, None, :] * swiglu(u, p['w_ffn1_in'][l], p['w_ffn1_out'][l])
    u = modulate(rms_norm(h, p['norm_w'][l, 1]), mod[:, 3], mod[:, 4])
    z, xbc, dt_raw, q, k, v, f_logit, g_a, g_b = jnp.split(u @ p['w_in'][l], _split_points(), axis=-1)
    y_a, conv_new, ssm_new = mamba_branch(z, xbc, dt_raw, conv_prev, ssm_prev, p['conv_w'][l], p['conv_b'][l],
                                          p['dt_bias'][l], p['a_log'][l], p['d_skip'][l], p['gn_w'][l])
    q = rms_norm(q.reshape(bsz, length, H_F, HD_F), p['q_norm_w'][l])
    k = rms_norm(k.reshape(bsz, length, KV_F, HD_F), p['k_norm_w'][l])
    v = v.reshape(bsz, length, KV_F, HD_F)
    logf = jax.nn.log_sigmoid(f_logit.astype(jnp.float32) + p['b_f'][l].astype(jnp.float32))
    y_b = fox_fn(l, q, k, v, logf)
    merged = (jax.nn.sigmoid(g_a) * (y_a @ p['w_proj_a'][l])
              + jax.nn.sigmoid(g_b) * (y_b @ p['w_proj_b'][l]))
    h = h + mod[:, 5][:, None, :] * (merged @ p['w_out'][l])
    u = modulate(rms_norm(h, p['norm_w'][l, 2]), mod[:, 6], mod[:, 7])
    h = h + RESID_HALF * mod[:, 8][:, None, :] * swiglu(u, p['w_ffn2_in'][l], p['w_ffn2_out'][l])
    return h, (k, v, logf, conv_new, ssm_new)


def setup_inputs(seed: int = 0) -> dict:
    key = jax.random.key(seed)
    ks = iter(jax.random.split(key, 40))
    f32 = jnp.float32

    def nrm(shape, scale):
        return jax.random.normal(next(ks), shape, f32) * scale

    def unif(shape, lo, hi):
        return jax.random.uniform(next(ks), shape, f32, lo, hi)

    n_pages = PAST_LEN // PAGE_SIZE
    n_used = DEC_BATCH * n_pages
    n_pool = n_used + max(1, n_used // 4)
    page_table = jax.random.permutation(next(ks), n_pool)[:n_used].reshape(DEC_BATCH, n_pages).astype(jnp.int32)
    dt0 = jnp.exp(unif((DEPTH, H_A), float(np.log(1e-3)), float(np.log(1e-1))))
    return {
        'x_prompt': nrm((BATCH, SEQ, D_MODEL), 1.0),
        'x_sample': nrm((DEC_BATCH, DEC_SEQ, D_MODEL), 1.0),
        'cache_k': nrm((DEPTH, n_pool, PAGE_SIZE, KV_F, HD_F), 1.0),
        'cache_v': nrm((DEPTH, n_pool, PAGE_SIZE, KV_F, HD_F), 1.0),
        'cache_logf': jax.nn.log_sigmoid(9.0 + nrm((DEPTH, n_pool, PAGE_SIZE, H_F), 1.0)),
        'state_conv': nrm((DEPTH, DEC_BATCH, CONV_W - 1, CONV_DIM), 1.0),
        'state_ssm': nrm((DEPTH, DEC_BATCH, H_A, P_A, N_A), 0.3),
        'page_table': page_table,
        'c_prompt': nrm((BATCH, D_MODEL), 1.0),
        'c_sample': nrm((DEC_BATCH, D_MODEL), 1.0),
        'w_ada': nrm((DEPTH, D_MODEL, N_MOD * D_MODEL), 0.5 * D_MODEL ** -0.5),
        'b_ada': nrm((DEPTH, N_MOD * D_MODEL), 0.02),
        'norm_w': 1.0 + nrm((DEPTH, 3, D_MODEL), 0.01),
        'w_ffn1_in': nrm((DEPTH, D_MODEL, 2 * D_FF), D_MODEL ** -0.5),
        'w_ffn1_out': nrm((DEPTH, D_FF, D_MODEL), D_FF ** -0.5),
        'w_ffn2_in': nrm((DEPTH, D_MODEL, 2 * D_FF), D_MODEL ** -0.5),
        'w_ffn2_out': nrm((DEPTH, D_FF, D_MODEL), D_FF ** -0.5),
        'w_in': nrm((DEPTH, D_MODEL, IN_WIDTH), D_MODEL ** -0.5),
        'conv_w': nrm((DEPTH, CONV_W, CONV_DIM), CONV_W ** -0.5),
        'conv_b': nrm((DEPTH, CONV_DIM), 0.02),
        'dt_bias': dt0 + jnp.log(-jnp.expm1(-dt0)),
        'a_log': jnp.log(unif((DEPTH, H_A), 1.0, 16.0)),
        'd_skip': 1.0 + nrm((DEPTH, H_A), 0.1),
        'gn_w': 1.0 + nrm((DEPTH, DI_A), 0.01),
        'q_norm_w': 1.0 + nrm((DEPTH, HD_F), 0.01),
        'k_norm_w': 1.0 + nrm((DEPTH, HD_F), 0.01),
        'b_f': unif((DEPTH, H_F), 3.0, 8.0),
        'w_proj_a': nrm((DEPTH, DI_A, D_MODEL), DI_A ** -0.5),
        'w_proj_b': nrm((DEPTH, H_F * HD_F, D_MODEL), (H_F * HD_F) ** -0.5),
        'w_out': nrm((DEPTH, D_MODEL, D_MODEL), D_MODEL ** -0.5),
    }


def reference(x_prompt, x_sample, cache_k, cache_v, cache_logf, state_conv, state_ssm, page_table,
              c_prompt, c_sample, w_ada, b_ada, norm_w, w_ffn1_in, w_ffn1_out, w_ffn2_in, w_ffn2_out,
              w_in, conv_w, conv_b, dt_bias, a_log, d_skip, gn_w, q_norm_w, k_norm_w, b_f,
              w_proj_a, w_proj_b, w_out):
    p = {'w_ada': w_ada, 'b_ada': b_ada, 'norm_w': norm_w, 'w_ffn1_in': w_ffn1_in, 'w_ffn1_out': w_ffn1_out,
         'w_ffn2_in': w_ffn2_in, 'w_ffn2_out': w_ffn2_out, 'w_in': w_in, 'conv_w': conv_w, 'conv_b': conv_b,
         'dt_bias': dt_bias, 'a_log': a_log, 'd_skip': d_skip, 'gn_w': gn_w, 'q_norm_w': q_norm_w,
         'k_norm_w': k_norm_w, 'b_f': b_f, 'w_proj_a': w_proj_a, 'w_proj_b': w_proj_b, 'w_out': w_out}

    def fox_p(l, q, k, v, logf):
        return fox_prompt(q, k, v, logf)

    def fox_s(l, q, k, v, logf):
        db = q.shape[0]
        n_past = page_table.shape[1] * cache_k.shape[2]
        k_past = cache_k[l][page_table].reshape(db, n_past, KV_F, HD_F)
        v_past = cache_v[l][page_table].reshape(db, n_past, KV_F, HD_F)
        lf_past = cache_logf[l][page_table].reshape(db, n_past, H_F)
        return fox_sample(q, k, v, logf, k_past, v_past, lf_past)

    bp = x_prompt.shape[0]
    conv0 = jnp.zeros((bp, CONV_W - 1, CONV_DIM), x_prompt.dtype)
    ssm0 = jnp.zeros((bp, H_A, P_A, N_A), jnp.float32)
    h = x_prompt
    st_p = []
    for l in range(DEPTH):
        h, st = run_layer(h, c_prompt, l, p, fox_p, conv0, ssm0)
        st_p.append(st)
    y_prompt = h

    h = x_sample
    st_s = []
    for l in range(DEPTH):
        h, st = run_layer(h, c_sample, l, p, fox_s, state_conv[l], state_ssm[l])
        st_s.append(st)
    y_sample = h

    k_p = jnp.stack([s[0] for s in st_p])
    v_p = jnp.stack([s[1] for s in st_p])
    lf_p = jnp.stack([s[2] for s in st_p])
    conv_p = jnp.stack([s[3] for s in st_p])
    ssm_p = jnp.stack([s[4] for s in st_p])
    k_s = jnp.stack([s[0] for s in st_s])
    v_s = jnp.stack([s[1] for s in st_s])
    lf_s = jnp.stack([s[2] for s in st_s])
    conv_s = jnp.stack([s[3] for s in st_s])
    ssm_s = jnp.stack([s[4] for s in st_s])
    return (y_prompt, y_sample, k_p, v_p, lf_p, conv_p, ssm_p, k_s, v_s, lf_s, conv_s, ssm_s)
```

```python
import functools

import numpy as np
import jax
import jax.numpy as jnp
from jax import lax
from jax.experimental import pallas as pl
from jax.experimental.pallas import tpu as pltpu

F32 = jnp.float32
BF16 = jnp.bfloat16

D_MODEL = 1024
D_FF = 2816
P_A = 64
H_A = 16
G_A = 2
N_A = 128
CONV_W = 4
CONV_DIM = D_MODEL + 2 * G_A * N_A
HD_F = 64
H_F = 16
KV_F = 8
N_MOD = 9
RESID_HALF = 0.5
EPS = 1e-6
ATTN_SCALE = HD_F ** -0.5
PAGE = 128
CHUNK = 128
NEG = -1e30

LANE = 128
FF_TILE = 256
OFF_Z = 0
OFF_X = OFF_Z + D_MODEL
OFF_Q = OFF_X + CONV_DIM
OFF_K = OFF_Q + H_F * HD_F
OFF_V = OFF_K + KV_F * HD_F
OFF_GA = OFF_V + KV_F * HD_F
OFF_GB = OFF_GA + D_MODEL
OFF_S = OFF_GB + D_MODEL
IN_W = OFF_S + LANE
HEAD_PERM = tuple(4 * j + o for j in range(4) for o in (0, 2, 1, 3))
VMEM_LIMIT = 56 * 1024 * 1024


def _params(*sem):
    return pltpu.CompilerParams(dimension_semantics=sem, vmem_limit_bytes=VMEM_LIMIT)


def _resident(shape, index_map):
    return pl.BlockSpec(shape, index_map, pipeline_mode=pl.Buffered(1))


def _sigmoid(x):
    return 1.0 / (1.0 + jnp.exp(-x))


def _norm_mod(x, nw, shift, scale):
    y = x * lax.rsqrt(jnp.mean(x * x, axis=-1, keepdims=True) + EPS) * nw
    return y * (1.0 + scale) + shift


def _split3(x):
    hi = x.astype(BF16)
    r = x - hi.astype(F32)
    mid = r.astype(BF16)
    lo = (r - mid.astype(F32)).astype(BF16)
    return hi, mid, lo


def _dot01_l(m01, x):
    hi, mid, lo = _split3(x)
    d = lambda p: jnp.dot(m01, p, preferred_element_type=F32)
    return d(hi) + d(mid) + d(lo)


def _dot01_r(x, m01):
    hi, mid, lo = _split3(x)
    d = lambda p: jnp.dot(p, m01, preferred_element_type=F32)
    return d(hi) + d(mid) + d(lo)


def _dot_nt(a, b):
    return lax.dot_general(a, b, (((1,), (1,)), ((), ())), preferred_element_type=F32)


def _ada_kernel(c_ref, w_ref, b_ref, o_ref):
    c = c_ref[...]
    a = (c * _sigmoid(c)).astype(BF16)
    o_ref[...] = jnp.dot(a, w_ref[...].astype(BF16), preferred_element_type=F32) + b_ref[...]


def _ada(c_all, w_ada, b_ada):
    depth, d, n = w_ada.shape
    rows = c_all.shape[0]
    tn = 1024
    return pl.pallas_call(
        _ada_kernel,
        grid=(depth, n // tn),
        in_specs=[pl.BlockSpec((rows, d), lambda l, j: (0, 0)),
                  pl.BlockSpec((None, d, tn), lambda l, j: (l, 0, j)),
                  pl.BlockSpec((None, 1, tn), lambda l, j: (l, 0, j))],
        out_specs=pl.BlockSpec((None, rows, tn), lambda l, j: (l, 0, j)),
        out_shape=jax.ShapeDtypeStruct((depth, rows, n), F32),
        compiler_params=_params("parallel", "parallel"),
        name="ada",
    )(c_all, w_ada, b_ada.reshape(depth, 1, n))


def _ffn_kernel(x_ref, nw_ref, sh_ref, sc_ref, gt_ref, wg_ref, wu_ref, wo_ref, o_ref, u_ref, acc_ref):
    x = x_ref[...]
    u_ref[...] = _norm_mod(x, nw_ref[...], sh_ref[...], sc_ref[...]).astype(BF16)
    for c in range(wg_ref.shape[0]):
        u = u_ref[...]
        g = jnp.dot(u, wg_ref[c], preferred_element_type=F32)
        up = jnp.dot(u, wu_ref[c], preferred_element_type=F32)
        act = ((g * _sigmoid(g)) * up).astype(BF16)
        part = jnp.dot(act, wo_ref[c], preferred_element_type=F32)
        if c == 0:
            acc_ref[...] = part
        else:
            acc_ref[...] += part
    o_ref[...] = x + (RESID_HALF * gt_ref[...]) * acc_ref[...]


def _mod_spec(mod_rows, tm, tiles_per_seq, k):
    if mod_rows == 1:
        return pl.BlockSpec((None, None, 1, D_MODEL), lambda i: (i // tiles_per_seq, k, 0, 0))
    return pl.BlockSpec((None, tm, D_MODEL), lambda i: (k, i, 0))


def _ffn(h, mod, kmod, nw, wg, wu, wo, tm, tiles_per_seq, mod_rows):
    m = h.shape[0]
    nc = wg.shape[0]
    row = pl.BlockSpec((tm, D_MODEL), lambda i: (i, 0))
    return pl.pallas_call(
        _ffn_kernel,
        grid=(m // tm,),
        in_specs=[row,
                  pl.BlockSpec((1, D_MODEL), lambda i: (0, 0)),
                  _mod_spec(mod_rows, tm, tiles_per_seq, kmod),
                  _mod_spec(mod_rows, tm, tiles_per_seq, kmod + 1),
                  _mod_spec(mod_rows, tm, tiles_per_seq, kmod + 2),
                  _resident((nc, D_MODEL, FF_TILE), lambda i: (0, 0, 0)),
                  _resident((nc, D_MODEL, FF_TILE), lambda i: (0, 0, 0)),
                  _resident((nc, FF_TILE, D_MODEL), lambda i: (0, 0, 0))],
        out_specs=row,
        out_shape=jax.ShapeDtypeStruct((m, D_MODEL), F32),
        scratch_shapes=[pltpu.VMEM((tm, D_MODEL), BF16), pltpu.VMEM((tm, D_MODEL), F32)],
        compiler_params=_params("parallel"),
        name="ffn",
    )(h, nw, mod, mod, mod, wg, wu, wo)


def _head_norm(x, g256, w):
    sq = x * x
    hi = sq.astype(BF16)
    lo = (sq - hi.astype(F32)).astype(BF16)
    ss = jnp.dot(hi, g256, preferred_element_type=F32) + jnp.dot(lo, g256, preferred_element_type=F32)
    return x * lax.rsqrt(ss * (1.0 / HD_F) + EPS) * w


def _inproj_kernel(x_ref, nw_ref, sh_ref, sc_ref, w_ref, g_ref, qnw_ref, knw_ref, sb_ref,
                   z_ref, xbc_ref, q_ref, k_ref, kb_ref, v_ref, vb_ref, ga_ref, gb_ref, sm_ref, u_ref):
    u_ref[...] = _norm_mod(x_ref[...], nw_ref[...], sh_ref[...], sc_ref[...]).astype(BF16)

    def mm(lo, n):
        return jnp.dot(u_ref[...], w_ref[:, lo:lo + n], preferred_element_type=F32)

    t = 256
    for c in range(D_MODEL // t):
        z_ref[:, c * t:(c + 1) * t] = mm(OFF_Z + c * t, t)
        ga_ref[:, c * t:(c + 1) * t] = mm(OFF_GA + c * t, t)
        gb_ref[:, c * t:(c + 1) * t] = mm(OFF_GB + c * t, t)
        qn = _head_norm(mm(OFF_Q + c * t, t), g_ref[...], qnw_ref[...])
        q_ref[:, c * t:(c + 1) * t] = (qn * ATTN_SCALE).astype(q_ref.dtype)
    for c in range(CONV_DIM // t):
        xbc_ref[:, c * t:(c + 1) * t] = mm(OFF_X + c * t, t)
    for c in range(KV_F * HD_F // t):
        kn = _head_norm(mm(OFF_K + c * t, t), g_ref[...], knw_ref[...])
        k_ref[:, c * t:(c + 1) * t] = kn
        kb_ref[:, c * t:(c + 1) * t] = kn.astype(BF16)
        vv = mm(OFF_V + c * t, t)
        v_ref[:, c * t:(c + 1) * t] = vv
        vb_ref[:, c * t:(c + 1) * t] = vv.astype(BF16)
    raw = mm(OFF_S, LANE) + sb_ref[...]
    tail = jnp.log1p(jnp.exp(-jnp.abs(raw)))
    lane = lax.broadcasted_iota(jnp.int32, raw.shape, 1)
    softplus = jnp.maximum(raw, 0.0) + tail
    log_sig = jnp.minimum(raw, 0.0) - tail
    sm_ref[...] = jnp.where(lane < H_A, softplus, jnp.where(lane < H_A + H_F, log_sig, 0.0))


def _inproj(h, mod, nw, w, g256, qnw, knw, sbias, tm, tiles_per_seq, mod_rows, q_dtype):
    m = h.shape[0]
    row = lambda n: pl.BlockSpec((tm, n), lambda i: (i, 0))
    const = lambda a: _resident(a.shape, lambda i: (0,) * a.ndim)
    kv = KV_F * HD_F
    outs = [(D_MODEL, F32), (CONV_DIM, F32), (H_F * HD_F, q_dtype), (kv, F32), (kv, BF16),
            (kv, F32), (kv, BF16), (D_MODEL, F32), (D_MODEL, F32), (LANE, F32)]
    return pl.pallas_call(
        _inproj_kernel,
        grid=(m // tm,),
        in_specs=[row(D_MODEL),
                  pl.BlockSpec((1, D_MODEL), lambda i: (0, 0)),
                  _mod_spec(mod_rows, tm, tiles_per_seq, 3),
                  _mod_spec(mod_rows, tm, tiles_per_seq, 4),
                  const(w), const(g256), const(qnw), const(knw), const(sbias)],
        out_specs=[row(n) for n, _ in outs],
        out_shape=[jax.ShapeDtypeStruct((m, n), dt) for n, dt in outs],
        scratch_shapes=[pltpu.VMEM((tm, D_MODEL), BF16)],
        compiler_params=_params("parallel"),
        name="inproj",
    )(h, nw, mod, mod, w, g256, qnw, knw, sbias)


def _fcum_kernel(sm_ref, tri_ref, col_ref, row_ref, carry_ref):
    @pl.when(pl.program_id(1) == 0)
    def _():
        carry_ref[...] = jnp.zeros_like(carry_ref)

    cum = _dot01_l(tri_ref[...], sm_ref[...]) + carry_ref[0:1, :]
    col_ref[...] = cum
    row_ref[...] = cum.T
    carry_ref[0:1, :] = cum[cum.shape[0] - 1:, :]


def _fcum(small, tri, bsz, s_len):
    t = tri.shape[0]
    nt = s_len // t
    return pl.pallas_call(
        _fcum_kernel,
        grid=(bsz, nt),
        in_specs=[pl.BlockSpec((t, LANE), lambda b, i: (b * nt + i, 0)),
                  _resident((t, t), lambda b, i: (0, 0))],
        out_specs=[pl.BlockSpec((t, LANE), lambda b, i: (b * nt + i, 0)),
                   pl.BlockSpec((None, LANE, t), lambda b, i: (b, 0, i))],
        out_shape=[jax.ShapeDtypeStruct((bsz * s_len, LANE), F32),
                   jax.ShapeDtypeStruct((bsz, LANE, s_len), F32)],
        scratch_shapes=[pltpu.VMEM((8, LANE), F32)],
        compiler_params=_params("parallel", "arbitrary"),
        name="fcum",
    )(small, tri)


def _mamba_kernel(xbc_ref, z_ref, sm_ref, cprev_ref, h0_ref, cw_ref, cb_ref, alog_ref, dsk_ref, gnw_ref,
                  tri_ref, e_ref, ya_ref, cnew_ref, ssm_ref, hist_ref, dt_ref, *, rows, valid):
    q = CHUNK
    c = pl.program_id(1)
    di = D_MODEL

    @pl.when(c == 0)
    def _():
        hist_ref[0:8, :] = cprev_ref[...]
        ssm_ref[...] = h0_ref[...]
        if rows < q:
            hist_ref[8 + rows:, :] = jnp.zeros((q - rows, CONV_DIM), F32)
            dt_ref[...] = jnp.zeros_like(dt_ref)

    hist_ref[8:8 + rows, :] = xbc_ref[...]
    conv = cb_ref[...] + hist_ref[5:5 + q, :] * cw_ref[0:1, :]
    for j in range(1, CONV_W):
        conv = conv + hist_ref[5 + j:5 + j + q, :] * cw_ref[j:j + 1, :]
    cnew_ref[...] = hist_ref[5 + valid:8 + valid, :]
    hist_ref[5:8, :] = hist_ref[5 + q:8 + q, :]
    act = conv * _sigmoid(conv)
    xs = act[:, :di]
    bmat = [act[:, di + g * N_A:di + (g + 1) * N_A].astype(BF16) for g in range(G_A)]
    cmat = [act[:, di + (G_A + g) * N_A:di + (G_A + g + 1) * N_A].astype(BF16) for g in range(G_A)]

    lane = lax.broadcasted_iota(jnp.int32, (1, LANE), 1)
    a_row = jnp.where(lane < H_A, -jnp.exp(alog_ref[...]), 0.0)
    if rows < q:
        rid = lax.broadcasted_iota(jnp.int32, (rows, LANE), 0)
        dt_ref[0:rows, :] = jnp.where(rid < valid, sm_ref[...], 0.0)
        dt = dt_ref[...]
    else:
        dt = sm_ref[...]
    tri = tri_ref[...]
    acum = _dot01_l(tri, dt * a_row)
    acum_t = acum.T
    expand = e_ref[...]
    dt_e = _dot01_r(dt, expand)
    acum_e = _dot01_r(acum, expand)
    xdt = xs * dt_e

    rq = lax.broadcasted_iota(jnp.int32, (q, q), 0)
    rs = lax.broadcasted_iota(jnp.int32, (q, q), 1)
    causal = rs <= rq
    half = lax.broadcasted_iota(jnp.int32, (q, LANE), 1) < P_A
    cb = [_dot_nt(cmat[g], bmat[g]) for g in range(G_A)]
    xdt_b = xdt.astype(BF16)
    y_parts = []
    for pair in range(H_A // 2):
        res = []
        for o in range(2):
            h = 2 * pair + o
            seg = acum[:, h:h + 1] - acum_t[h:h + 1, :]
            mix = (cb[h // (H_A // G_A)] * jnp.where(causal, jnp.exp(seg), 0.0)).astype(BF16)
            res.append(jnp.dot(mix, xdt_b[:, pair * LANE:(pair + 1) * LANE], preferred_element_type=F32))
        y_parts.append(jnp.where(half, res[0], res[1]))
    y = jnp.concatenate(y_parts, axis=1)

    hg = di // G_A
    state = ssm_ref[...]
    state_b = state.astype(BF16)
    y_off = jnp.concatenate([_dot_nt(cmat[g], state_b[g * hg:(g + 1) * hg, :]) for g in range(G_A)], axis=1)
    y = y + y_off * jnp.exp(acum_e)
    xdt_t = xdt.T
    acum_et = acum_e.T
    a_end = acum_et[:, q - 1:q]
    xdtd_t = (xdt_t * jnp.exp(a_end - acum_et)).astype(BF16)
    upd = jnp.concatenate(
        [jnp.dot(xdtd_t[g * hg:(g + 1) * hg, :], bmat[g], preferred_element_type=F32) for g in range(G_A)], axis=0)
    ssm_ref[...] = jnp.exp(a_end) * state + upd

    y = (y + dsk_ref[...] * xs)[0:rows, :]
    zz = z_ref[...]
    gated = y * (zz * _sigmoid(zz))
    out = gated * lax.rsqrt(jnp.mean(gated * gated, axis=-1, keepdims=True) + EPS) * gnw_ref[...]
    ya_ref[...] = out.astype(ya_ref.dtype)


def _mamba(xbc, z, small, cprev8, h0, cw8, cb, alog, dsk, gnw, tri, expand, nseq, nchunk, rows, valid, ya_dtype):
    m = xbc.shape[0]
    row = lambda n: pl.BlockSpec((rows, n), lambda b, c: (b * nchunk + c, 0))
    const = lambda a: _resident(a.shape, lambda b, c: (0,) * a.ndim)
    di = D_MODEL
    kern = functools.partial(_mamba_kernel, rows=rows, valid=valid)
    return pl.pallas_call(
        kern,
        grid=(nseq, nchunk),
        in_specs=[row(CONV_DIM), row(di), row(LANE),
                  pl.BlockSpec((None, 8, CONV_DIM), lambda b, c: (b, 0, 0)),
                  pl.BlockSpec((None, di, N_A), lambda b, c: (b, 0, 0)),
                  const(cw8), const(cb), const(alog), const(dsk), const(gnw), const(tri), const(expand)],
        out_specs=[row(di),
                   pl.BlockSpec((None, CONV_W - 1, CONV_DIM), lambda b, c: (b, 0, 0)),
                   pl.BlockSpec((None, di, N_A), lambda b, c: (b, 0, 0))],
        out_shape=[jax.ShapeDtypeStruct((m, di), ya_dtype),
                   jax.ShapeDtypeStruct((nseq, CONV_W - 1, CONV_DIM), F32),
                   jax.ShapeDtypeStruct((nseq, di, N_A), F32)],
        scratch_shapes=[pltpu.VMEM((8 + CHUNK, CONV_DIM), F32), pltpu.VMEM((CHUNK, LANE), F32)],
        compiler_params=_params("parallel", "arbitrary"),
        name="mamba",
    )(xbc, z, small, cprev8, h0, cw8, cb, alog, dsk, gnw, tri, expand)


def _attn_kernel(q_ref, k_ref, v_ref, fcol_ref, frow_ref, o_ref, m_ref, l_ref, acc_ref, *, tq, tk):
    qi = pl.program_id(1)
    ki = pl.program_id(2)
    nk = pl.num_programs(2)

    @pl.when(ki == 0)
    def _():
        m_ref[...] = jnp.full_like(m_ref, NEG)
        l_ref[...] = jnp.zeros_like(l_ref)
        acc_ref[...] = jnp.zeros_like(acc_ref)

    @pl.when(ki <= qi)
    def _():
        rowi = lax.broadcasted_iota(jnp.int32, (tq, tk), 0) + qi * tq
        coli = lax.broadcasted_iota(jnp.int32, (tq, tk), 1) + ki * tk
        causal = coli <= rowi
        lo = lax.broadcasted_iota(jnp.int32, (tq, LANE), 1) < HD_F
        for blk in range(H_F // 2):
            pair = blk // 2
            qb = q_ref[:, blk * LANE:(blk + 1) * LANE]
            kp = k_ref[:, pair * LANE:(pair + 1) * LANE]
            vp = v_ref[:, pair * LANE:(pair + 1) * LANE]
            for o in range(2):
                slot = 2 * blk + o
                hq = HEAD_PERM[slot]
                qm = jnp.where(lo if o == 0 else jnp.logical_not(lo), qb, jnp.zeros_like(qb))
                s = _dot_nt(qm, kp)
                s = s + (fcol_ref[:, H_A + hq:H_A + hq + 1] - frow_ref[hq:hq + 1, :])
                s = jnp.where(causal, s, NEG)
                m_old = m_ref[:, slot:slot + 1]
                m_new = jnp.maximum(m_old, jnp.max(s, axis=-1, keepdims=True))
                alpha = jnp.exp(m_old - m_new)
                p = jnp.exp(s - m_new)
                l_ref[:, slot:slot + 1] = alpha * l_ref[:, slot:slot + 1] + jnp.sum(p, axis=-1, keepdims=True)
                m_ref[:, slot:slot + 1] = m_new
                acc_ref[slot] = alpha * acc_ref[slot] + jnp.dot(p.astype(BF16), vp, preferred_element_type=F32)

    @pl.when(ki == nk - 1)
    def _():
        lo = lax.broadcasted_iota(jnp.int32, (tq, LANE), 1) < HD_F
        for blk in range(H_F // 2):
            a0 = acc_ref[2 * blk] / l_ref[:, 2 * blk:2 * blk + 1]
            a1 = acc_ref[2 * blk + 1] / l_ref[:, 2 * blk + 1:2 * blk + 2]
            o_ref[:, blk * LANE:(blk + 1) * LANE] = jnp.where(lo, a0, a1).astype(o_ref.dtype)


def _attn(qb, kb, vb, fcol, frow, bsz, s_len, tq):
    tk = tq
    nq = s_len // tq
    kern = functools.partial(_attn_kernel, tq=tq, tk=tk)
    kvw = KV_F * HD_F
    kv_spec = pl.BlockSpec((tk, kvw), lambda b, i, j: (b * nq + jnp.minimum(i, j), 0))
    return pl.pallas_call(
        kern,
        grid=(bsz, nq, nq),
        in_specs=[pl.BlockSpec((tq, D_MODEL), lambda b, i, j: (b * nq + i, 0)),
                  kv_spec, kv_spec,
                  pl.BlockSpec((tq, LANE), lambda b, i, j: (b * nq + i, 0)),
                  pl.BlockSpec((None, H_F, tk), lambda b, i, j: (b, 1, jnp.minimum(i, j)))],
        out_specs=pl.BlockSpec((tq, D_MODEL), lambda b, i, j: (b * nq + i, 0)),
        out_shape=jax.ShapeDtypeStruct((bsz * s_len, D_MODEL), BF16),
        scratch_shapes=[pltpu.VMEM((tq, LANE), F32), pltpu.VMEM((tq, LANE), F32),
                        pltpu.VMEM((H_F, tq, LANE), F32)],
        compiler_params=_params("parallel", "parallel", "arbitrary"),
        name="attn",
    )(qb, kb, vb, fcol, frow)


def _dec_kernel(pt_ref, q_ref, kn_ref, vn_ref, sm_ref, *rest, npp, tokens, valid):
    k_refs = rest[0:npp]
    v_refs = rest[npp:2 * npp]
    f_refs = rest[2 * npp:3 * npp]
    (e16_ref, e128_ref, tri_ref, su_ref, o_ref,
     qf_ref, qrow_ref, blk_ref, smp_ref, stat_ref, acc_ref) = rest[3 * npp:]
    g = pl.program_id(1)
    ng = pl.num_programs(1)

    def attend(kmat, vmat, bias):
        s = _dot_nt(kmat, qrow_ref[...]) + bias
        m_old = stat_ref[0:1, :]
        m_new = jnp.maximum(m_old, jnp.max(s, axis=0, keepdims=True))
        alpha = jnp.exp(m_old - m_new)
        p = jnp.exp(s - m_new)
        stat_ref[1:2, :] = alpha * stat_ref[1:2, :] + jnp.sum(p, axis=0, keepdims=True)
        stat_ref[0:1, :] = m_new
        alpha_col = jnp.broadcast_to(alpha, (LANE, LANE)).T[:, 0:1]
        acc_ref[...] = alpha_col * acc_ref[...] + jnp.dot(p.T.astype(BF16), vmat, preferred_element_type=F32)

    @pl.when(g == 0)
    def _():
        lo = lax.broadcasted_iota(jnp.int32, (tokens, LANE), 1) < HD_F
        qf_ref[...] = jnp.zeros_like(qf_ref)
        for blk in range(H_F // 2):
            pair = blk // 2
            qb = q_ref[:, blk * LANE:(blk + 1) * LANE]
            for o in range(2):
                slot = 2 * blk + o
                qm = jnp.where(lo if o == 0 else jnp.logical_not(lo), qb, 0.0)
                qf_ref[slot * tokens:(slot + 1) * tokens, pair * LANE:(pair + 1) * LANE] = qm
        qrow_ref[...] = qf_ref[...].astype(BF16)
        stat_ref[...] = jnp.zeros_like(stat_ref)
        stat_ref[0:1, :] = jnp.full((1, LANE), NEG, F32)
        acc_ref[...] = jnp.zeros_like(acc_ref)
        blk_ref[...] = jnp.zeros_like(blk_ref)
        smp_ref[...] = jnp.zeros_like(smp_ref)
        blk_ref[0, 0:tokens, :] = kn_ref[...]
        blk_ref[1, 0:tokens, :] = vn_ref[...]
        smp_ref[0:tokens, :] = sm_ref[...]
        g_new = _dot01_r(_dot01_l(tri_ref[...], smp_ref[...]), e128_ref[...])
        key = lax.broadcasted_iota(jnp.int32, (PAGE, LANE), 0)
        qtok = lax.broadcasted_iota(jnp.int32, (PAGE, LANE), 1) % tokens
        g_q = jnp.sum(jnp.where(key == qtok, g_new, 0.0), axis=0, keepdims=True)
        stat_ref[2:3, :] = g_q
        ok = jnp.logical_and(key <= qtok, key < valid)
        bias = jnp.where(ok, g_q - g_new, NEG)
        attend(blk_ref[0].astype(BF16), blk_ref[1].astype(BF16), bias)

    for i in reversed(range(npp)):
        lf_e = _dot01_r(f_refs[i][...], e16_ref[...])
        later = _dot01_l(su_ref[...], lf_e)
        bias = stat_ref[2:3, :] + stat_ref[3:4, :] + later
        attend(k_refs[i][...].astype(BF16), v_refs[i][...].astype(BF16), bias)
        stat_ref[3:4, :] = stat_ref[3:4, :] + later[0:1, :] + lf_e[0:1, :]

    @pl.when(g == ng - 1)
    def _():
        l_col = jnp.broadcast_to(stat_ref[1:2, :], (LANE, LANE)).T[:, 0:1]
        acc = acc_ref[...] / l_col
        lo = lax.broadcasted_iota(jnp.int32, (tokens, LANE), 1) < HD_F
        for blk in range(H_F // 2):
            pair = blk // 2
            a0 = acc[(2 * blk) * tokens:(2 * blk + 1) * tokens, pair * LANE:(pair + 1) * LANE]
            a1 = acc[(2 * blk + 1) * tokens:(2 * blk + 2) * tokens, pair * LANE:(pair + 1) * LANE]
            o_ref[:, blk * LANE:(blk + 1) * LANE] = jnp.where(lo, a0, a1)


def _dec_attn(page_table, q, k_new, v_new, small, cache_k, cache_v, cache_f, layer, e16, e128, tri, striu,
              tokens, valid, npp):
    nseq, n_pages = page_table.shape
    ng = n_pages // npp
    kvw = KV_F * HD_F
    kern = functools.partial(_dec_kernel, npp=npp, tokens=tokens, valid=valid)
    row = lambda n: pl.BlockSpec((tokens, n), lambda b, g, pt: (b, 0))

    def page_spec(i, width):
        return pl.BlockSpec((None, None, PAGE, width),
                            lambda b, g, pt: (layer, pt[b, (ng - 1 - g) * npp + i], 0, 0))

    const = lambda a: _resident(a.shape, lambda b, g, pt: (0,) * a.ndim)
    in_specs = ([row(D_MODEL), row(kvw), row(kvw), row(LANE)]
                + [page_spec(i, kvw) for i in range(npp)]
                + [page_spec(i, kvw) for i in range(npp)]
                + [page_spec(i, H_F) for i in range(npp)]
                + [const(e16), const(e128), const(tri), const(striu)])
    grid_spec = pltpu.PrefetchScalarGridSpec(
        num_scalar_prefetch=1,
        grid=(nseq, ng),
        in_specs=in_specs,
        out_specs=pl.BlockSpec((tokens, D_MODEL), lambda b, g, pt: (b, 0)),
        scratch_shapes=[pltpu.VMEM((LANE, kvw), F32), pltpu.VMEM((LANE, kvw), BF16),
                        pltpu.VMEM((2, PAGE, kvw), F32), pltpu.VMEM((PAGE, LANE), F32),
                        pltpu.VMEM((8, LANE), F32), pltpu.VMEM((LANE, kvw), F32)])
    return pl.pallas_call(
        kern,
        grid_spec=grid_spec,
        out_shape=jax.ShapeDtypeStruct((nseq * tokens, D_MODEL), F32),
        compiler_params=_params("parallel", "arbitrary"),
        name="dec_attn",
    )(page_table, q, k_new, v_new, small, *([cache_k] * npp), *([cache_v] * npp), *([cache_f] * npp),
      e16, e128, tri, striu)


def _merge_kernel(h_ref, ya_ref, yb_ref, ga_ref, gb_ref, gt_ref, wa_ref, wb_ref, wo_ref, o_ref):
    pa = jnp.dot(ya_ref[...].astype(BF16), wa_ref[...], preferred_element_type=F32)
    pb = jnp.dot(yb_ref[...].astype(BF16), wb_ref[...], preferred_element_type=F32)
    merged = _sigmoid(ga_ref[...]) * pa + _sigmoid(gb_ref[...]) * pb
    out = jnp.dot(merged.astype(BF16), wo_ref[...], preferred_element_type=F32)
    o_ref[...] = h_ref[...] + gt_ref[...] * out


def _merge(h, ya, yb, ga, gb, mod, wa, wb, wo, tm, tiles_per_seq, mod_rows):
    m = h.shape[0]
    row = pl.BlockSpec((tm, D_MODEL), lambda i: (i, 0))
    wspec = _resident((D_MODEL, D_MODEL), lambda i: (0, 0))
    return pl.pallas_call(
        _merge_kernel,
        grid=(m // tm,),
        in_specs=[row, row, row, row, row, _mod_spec(mod_rows, tm, tiles_per_seq, 5), wspec, wspec, wspec],
        out_specs=row,
        out_shape=jax.ShapeDtypeStruct((m, D_MODEL), F32),
        compiler_params=_params("parallel"),
        name="merge",
    )(h, ya, yb, ga, gb, mod, wa, wb, wo)


def _constants():
    f = np.float32
    head = np.arange(D_MODEL) // HD_F
    g256 = (head[:256, None] == head[None, :256]).astype(f)
    tri128 = np.tril(np.ones((CHUNK, CHUNK), f))
    tri256 = np.tril(np.ones((256, 256), f))
    striu = np.triu(np.ones((PAGE, PAGE), f), 1)
    expand = np.zeros((LANE, D_MODEL), f)
    expand[head, np.arange(D_MODEL)] = 1.0
    return {k: jnp.asarray(v, BF16) for k, v in
            dict(g256=g256, tri128=tri128, tri256=tri256, striu=striu, expand=expand).items()}


def _head_to_lanes(tokens):
    e = np.zeros((H_F, LANE), np.float32)
    for slot, hq in enumerate(HEAD_PERM):
        e[hq, slot * tokens:(slot + 1) * tokens] = 1.0
    e128 = np.zeros((LANE, LANE), np.float32)
    e128[H_A:H_A + H_F] = e
    return jnp.asarray(e, BF16), jnp.asarray(e128, BF16)


def _ffn_weights(w_in, w_out):
    depth = w_in.shape[0]
    nc = D_FF // FF_TILE
    wb = w_in.astype(BF16)
    wg = wb[:, :, :D_FF].reshape(depth, D_MODEL, nc, FF_TILE).transpose(0, 2, 1, 3)
    wu = wb[:, :, D_FF:].reshape(depth, D_MODEL, nc, FF_TILE).transpose(0, 2, 1, 3)
    wo = w_out.astype(BF16).reshape(depth, nc, FF_TILE, D_MODEL)
    return wg, wu, wo


def _inproj_weights(w_in):
    depth = w_in.shape[0]
    sizes = (D_MODEL, CONV_DIM, H_A, H_F * HD_F, KV_F * HD_F, KV_F * HD_F, H_F, D_MODEL, D_MODEL)
    pts = np.cumsum(sizes)[:-1].tolist()
    wz, wx, wdt, wq, wk, wv, wf, wga, wgb = jnp.split(w_in, pts, axis=-1)
    wq = wq.reshape(depth, D_MODEL, H_F, HD_F)[:, :, list(HEAD_PERM), :].reshape(depth, D_MODEL, H_F * HD_F)
    pad = jnp.zeros((depth, D_MODEL, LANE - H_A - H_F), w_in.dtype)
    return jnp.concatenate([wz, wx, wq, wk, wv, wga, wgb, wdt, wf, pad], axis=-1).astype(BF16)


def _run_group(x2d, mod, mod_rows, tiles, conv_prev, ssm_prev, lw, consts, fox, nseq, nchunk, rows, valid,
               q_dtype, y_dtype):
    tm_ffn, tm_in, tm_mg, tps_ffn, tps_in, tps_mg = tiles
    h = x2d
    states = []
    depth = lw["wg1"].shape[0]
    for l in range(depth):
        ml = mod[l]
        h = _ffn(h, ml, 0, lw["nw"][l, 0:1], lw["wg1"][l], lw["wu1"][l], lw["wo1"][l], tm_ffn, tps_ffn, mod_rows)
        z, xbc, q, k, kb, v, vb, ga, gb, small = _inproj(
            h, ml, lw["nw"][l, 1:2], lw["w_in"][l], consts["g256"], lw["qnw"][l], lw["knw"][l], lw["sbias"][l],
            tm_in, tps_in, mod_rows, q_dtype)
        cprev8 = jnp.pad(conv_prev[l], ((0, 0), (8 - (CONV_W - 1), 0), (0, 0)))
        ya, conv_new, ssm_new = _mamba(
            xbc, z, small, cprev8, ssm_prev[l], lw["cw8"][l], lw["cb"][l], lw["alog"][l], lw["dsk"][l], lw["gnw"][l],
            consts["tri128"], consts["expand"], nseq, nchunk, rows, valid, y_dtype)
        yb = fox(l, q, k, kb, v, vb, small)
        h = _merge(h, ya, yb, ga, gb, ml, lw["wa"][l], lw["wb"][l], lw["wout"][l], tm_mg, tps_mg, mod_rows)
        h = _ffn(h, ml, 6, lw["nw"][l, 2:3], lw["wg2"][l], lw["wu2"][l], lw["wo2"][l], tm_ffn, tps_ffn, mod_rows)
        states.append((k, v, small, conv_new, ssm_new))
    return h, states


def kernel(x_prompt, x_sample, cache_k, cache_v, cache_logf, state_conv, state_ssm, page_table,
           c_prompt, c_sample, w_ada, b_ada, norm_w, w_ffn1_in, w_ffn1_out, w_ffn2_in, w_ffn2_out,
           w_in, conv_w, conv_b, dt_bias, a_log, d_skip, gn_w, q_norm_w, k_norm_w, b_f,
           w_proj_a, w_proj_b, w_out):
    bp, s_len, d = x_prompt.shape
    db, t_len, _ = x_sample.shape
    depth = w_in.shape[0]
    n_pool = cache_k.shape[1]
    kvw = KV_F * HD_F
    assert d == D_MODEL and s_len % CHUNK == 0 and t_len <= 8 and cache_k.shape[2] == PAGE
    consts = _constants()

    wg1, wu1, wo1 = _ffn_weights(w_ffn1_in, w_ffn1_out)
    wg2, wu2, wo2 = _ffn_weights(w_ffn2_in, w_ffn2_out)
    pad_lanes = lambda a: jnp.pad(a, ((0, 0), (0, LANE - a.shape[1])))[:, None, :]
    lw = dict(
        nw=norm_w, wg1=wg1, wu1=wu1, wo1=wo1, wg2=wg2, wu2=wu2, wo2=wo2,
        w_in=_inproj_weights(w_in),
        qnw=jnp.tile(q_norm_w, (1, 256 // HD_F))[:, None, :],
        knw=jnp.tile(k_norm_w, (1, 256 // HD_F))[:, None, :],
        sbias=pad_lanes(jnp.concatenate([dt_bias, b_f], axis=1)),
        cw8=jnp.pad(conv_w, ((0, 0), (0, 8 - CONV_W), (0, 0))),
        cb=conv_b[:, None, :],
        alog=pad_lanes(a_log),
        dsk=jnp.repeat(d_skip, P_A, axis=1)[:, None, :],
        gnw=gn_w[:, None, :],
        wa=w_proj_a.astype(BF16),
        wb=w_proj_b.reshape(depth, H_F, HD_F, d)[:, list(HEAD_PERM)].reshape(depth, d, d).astype(BF16),
        wout=w_out.astype(BF16),
    )

    mod_all = _ada(jnp.concatenate([c_prompt, c_sample], axis=0), w_ada, b_ada)
    mod_all = mod_all.reshape(depth, bp + db, N_MOD, d)
    mod_p = mod_all[:, :bp, :, None, :]
    tok = 8
    mod_s = jnp.broadcast_to(mod_all[:, bp:, None], (depth, db, tok, N_MOD, d))
    mod_s = mod_s.reshape(depth, db * tok, N_MOD, d).transpose(0, 2, 1, 3)

    tq = min(512, s_len)

    def fox_p(l, q, k, kb, v, vb, small):
        fcol, frow = _fcum(small, consts["tri256"], bp, s_len)
        return _attn(q, kb, vb, fcol, frow, bp, s_len, tq)

    tm = min(512, s_len)
    tm_in = min(256, s_len)
    tiles_p = (tm, tm_in, tm, s_len // tm, s_len // tm_in, s_len // tm)
    conv0 = jnp.zeros((depth, bp, CONV_W - 1, CONV_DIM), F32)
    ssm0 = jnp.zeros((depth, bp, D_MODEL, N_A), F32)
    y_p, st_p = _run_group(x_prompt.reshape(bp * s_len, d), mod_p, 1, tiles_p, conv0, ssm0, lw, consts, fox_p,
                           bp, s_len // CHUNK, CHUNK, CHUNK, BF16, BF16)

    cache_k4 = cache_k.reshape(depth, n_pool, PAGE, kvw)
    cache_v4 = cache_v.reshape(depth, n_pool, PAGE, kvw)
    e16, e128 = _head_to_lanes(tok)
    npp = 8 if page_table.shape[1] % 8 == 0 else 1

    def fox_s(l, q, k, kb, v, vb, small):
        return _dec_attn(page_table, q, k, v, small, cache_k4, cache_v4, cache_logf, l, e16, e128, consts["tri128"],
                         consts["striu"], tok, t_len, npp)

    ms = db * tok
    tiles_s = (ms, ms, ms, 1, 1, 1)
    x_s = jnp.pad(x_sample, ((0, 0), (0, tok - t_len), (0, 0))).reshape(ms, d)
    y_s, st_s = _run_group(x_s, mod_s, ms, tiles_s, state_conv, state_ssm.reshape(depth, db, D_MODEL, N_A), lw, consts,
                           fox_s, db, 1, tok, t_len, F32, F32)

    def gather(states, nseq, length, keep):
        k = jnp.stack([s[0] for s in states]).reshape(depth, nseq, length, KV_F, HD_F)[:, :, :keep]
        v = jnp.stack([s[1] for s in states]).reshape(depth, nseq, length, KV_F, HD_F)[:, :, :keep]
        lf = jnp.stack([s[2][:, H_A:H_A + H_F] for s in states]).reshape(depth, nseq, length, H_F)[:, :, :keep]
        conv = jnp.stack([s[3] for s in states])
        ssm = jnp.stack([s[4] for s in states]).reshape(depth, nseq, H_A, P_A, N_A)
        return k, v, lf, conv, ssm

    k_p, v_p, lf_p, conv_p, ssm_p = gather(st_p, bp, s_len, s_len)
    k_s, v_s, lf_s, conv_s, ssm_s = gather(st_s, db, tok, t_len)
    y_prompt = y_p.reshape(bp, s_len, d)
    y_sample = y_s.reshape(db, tok, d)[:, :t_len]
    return (y_prompt, y_sample, k_p, v_p, lf_p, conv_p, ssm_p, k_s, v_s, lf_s, conv_s, ssm_s)
```

```python
import functools

import numpy as np
import jax
import jax.numpy as jnp
from jax import lax
from jax.experimental import pallas as pl
from jax.experimental.pallas import tpu as pltpu

F32 = jnp.float32
BF16 = jnp.bfloat16

D_MODEL = 1024
D_FF = 2816
P_A = 64
H_A = 16
G_A = 2
N_A = 128
CONV_W = 4
CONV_DIM = D_MODEL + 2 * G_A * N_A
HD_F = 64
H_F = 16
KV_F = 8
N_MOD = 9
RESID_HALF = 0.5
EPS = 1e-6
ATTN_SCALE = HD_F ** -0.5
PAGE = 128
CHUNK = 128
NEG = -1e30
LOG2E = 1.4426950408889634
F_LANES = 6
DEC_ROWS = PAGE * KV_F

LANE = 128
FF_TILE = 256
OFF_Z = 0
OFF_X = OFF_Z + D_MODEL
OFF_Q = OFF_X + CONV_DIM
OFF_K = OFF_Q + H_F * HD_F
OFF_V = OFF_K + KV_F * HD_F
OFF_GA = OFF_V + KV_F * HD_F
OFF_GB = OFF_GA + D_MODEL
OFF_S = OFF_GB + D_MODEL
IN_W = OFF_S + LANE
HEAD_PERM = tuple(4 * j + o for j in range(4) for o in (0, 2, 1, 3))
VMEM_LIMIT = 56 * 1024 * 1024


def _params(*sem):
    return pltpu.CompilerParams(dimension_semantics=sem, vmem_limit_bytes=VMEM_LIMIT)


def _resident(shape, index_map):
    return pl.BlockSpec(shape, index_map, pipeline_mode=pl.Buffered(1))


def _sigmoid(x):
    return 1.0 / (1.0 + jnp.exp(-x))


def _norm_mod(x, nw, shift, scale):
    y = x * lax.rsqrt(jnp.mean(x * x, axis=-1, keepdims=True) + EPS) * nw
    return y * (1.0 + scale) + shift


def _split3(x):
    hi = x.astype(BF16)
    r = x - hi.astype(F32)
    mid = r.astype(BF16)
    lo = (r - mid.astype(F32)).astype(BF16)
    return hi, mid, lo


def _dot01_l(m01, x):
    hi, mid, lo = _split3(x)
    d = lambda p: jnp.dot(m01, p, preferred_element_type=F32)
    return d(hi) + d(mid) + d(lo)


def _dot01_r(x, m01):
    hi, mid, lo = _split3(x)
    d = lambda p: jnp.dot(p, m01, preferred_element_type=F32)
    return d(hi) + d(mid) + d(lo)


def _dot_nt(a, b):
    return lax.dot_general(a, b, (((1,), (1,)), ((), ())), preferred_element_type=F32)


def _ada_kernel(c_ref, w_ref, b_ref, o_ref):
    c = c_ref[...]
    a = (c * _sigmoid(c)).astype(BF16)
    o_ref[...] = jnp.dot(a, w_ref[...].astype(BF16), preferred_element_type=F32) + b_ref[...]


def _ada(c_all, w_ada, b_ada):
    depth, d, n = w_ada.shape
    rows = c_all.shape[0]
    tn = 1024
    return pl.pallas_call(
        _ada_kernel,
        grid=(depth, n // tn),
        in_specs=[pl.BlockSpec((rows, d), lambda l, j: (0, 0)),
                  pl.BlockSpec((None, d, tn), lambda l, j: (l, 0, j)),
                  pl.BlockSpec((None, 1, tn), lambda l, j: (l, 0, j))],
        out_specs=pl.BlockSpec((None, rows, tn), lambda l, j: (l, 0, j)),
        out_shape=jax.ShapeDtypeStruct((depth, rows, n), F32),
        compiler_params=_params("parallel", "parallel"),
        name="ada",
    )(c_all, w_ada, b_ada.reshape(depth, 1, n))


def _ffn_kernel(x_ref, nw_ref, sh_ref, sc_ref, gt_ref, wg_ref, wu_ref, wo_ref, o_ref, u_ref, acc_ref):
    x = x_ref[...]
    u_ref[...] = _norm_mod(x, nw_ref[...], sh_ref[...], sc_ref[...]).astype(BF16)
    for c in range(wg_ref.shape[0]):
        u = u_ref[...]
        g = jnp.dot(u, wg_ref[c], preferred_element_type=F32)
        up = jnp.dot(u, wu_ref[c], preferred_element_type=F32)
        act = ((g * _sigmoid(g)) * up).astype(BF16)
        part = jnp.dot(act, wo_ref[c], preferred_element_type=F32)
        if c == 0:
            acc_ref[...] = part
        else:
            acc_ref[...] += part
    o_ref[...] = x + (RESID_HALF * gt_ref[...]) * acc_ref[...]


def _mod_spec(mod_rows, tm, tiles_per_seq, k):
    if mod_rows == 1:
        return pl.BlockSpec((None, None, 1, D_MODEL), lambda i: (i // tiles_per_seq, k, 0, 0))
    return pl.BlockSpec((None, tm, D_MODEL), lambda i: (k, i, 0))


def _ffn(h, mod, kmod, nw, wg, wu, wo, tm, tiles_per_seq, mod_rows):
    m = h.shape[0]
    nc = wg.shape[0]
    row = pl.BlockSpec((tm, D_MODEL), lambda i: (i, 0))
    return pl.pallas_call(
        _ffn_kernel,
        grid=(m // tm,),
        in_specs=[row,
                  pl.BlockSpec((1, D_MODEL), lambda i: (0, 0)),
                  _mod_spec(mod_rows, tm, tiles_per_seq, kmod),
                  _mod_spec(mod_rows, tm, tiles_per_seq, kmod + 1),
                  _mod_spec(mod_rows, tm, tiles_per_seq, kmod + 2),
                  _resident((nc, D_MODEL, FF_TILE), lambda i: (0, 0, 0)),
                  _resident((nc, D_MODEL, FF_TILE), lambda i: (0, 0, 0)),
                  _resident((nc, FF_TILE, D_MODEL), lambda i: (0, 0, 0))],
        out_specs=row,
        out_shape=jax.ShapeDtypeStruct((m, D_MODEL), F32),
        scratch_shapes=[pltpu.VMEM((tm, D_MODEL), BF16), pltpu.VMEM((tm, D_MODEL), F32)],
        compiler_params=_params("parallel"),
        name="ffn",
    )(h, nw, mod, mod, mod, wg, wu, wo)


def _head_norm(x, g256, w):
    sq = x * x
    hi = sq.astype(BF16)
    lo = (sq - hi.astype(F32)).astype(BF16)
    ss = jnp.dot(hi, g256, preferred_element_type=F32) + jnp.dot(lo, g256, preferred_element_type=F32)
    return x * lax.rsqrt(ss * (1.0 / HD_F) + EPS) * w


def _inproj_kernel(x_ref, nw_ref, sh_ref, sc_ref, w_ref, g_ref, qnw_ref, knw_ref, sb_ref,
                   z_ref, xbc_ref, q_ref, k_ref, kb_ref, v_ref, vb_ref, ga_ref, gb_ref, sm_ref, u_ref, *, q_scale):
    u_ref[...] = _norm_mod(x_ref[...], nw_ref[...], sh_ref[...], sc_ref[...]).astype(BF16)

    def mm(lo, n):
        return jnp.dot(u_ref[...], w_ref[:, lo:lo + n], preferred_element_type=F32)

    t = 256
    for c in range(D_MODEL // t):
        z_ref[:, c * t:(c + 1) * t] = mm(OFF_Z + c * t, t)
        ga_ref[:, c * t:(c + 1) * t] = mm(OFF_GA + c * t, t)
        gb_ref[:, c * t:(c + 1) * t] = mm(OFF_GB + c * t, t)
        qn = _head_norm(mm(OFF_Q + c * t, t), g_ref[...], qnw_ref[...])
        q_ref[:, c * t:(c + 1) * t] = (qn * q_scale).astype(q_ref.dtype)
    for c in range(CONV_DIM // t):
        xbc_ref[:, c * t:(c + 1) * t] = mm(OFF_X + c * t, t)
    for c in range(KV_F * HD_F // t):
        kn = _head_norm(mm(OFF_K + c * t, t), g_ref[...], knw_ref[...])
        k_ref[:, c * t:(c + 1) * t] = kn
        kb_ref[:, c * t:(c + 1) * t] = kn.astype(BF16)
        vv = mm(OFF_V + c * t, t)
        v_ref[:, c * t:(c + 1) * t] = vv
        vb_ref[:, c * t:(c + 1) * t] = vv.astype(BF16)
    raw = mm(OFF_S, LANE) + sb_ref[...]
    tail = jnp.log1p(jnp.exp(-jnp.abs(raw)))
    lane = lax.broadcasted_iota(jnp.int32, raw.shape, 1)
    softplus = jnp.maximum(raw, 0.0) + tail
    log_sig = jnp.minimum(raw, 0.0) - tail
    sm_ref[...] = jnp.where(lane < H_A, softplus, jnp.where(lane < H_A + H_F, log_sig, 0.0))


def _inproj(h, mod, nw, w, g256, qnw, knw, sbias, tm, tiles_per_seq, mod_rows, q_dtype, q_scale):
    m = h.shape[0]
    row = lambda n: pl.BlockSpec((tm, n), lambda i: (i, 0))
    const = lambda a: _resident(a.shape, lambda i: (0,) * a.ndim)
    kv = KV_F * HD_F
    outs = [(D_MODEL, F32), (CONV_DIM, F32), (H_F * HD_F, q_dtype), (kv, F32), (kv, BF16),
            (kv, F32), (kv, BF16), (D_MODEL, F32), (D_MODEL, F32), (LANE, F32)]
    return pl.pallas_call(
        functools.partial(_inproj_kernel, q_scale=q_scale),
        grid=(m // tm,),
        in_specs=[row(D_MODEL),
                  pl.BlockSpec((1, D_MODEL), lambda i: (0, 0)),
                  _mod_spec(mod_rows, tm, tiles_per_seq, 3),
                  _mod_spec(mod_rows, tm, tiles_per_seq, 4),
                  const(w), const(g256), const(qnw), const(knw), const(sbias)],
        out_specs=[row(n) for n, _ in outs],
        out_shape=[jax.ShapeDtypeStruct((m, n), dt) for n, dt in outs],
        scratch_shapes=[pltpu.VMEM((tm, D_MODEL), BF16)],
        compiler_params=_params("parallel"),
        name="inproj",
    )(h, nw, mod, mod, w, g256, qnw, knw, sbias)


def _fcum_kernel(sm_ref, kb_ref, tri_ref, pq_ref, pk_ref, oq_ref, ok_ref, fq_ref, kcat_ref, carry_ref):
    @pl.when(pl.program_id(1) == 0)
    def _():
        carry_ref[...] = jnp.zeros_like(carry_ref)

    cum = _dot01_l(tri_ref[...], sm_ref[...]) + carry_ref[0:1, :]
    carry_ref[0:1, :] = cum[cum.shape[0] - 1:, :]
    pieces = _split3(cum * LOG2E)

    def place(p_ref):
        return sum(jnp.dot(pieces[i], p_ref[i], preferred_element_type=F32) for i in range(3))

    fq_ref[...] = (place(pq_ref) + oq_ref[...]).astype(BF16)
    fk = (ok_ref[...] - place(pk_ref)).astype(BF16)
    for pair in range(KV_F // 2):
        kcat_ref[:, 2 * pair * LANE:(2 * pair + 1) * LANE] = kb_ref[:, pair * LANE:(pair + 1) * LANE]
        kcat_ref[:, (2 * pair + 1) * LANE:(2 * pair + 2) * LANE] = fk


def _fcum(small, kb, consts, bsz, s_len):
    tri = consts["tri256"]
    t = tri.shape[0]
    nt = s_len // t
    kvw = KV_F * HD_F
    row = lambda n: pl.BlockSpec((t, n), lambda b, i: (b * nt + i, 0))
    const = lambda a: _resident(a.shape, lambda b, i: (0,) * a.ndim)
    cs = [tri, consts["place_q"], consts["place_k"], consts["ones_q"], consts["ones_k"]]
    return pl.pallas_call(
        _fcum_kernel,
        grid=(bsz, nt),
        in_specs=[row(LANE), row(kvw)] + [const(a) for a in cs],
        out_specs=[row(LANE), row(2 * kvw)],
        out_shape=[jax.ShapeDtypeStruct((bsz * s_len, LANE), BF16),
                   jax.ShapeDtypeStruct((bsz * s_len, 2 * kvw), BF16)],
        scratch_shapes=[pltpu.VMEM((8, LANE), F32)],
        compiler_params=_params("parallel", "arbitrary"),
        name="fcum",
    )(small, kb, *cs)


def _mamba_kernel(xbc_ref, z_ref, sm_ref, cprev_ref, h0_ref, cw_ref, cb_ref, alog_ref, dsk_ref, gnw_ref,
                  tri_ref, e_ref, ya_ref, cnew_ref, ssm_ref, hist_ref, dt_ref, *, rows, valid):
    q = CHUNK
    c = pl.program_id(1)
    di = D_MODEL

    @pl.when(c == 0)
    def _():
        hist_ref[0:8, :] = cprev_ref[...]
        ssm_ref[...] = h0_ref[...]
        if rows < q:
            hist_ref[8 + rows:, :] = jnp.zeros((q - rows, CONV_DIM), F32)
            dt_ref[...] = jnp.zeros_like(dt_ref)

    hist_ref[8:8 + rows, :] = xbc_ref[...]
    conv = cb_ref[...] + hist_ref[5:5 + q, :] * cw_ref[0:1, :]
    for j in range(1, CONV_W):
        conv = conv + hist_ref[5 + j:5 + j + q, :] * cw_ref[j:j + 1, :]
    cnew_ref[...] = hist_ref[5 + valid:8 + valid, :]
    hist_ref[5:8, :] = hist_ref[5 + q:8 + q, :]
    act = conv * _sigmoid(conv)
    xs = act[:, :di]
    bmat = [act[:, di + g * N_A:di + (g + 1) * N_A].astype(BF16) for g in range(G_A)]
    cmat = [act[:, di + (G_A + g) * N_A:di + (G_A + g + 1) * N_A].astype(BF16) for g in range(G_A)]

    lane = lax.broadcasted_iota(jnp.int32, (1, LANE), 1)
    a_row = jnp.where(lane < H_A, -jnp.exp(alog_ref[...]), 0.0)
    if rows < q:
        rid = lax.broadcasted_iota(jnp.int32, (rows, LANE), 0)
        dt_ref[0:rows, :] = jnp.where(rid < valid, sm_ref[...], 0.0)
        dt = dt_ref[...]
    else:
        dt = sm_ref[...]
    tri = tri_ref[...]
    acum = _dot01_l(tri, dt * a_row)
    acum_t = acum.T
    expand = e_ref[...]
    dt_e = _dot01_r(dt, expand)
    acum_e = _dot01_r(acum, expand)
    xdt = xs * dt_e

    rq = lax.broadcasted_iota(jnp.int32, (q, q), 0)
    rs = lax.broadcasted_iota(jnp.int32, (q, q), 1)
    causal = rs <= rq
    half = lax.broadcasted_iota(jnp.int32, (q, LANE), 1) < P_A
    cb = [_dot_nt(cmat[g], bmat[g]) for g in range(G_A)]
    xdt_b = xdt.astype(BF16)
    y_parts = []
    for pair in range(H_A // 2):
        res = []
        for o in range(2):
            h = 2 * pair + o
            seg = acum[:, h:h + 1] - acum_t[h:h + 1, :]
            mix = (cb[h // (H_A // G_A)] * jnp.where(causal, jnp.exp(seg), 0.0)).astype(BF16)
            res.append(jnp.dot(mix, xdt_b[:, pair * LANE:(pair + 1) * LANE], preferred_element_type=F32))
        y_parts.append(jnp.where(half, res[0], res[1]))
    y = jnp.concatenate(y_parts, axis=1)

    hg = di // G_A
    state = ssm_ref[...]
    state_b = state.astype(BF16)
    y_off = jnp.concatenate([_dot_nt(cmat[g], state_b[g * hg:(g + 1) * hg, :]) for g in range(G_A)], axis=1)
    y = y + y_off * jnp.exp(acum_e)
    xdt_t = xdt.T
    acum_et = acum_e.T
    a_end = acum_et[:, q - 1:q]
    xdtd_t = (xdt_t * jnp.exp(a_end - acum_et)).astype(BF16)
    upd = jnp.concatenate(
        [jnp.dot(xdtd_t[g * hg:(g + 1) * hg, :], bmat[g], preferred_element_type=F32) for g in range(G_A)], axis=0)
    ssm_ref[...] = jnp.exp(a_end) * state + upd

    y = (y + dsk_ref[...] * xs)[0:rows, :]
    zz = z_ref[...]
    gated = y * (zz * _sigmoid(zz))
    out = gated * lax.rsqrt(jnp.mean(gated * gated, axis=-1, keepdims=True) + EPS) * gnw_ref[...]
    ya_ref[...] = out.astype(ya_ref.dtype)


def _mamba(xbc, z, small, cprev8, h0, cw8, cb, alog, dsk, gnw, tri, expand, nseq, nchunk, rows, valid, ya_dtype):
    m = xbc.shape[0]
    row = lambda n: pl.BlockSpec((rows, n), lambda b, c: (b * nchunk + c, 0))
    const = lambda a: _resident(a.shape, lambda b, c: (0,) * a.ndim)
    di = D_MODEL
    kern = functools.partial(_mamba_kernel, rows=rows, valid=valid)
    return pl.pallas_call(
        kern,
        grid=(nseq, nchunk),
        in_specs=[row(CONV_DIM), row(di), row(LANE),
                  pl.BlockSpec((None, 8, CONV_DIM), lambda b, c: (b, 0, 0)),
                  pl.BlockSpec((None, di, N_A), lambda b, c: (b, 0, 0)),
                  const(cw8), const(cb), const(alog), const(dsk), const(gnw), const(tri), const(expand)],
        out_specs=[row(di),
                   pl.BlockSpec((None, CONV_W - 1, CONV_DIM), lambda b, c: (b, 0, 0)),
                   pl.BlockSpec((None, di, N_A), lambda b, c: (b, 0, 0))],
        out_shape=[jax.ShapeDtypeStruct((m, di), ya_dtype),
                   jax.ShapeDtypeStruct((nseq, CONV_W - 1, CONV_DIM), F32),
                   jax.ShapeDtypeStruct((nseq, di, N_A), F32)],
        scratch_shapes=[pltpu.VMEM((8 + CHUNK, CONV_DIM), F32), pltpu.VMEM((CHUNK, LANE), F32)],
        compiler_params=_params("parallel", "arbitrary"),
        name="mamba",
    )(xbc, z, small, cprev8, h0, cw8, cb, alog, dsk, gnw, tri, expand)


def _attn_kernel(q_ref, fq_ref, k_ref, v_ref, o_ref, qcat_ref, m_ref, l_ref, acc_ref, *, tq, tk):
    qi = pl.program_id(1)
    ki = pl.program_id(2)
    nk = pl.num_programs(2)

    @pl.when(ki == 0)
    def _():
        m_ref[...] = jnp.full_like(m_ref, NEG)
        l_ref[...] = jnp.zeros_like(l_ref)
        acc_ref[...] = jnp.zeros_like(acc_ref)
        lane = lax.broadcasted_iota(jnp.int32, (tq, LANE), 1)
        lo = lane < HD_F
        fq = fq_ref[...]
        for blk in range(H_F // 2):
            qb = q_ref[:, blk * LANE:(blk + 1) * LANE]
            for o in range(2):
                slot = 2 * blk + o
                hq = HEAD_PERM[slot]
                own = jnp.logical_and(lane >= F_LANES * hq, lane < F_LANES * (hq + 1))
                qcat_ref[slot, :, 0:LANE] = jnp.where(lo if o == 0 else jnp.logical_not(lo), qb, jnp.zeros_like(qb))
                qcat_ref[slot, :, LANE:2 * LANE] = jnp.where(own, fq, jnp.zeros_like(fq))

    def step(diagonal):
        if diagonal:
            causal = (lax.broadcasted_iota(jnp.int32, (tq, tk), 1) <= lax.broadcasted_iota(jnp.int32, (tq, tk), 0))
        for slot in range(H_F):
            pair = slot // 4
            s = _dot_nt(qcat_ref[slot], k_ref[:, 2 * pair * LANE:(2 * pair + 2) * LANE])
            if diagonal:
                s = jnp.where(causal, s, NEG)
            m_old = m_ref[slot]
            m_new = jnp.maximum(m_old, jnp.max(s, axis=-1, keepdims=True))
            alpha = jnp.exp2(m_old - m_new)
            p = jnp.exp2(s - m_new)
            l_ref[slot] = alpha * l_ref[slot] + jnp.sum(p, axis=-1, keepdims=True)
            m_ref[slot] = m_new
            acc_ref[slot] = alpha * acc_ref[slot] + jnp.dot(
                p.astype(BF16), v_ref[:, pair * LANE:(pair + 1) * LANE], preferred_element_type=F32)

    pl.when(ki < qi)(functools.partial(step, False))
    pl.when(ki == qi)(functools.partial(step, True))

    @pl.when(ki == nk - 1)
    def _():
        lo = lax.broadcasted_iota(jnp.int32, (tq, LANE), 1) < HD_F
        for blk in range(H_F // 2):
            a0 = acc_ref[2 * blk] / l_ref[2 * blk]
            a1 = acc_ref[2 * blk + 1] / l_ref[2 * blk + 1]
            o_ref[:, blk * LANE:(blk + 1) * LANE] = jnp.where(lo, a0, a1).astype(o_ref.dtype)


def _attn(qb, fq, kcat, vb, bsz, s_len, tq):
    tk = tq
    nq = s_len // tq
    kern = functools.partial(_attn_kernel, tq=tq, tk=tk)
    kvw = KV_F * HD_F
    q_spec = lambda n: pl.BlockSpec((tq, n), lambda b, i, j: (b * nq + i, 0))
    k_spec = lambda n: pl.BlockSpec((tk, n), lambda b, i, j: (b * nq + jnp.minimum(i, j), 0))
    return pl.pallas_call(
        kern,
        grid=(bsz, nq, nq),
        in_specs=[q_spec(D_MODEL), q_spec(LANE), k_spec(2 * kvw), k_spec(kvw)],
        out_specs=q_spec(D_MODEL),
        out_shape=jax.ShapeDtypeStruct((bsz * s_len, D_MODEL), BF16),
        scratch_shapes=[pltpu.VMEM((H_F, tq, 2 * LANE), BF16),
                        pltpu.VMEM((H_F, tq, 1), F32), pltpu.VMEM((H_F, tq, 1), F32),
                        pltpu.VMEM((H_F, tq, LANE), F32)],
        compiler_params=_params("parallel", "parallel", "arbitrary"),
        name="attn",
    )(qb, fq, kcat, vb)


def _dec_kernel(pt_ref, q_ref, kn_ref, vn_ref, sm_ref, *rest, npp, tokens, valid):
    k_refs = rest[0:npp]
    v_refs = rest[npp:2 * npp]
    f_refs = rest[2 * npp:3 * npp]
    (e16_ref, e128_ref, tri_ref, su_ref, rep_ref, kvm_ref, o_ref, smp_ref, stat_ref, acc_ref) = rest[3 * npp:]
    g = pl.program_id(1)
    ng = pl.num_programs(1)
    q2 = q_ref[...].astype(BF16)

    def spread(x):
        hi = x.astype(BF16)
        mid = (x - hi.astype(F32)).astype(BF16)
        return jnp.dot(rep_ref[...], jnp.concatenate([hi, mid], axis=0), preferred_element_type=F32)

    def update(scores, values):
        m_old = stat_ref[0:1, :]
        m_new = m_old
        for s in scores:
            m_new = jnp.maximum(m_new, jnp.max(s, axis=0, keepdims=True))
        alpha = jnp.exp(m_old - m_new)
        l_add = jnp.zeros((1, LANE), F32)
        pv = jnp.zeros((LANE, HD_F), F32)
        for s, v2 in zip(scores, values):
            p = jnp.exp(s - m_new)
            l_add = l_add + jnp.sum(p, axis=0, keepdims=True)
            pv = pv + lax.dot_general(p.astype(BF16), v2.astype(BF16), (((0,), (0,)), ((), ())),
                                      preferred_element_type=F32)
        stat_ref[1:2, :] = alpha * stat_ref[1:2, :] + l_add
        stat_ref[0:1, :] = m_new
        alpha_col = jnp.broadcast_to(alpha, (LANE, LANE)).T[:, 0:1]
        acc_ref[...] = alpha_col * acc_ref[...] + pv

    @pl.when(g == 0)
    def _():
        stat_ref[...] = jnp.zeros_like(stat_ref)
        stat_ref[0:1, :] = jnp.full((1, LANE), NEG, F32)
        acc_ref[...] = jnp.zeros_like(acc_ref)
        smp_ref[...] = jnp.zeros_like(smp_ref)
        smp_ref[0:tokens, :] = sm_ref[...]
        g_new = _dot01_r(_dot01_l(tri_ref[...], smp_ref[...]), e128_ref[...])
        key = lax.broadcasted_iota(jnp.int32, (PAGE, LANE), 0)
        qtok = lax.broadcasted_iota(jnp.int32, (PAGE, LANE), 1) % tokens
        g_q = jnp.sum(jnp.where(key == qtok, g_new, 0.0), axis=0, keepdims=True)
        stat_ref[2:3, :] = g_q
        key_r = lax.broadcasted_iota(jnp.int32, (DEC_ROWS, LANE), 0) // KV_F
        qtok_r = lax.broadcasted_iota(jnp.int32, (DEC_ROWS, LANE), 1) % tokens
        ok = jnp.logical_and(key_r <= qtok_r, key_r < valid)
        s = _dot_nt(kn_ref[...].astype(BF16), q2) + spread(-g_new) + (kvm_ref[...] + g_q)
        update([jnp.where(ok, s, NEG)], [vn_ref[...]])

    base = kvm_ref[...] + (stat_ref[2:3, :] + stat_ref[3:4, :])
    carry = jnp.zeros((1, LANE), F32)
    scores, values = [], []
    for i in reversed(range(npp)):
        lf_e = _dot01_r(f_refs[i][...], e16_ref[...])
        later = _dot01_l(su_ref[...], lf_e)
        k2 = k_refs[i][...].reshape(DEC_ROWS, HD_F).astype(BF16)
        scores.append(_dot_nt(k2, q2) + spread(later + carry) + base)
        values.append(v_refs[i][...].reshape(DEC_ROWS, HD_F))
        carry = carry + later[0:1, :] + lf_e[0:1, :]
    update(scores, values)
    stat_ref[3:4, :] = stat_ref[3:4, :] + carry

    @pl.when(g == ng - 1)
    def _():
        l_col = jnp.broadcast_to(stat_ref[1:2, :], (LANE, LANE)).T[:, 0:1]
        o_ref[...] = acc_ref[...] / l_col


def _dec_attn(page_table, q2, k_new, v_new, small, cache_k, cache_v, cache_f, layer, dc, tokens, valid, npp):
    nseq, n_pages = page_table.shape
    ng = n_pages // npp
    kern = functools.partial(_dec_kernel, npp=npp, tokens=tokens, valid=valid)
    per_seq = lambda r: pl.BlockSpec((None, r, HD_F), lambda b, g, pt: (b, 0, 0))
    page_idx = lambda b, g, pt, i: pt[b, (ng - 1 - g) * npp + i]
    kv_page = lambda i: pl.BlockSpec((None, None, PAGE, KV_F, HD_F),
                                     lambda b, g, pt: (layer, page_idx(b, g, pt, i), 0, 0, 0))
    f_page = lambda i: pl.BlockSpec((None, None, PAGE, H_F), lambda b, g, pt: (layer, page_idx(b, g, pt, i), 0, 0))
    const = lambda a: _resident(a.shape, lambda b, g, pt: (0,) * a.ndim)
    cs = [dc["e16"], dc["e128"], dc["tri"], dc["striu"], dc["rep"], dc["kvmask"]]
    in_specs = ([per_seq(H_F * tokens), per_seq(DEC_ROWS), per_seq(DEC_ROWS),
                 pl.BlockSpec((tokens, LANE), lambda b, g, pt: (b, 0))]
                + [kv_page(i) for i in range(npp)] + [kv_page(i) for i in range(npp)]
                + [f_page(i) for i in range(npp)] + [const(a) for a in cs])
    grid_spec = pltpu.PrefetchScalarGridSpec(
        num_scalar_prefetch=1,
        grid=(nseq, ng),
        in_specs=in_specs,
        out_specs=per_seq(H_F * tokens),
        scratch_shapes=[pltpu.VMEM((PAGE, LANE), F32), pltpu.VMEM((8, LANE), F32),
                        pltpu.VMEM((H_F * tokens, HD_F), F32)])
    return pl.pallas_call(
        kern,
        grid_spec=grid_spec,
        out_shape=jax.ShapeDtypeStruct((nseq, H_F * tokens, HD_F), F32),
        compiler_params=_params("parallel", "arbitrary"),
        name="dec_attn",
    )(page_table, q2, k_new, v_new, small, *([cache_k] * npp), *([cache_v] * npp), *([cache_f] * npp), *cs)


def _merge_kernel(h_ref, ya_ref, yb_ref, ga_ref, gb_ref, gt_ref, wa_ref, wb_ref, wo_ref, o_ref):
    pa = jnp.dot(ya_ref[...].astype(BF16), wa_ref[...], preferred_element_type=F32)
    pb = jnp.dot(yb_ref[...].astype(BF16), wb_ref[...], preferred_element_type=F32)
    merged = _sigmoid(ga_ref[...]) * pa + _sigmoid(gb_ref[...]) * pb
    out = jnp.dot(merged.astype(BF16), wo_ref[...], preferred_element_type=F32)
    o_ref[...] = h_ref[...] + gt_ref[...] * out


def _merge(h, ya, yb, ga, gb, mod, wa, wb, wo, tm, tiles_per_seq, mod_rows):
    m = h.shape[0]
    row = pl.BlockSpec((tm, D_MODEL), lambda i: (i, 0))
    wspec = _resident((D_MODEL, D_MODEL), lambda i: (0, 0))
    return pl.pallas_call(
        _merge_kernel,
        grid=(m // tm,),
        in_specs=[row, row, row, row, row, _mod_spec(mod_rows, tm, tiles_per_seq, 5), wspec, wspec, wspec],
        out_specs=row,
        out_shape=jax.ShapeDtypeStruct((m, D_MODEL), F32),
        compiler_params=_params("parallel"),
        name="merge",
    )(h, ya, yb, ga, gb, mod, wa, wb, wo)


def _constants():
    f = np.float32
    head = np.arange(D_MODEL) // HD_F
    g256 = (head[:256, None] == head[None, :256]).astype(f)
    tri128 = np.tril(np.ones((CHUNK, CHUNK), f))
    tri256 = np.tril(np.ones((256, 256), f))
    striu = np.triu(np.ones((PAGE, PAGE), f), 1)
    expand = np.zeros((LANE, D_MODEL), f)
    expand[head, np.arange(D_MODEL)] = 1.0
    place_q = np.zeros((3, LANE, LANE), f)
    place_k = np.zeros((3, LANE, LANE), f)
    ones_q = np.zeros((1, LANE), f)
    ones_k = np.zeros((1, LANE), f)
    for h in range(H_F):
        for i in range(3):
            place_q[i, H_A + h, F_LANES * h + i] = 1.0
            place_k[i, H_A + h, F_LANES * h + 3 + i] = 1.0
            ones_q[0, F_LANES * h + 3 + i] = 1.0
            ones_k[0, F_LANES * h + i] = 1.0
    out = {k: jnp.asarray(v, BF16) for k, v in
           dict(g256=g256, tri128=tri128, tri256=tri256, striu=striu, expand=expand,
                place_q=place_q, place_k=place_k).items()}
    out.update(ones_q=jnp.asarray(ones_q), ones_k=jnp.asarray(ones_k))
    return out


def _dec_constants(tokens, consts):
    f = np.float32
    e16 = np.zeros((H_F, LANE), f)
    for hq in range(H_F):
        e16[hq, hq * tokens:(hq + 1) * tokens] = 1.0
    e128 = np.zeros((LANE, LANE), f)
    e128[H_A:H_A + H_F] = e16
    r = np.arange(DEC_ROWS)
    rep = (r[:, None] // KV_F == np.arange(PAGE)[None, :]).astype(f)
    lane_kv = (np.arange(LANE) // tokens) // (H_F // KV_F)
    kvmask = np.where(r[:, None] % KV_F == lane_kv[None, :], 0.0, NEG).astype(f)
    return dict(e16=jnp.asarray(e16, BF16), e128=jnp.asarray(e128, BF16), tri=consts["tri128"],
                striu=consts["striu"], rep=jnp.asarray(np.concatenate([rep, rep], axis=1), BF16),
                kvmask=jnp.asarray(kvmask))


def _ffn_weights(w_in, w_out):
    depth = w_in.shape[0]
    nc = D_FF // FF_TILE
    wb = w_in.astype(BF16)
    wg = wb[:, :, :D_FF].reshape(depth, D_MODEL, nc, FF_TILE).transpose(0, 2, 1, 3)
    wu = wb[:, :, D_FF:].reshape(depth, D_MODEL, nc, FF_TILE).transpose(0, 2, 1, 3)
    wo = w_out.astype(BF16).reshape(depth, nc, FF_TILE, D_MODEL)
    return wg, wu, wo


def _inproj_weights(w_in):
    depth = w_in.shape[0]
    sizes = (D_MODEL, CONV_DIM, H_A, H_F * HD_F, KV_F * HD_F, KV_F * HD_F, H_F, D_MODEL, D_MODEL)
    pts = np.cumsum(sizes)[:-1].tolist()
    wz, wx, wdt, wq, wk, wv, wf, wga, wgb = jnp.split(w_in, pts, axis=-1)
    wq = wq.reshape(depth, D_MODEL, H_F, HD_F)[:, :, list(HEAD_PERM), :].reshape(depth, D_MODEL, H_F * HD_F)
    pad = jnp.zeros((depth, D_MODEL, LANE - H_A - H_F), w_in.dtype)
    return jnp.concatenate([wz, wx, wq, wk, wv, wga, wgb, wdt, wf, pad], axis=-1).astype(BF16)


def _run_group(x2d, mod, mod_rows, tiles, conv_prev, ssm_prev, lw, consts, fox, nseq, nchunk, rows, valid,
               q_dtype, q_scale, y_dtype):
    tm_ffn, tm_in, tm_mg, tps_ffn, tps_in, tps_mg = tiles
    h = x2d
    states = []
    depth = lw["wg1"].shape[0]
    for l in range(depth):
        ml = mod[l]
        h = _ffn(h, ml, 0, lw["nw"][l, 0:1], lw["wg1"][l], lw["wu1"][l], lw["wo1"][l], tm_ffn, tps_ffn, mod_rows)
        z, xbc, q, k, kb, v, vb, ga, gb, small = _inproj(
            h, ml, lw["nw"][l, 1:2], lw["w_in"][l], consts["g256"], lw["qnw"][l], lw["knw"][l], lw["sbias"][l],
            tm_in, tps_in, mod_rows, q_dtype, q_scale)
        cprev8 = jnp.pad(conv_prev[l], ((0, 0), (8 - (CONV_W - 1), 0), (0, 0)))
        ya, conv_new, ssm_new = _mamba(
            xbc, z, small, cprev8, ssm_prev[l], lw["cw8"][l], lw["cb"][l], lw["alog"][l], lw["dsk"][l], lw["gnw"][l],
            consts["tri128"], consts["expand"], nseq, nchunk, rows, valid, y_dtype)
        yb = fox(l, q, k, kb, v, vb, small)
        h = _merge(h, ya, yb, ga, gb, ml, lw["wa"][l], lw["wb"][l], lw["wout"][l], tm_mg, tps_mg, mod_rows)
        h = _ffn(h, ml, 6, lw["nw"][l, 2:3], lw["wg2"][l], lw["wu2"][l], lw["wo2"][l], tm_ffn, tps_ffn, mod_rows)
        states.append((k, v, small, conv_new, ssm_new))
    return h, states


def kernel(x_prompt, x_sample, cache_k, cache_v, cache_logf, state_conv, state_ssm, page_table,
           c_prompt, c_sample, w_ada, b_ada, norm_w, w_ffn1_in, w_ffn1_out, w_ffn2_in, w_ffn2_out,
           w_in, conv_w, conv_b, dt_bias, a_log, d_skip, gn_w, q_norm_w, k_norm_w, b_f,
           w_proj_a, w_proj_b, w_out):
    bp, s_len, d = x_prompt.shape
    db, t_len, _ = x_sample.shape
    depth = w_in.shape[0]
    n_pool = cache_k.shape[1]
    kvw = KV_F * HD_F
    assert d == D_MODEL and s_len % CHUNK == 0 and t_len <= 8 and cache_k.shape[2] == PAGE
    consts = _constants()

    wg1, wu1, wo1 = _ffn_weights(w_ffn1_in, w_ffn1_out)
    wg2, wu2, wo2 = _ffn_weights(w_ffn2_in, w_ffn2_out)
    pad_lanes = lambda a: jnp.pad(a, ((0, 0), (0, LANE - a.shape[1])))[:, None, :]
    lw = dict(
        nw=norm_w, wg1=wg1, wu1=wu1, wo1=wo1, wg2=wg2, wu2=wu2, wo2=wo2,
        w_in=_inproj_weights(w_in),
        qnw=jnp.tile(q_norm_w, (1, 256 // HD_F))[:, None, :],
        knw=jnp.tile(k_norm_w, (1, 256 // HD_F))[:, None, :],
        sbias=pad_lanes(jnp.concatenate([dt_bias, b_f], axis=1)),
        cw8=jnp.pad(conv_w, ((0, 0), (0, 8 - CONV_W), (0, 0))),
        cb=conv_b[:, None, :],
        alog=pad_lanes(a_log),
        dsk=jnp.repeat(d_skip, P_A, axis=1)[:, None, :],
        gnw=gn_w[:, None, :],
        wa=w_proj_a.astype(BF16),
        wb=w_proj_b.reshape(depth, H_F, HD_F, d)[:, list(HEAD_PERM)].reshape(depth, d, d).astype(BF16),
        wout=w_out.astype(BF16),
    )

    mod_all = _ada(jnp.concatenate([c_prompt, c_sample], axis=0), w_ada, b_ada)
    mod_all = mod_all.reshape(depth, bp + db, N_MOD, d)
    mod_p = mod_all[:, :bp, :, None, :]
    tok = 8
    mod_s = jnp.broadcast_to(mod_all[:, bp:, None], (depth, db, tok, N_MOD, d))
    mod_s = mod_s.reshape(depth, db * tok, N_MOD, d).transpose(0, 2, 1, 3)

    tq = min(512, s_len)

    def fox_p(l, q, k, kb, v, vb, small):
        fq, kcat = _fcum(small, kb, consts, bp, s_len)
        return _attn(q, fq, kcat, vb, bp, s_len, tq)

    tm = min(512, s_len)
    tm_in = min(256, s_len)
    tiles_p = (tm, tm_in, tm, s_len // tm, s_len // tm_in, s_len // tm)
    conv0 = jnp.zeros((depth, bp, CONV_W - 1, CONV_DIM), F32)
    ssm0 = jnp.zeros((depth, bp, D_MODEL, N_A), F32)
    y_p, st_p = _run_group(x_prompt.reshape(bp * s_len, d), mod_p, 1, tiles_p, conv0, ssm0, lw, consts, fox_p,
                           bp, s_len // CHUNK, CHUNK, CHUNK, BF16, ATTN_SCALE * LOG2E, BF16)

    dc = _dec_constants(tok, consts)
    npp = 8 if page_table.shape[1] % 8 == 0 else 1
    head_inv = list(np.argsort(HEAD_PERM))

    def fox_s(l, q, k, kb, v, vb, small):
        q2 = q.reshape(db, tok, H_F, HD_F)[:, :, head_inv].transpose(0, 2, 1, 3).reshape(db, H_F * tok, HD_F)
        rows_new = lambda a: jnp.pad(a.reshape(db, tok * KV_F, HD_F), ((0, 0), (0, DEC_ROWS - tok * KV_F), (0, 0)))
        out = _dec_attn(page_table, q2, rows_new(k), rows_new(v), small, cache_k, cache_v, cache_logf, l, dc,
                        tok, t_len, npp)
        out = out.reshape(db, H_F, tok, HD_F)[:, list(HEAD_PERM)].transpose(0, 2, 1, 3)
        return out.reshape(db * tok, d)

    ms = db * tok
    tiles_s = (ms, ms, ms, 1, 1, 1)
    x_s = jnp.pad(x_sample, ((0, 0), (0, tok - t_len), (0, 0))).reshape(ms, d)
    y_s, st_s = _run_group(x_s, mod_s, ms, tiles_s, state_conv, state_ssm.reshape(depth, db, D_MODEL, N_A), lw, consts,
                           fox_s, db, 1, tok, t_len, F32, ATTN_SCALE, F32)

    def gather(states, nseq, length, keep):
        k = jnp.stack([s[0] for s in states]).reshape(depth, nseq, length, KV_F, HD_F)[:, :, :keep]
        v = jnp.stack([s[1] for s in states]).reshape(depth, nseq, length, KV_F, HD_F)[:, :, :keep]
        lf = jnp.stack([s[2][:, H_A:H_A + H_F] for s in states]).reshape(depth, nseq, length, H_F)[:, :, :keep]
        conv = jnp.stack([s[3] for s in states])
        ssm = jnp.stack([s[4] for s in states]).reshape(depth, nseq, H_A, P_A, N_A)
        return k, v, lf, conv, ssm

    k_p, v_p, lf_p, conv_p, ssm_p = gather(st_p, bp, s_len, s_len)
    k_s, v_s, lf_s, conv_s, ssm_s = gather(st_s, db, tok, t_len)
    y_prompt = y_p.reshape(bp, s_len, d)
    y_sample = y_s.reshape(db, tok, d)[:, :t_len]
    return (y_prompt, y_sample, k_p, v_p, lf_p, conv_p, ssm_p, k_s, v_s, lf_s, conv_s, ssm_s)
```

```python
import functools

import numpy as np
import jax
import jax.numpy as jnp
from jax import lax
from jax.experimental import pallas as pl
from jax.experimental.pallas import tpu as pltpu

F32 = jnp.float32
BF16 = jnp.bfloat16

D_MODEL = 1024
D_FF = 2816
P_A = 64
H_A = 16
G_A = 2
N_A = 128
CONV_W = 4
CONV_DIM = D_MODEL + 2 * G_A * N_A
HD_F = 64
H_F = 16
KV_F = 8
N_MOD = 9
RESID_HALF = 0.5
EPS = 1e-6
ATTN_SCALE = HD_F ** -0.5
PAGE = 128
CHUNK = 128
NEG = -1e30
LOG2E = 1.4426950408889634
F_LANES = 6
DEC_ROWS = PAGE * KV_F

LANE = 128
FF_TILE = 256
OFF_Z = 0
OFF_X = OFF_Z + D_MODEL
OFF_Q = OFF_X + CONV_DIM
OFF_K = OFF_Q + H_F * HD_F
OFF_V = OFF_K + KV_F * HD_F
OFF_GA = OFF_V + KV_F * HD_F
OFF_GB = OFF_GA + D_MODEL
OFF_S = OFF_GB + D_MODEL
IN_W = OFF_S + LANE
HEAD_PERM = tuple(4 * j + o for j in range(4) for o in (0, 2, 1, 3))
VMEM_LIMIT = 56 * 1024 * 1024


def _params(*sem):
    return pltpu.CompilerParams(dimension_semantics=sem, vmem_limit_bytes=VMEM_LIMIT)


def _resident(shape, index_map):
    return pl.BlockSpec(shape, index_map, pipeline_mode=pl.Buffered(1))


def _sigmoid(x):
    return 1.0 / (1.0 + jnp.exp(-x))


def _norm_mod(x, nw, shift, scale):
    y = x * lax.rsqrt(jnp.mean(x * x, axis=-1, keepdims=True) + EPS) * nw
    return y * (1.0 + scale) + shift


def _split3(x):
    hi = x.astype(BF16)
    r = x - hi.astype(F32)
    mid = r.astype(BF16)
    lo = (r - mid.astype(F32)).astype(BF16)
    return hi, mid, lo


def _dot01_l(m01, x):
    hi, mid, lo = _split3(x)
    d = lambda p: jnp.dot(m01, p, preferred_element_type=F32)
    return d(hi) + d(mid) + d(lo)


def _dot01_r(x, m01):
    hi, mid, lo = _split3(x)
    d = lambda p: jnp.dot(p, m01, preferred_element_type=F32)
    return d(hi) + d(mid) + d(lo)


def _dot_nt(a, b):
    return lax.dot_general(a, b, (((1,), (1,)), ((), ())), preferred_element_type=F32)


def _ada_kernel(c_ref, w_ref, b_ref, o_ref):
    c = c_ref[...]
    a = (c * _sigmoid(c)).astype(BF16)
    o_ref[...] = jnp.dot(a, w_ref[...].astype(BF16), preferred_element_type=F32) + b_ref[...]


def _ada(c_all, w_ada, b_ada):
    depth, d, n = w_ada.shape
    rows = c_all.shape[0]
    tn = 1024
    return pl.pallas_call(
        _ada_kernel,
        grid=(depth, n // tn),
        in_specs=[pl.BlockSpec((rows, d), lambda l, j: (0, 0)),
                  pl.BlockSpec((None, d, tn), lambda l, j: (l, 0, j)),
                  pl.BlockSpec((None, 1, tn), lambda l, j: (l, 0, j))],
        out_specs=pl.BlockSpec((None, rows, tn), lambda l, j: (l, 0, j)),
        out_shape=jax.ShapeDtypeStruct((depth, rows, n), F32),
        compiler_params=_params("parallel", "parallel"),
        name="ada",
    )(c_all, w_ada, b_ada.reshape(depth, 1, n))


def _ffn_kernel(x_ref, nw_ref, sh_ref, sc_ref, gt_ref, wg_ref, wu_ref, wo_ref, o_ref, u_ref, acc_ref):
    x = x_ref[...]
    u_ref[...] = _norm_mod(x, nw_ref[...], sh_ref[...], sc_ref[...]).astype(BF16)
    for c in range(wg_ref.shape[0]):
        u = u_ref[...]
        g = jnp.dot(u, wg_ref[c], preferred_element_type=F32)
        up = jnp.dot(u, wu_ref[c], preferred_element_type=F32)
        act = ((g * _sigmoid(g)) * up).astype(BF16)
        part = jnp.dot(act, wo_ref[c], preferred_element_type=F32)
        if c == 0:
            acc_ref[...] = part
        else:
            acc_ref[...] += part
    o_ref[...] = x + (RESID_HALF * gt_ref[...]) * acc_ref[...]


def _mod_spec(mod_rows, tm, tiles_per_seq, k):
    if mod_rows == 1:
        return pl.BlockSpec((None, None, 1, D_MODEL), lambda i: (i // tiles_per_seq, k, 0, 0))
    return pl.BlockSpec((None, tm, D_MODEL), lambda i: (k, i, 0))


def _ffn(h, mod, kmod, nw, wg, wu, wo, tm, tiles_per_seq, mod_rows):
    m = h.shape[0]
    nc = wg.shape[0]
    row = pl.BlockSpec((tm, D_MODEL), lambda i: (i, 0))
    return pl.pallas_call(
        _ffn_kernel,
        grid=(m // tm,),
        in_specs=[row,
                  pl.BlockSpec((1, D_MODEL), lambda i: (0, 0)),
                  _mod_spec(mod_rows, tm, tiles_per_seq, kmod),
                  _mod_spec(mod_rows, tm, tiles_per_seq, kmod + 1),
                  _mod_spec(mod_rows, tm, tiles_per_seq, kmod + 2),
                  _resident((nc, D_MODEL, FF_TILE), lambda i: (0, 0, 0)),
                  _resident((nc, D_MODEL, FF_TILE), lambda i: (0, 0, 0)),
                  _resident((nc, FF_TILE, D_MODEL), lambda i: (0, 0, 0))],
        out_specs=row,
        out_shape=jax.ShapeDtypeStruct((m, D_MODEL), F32),
        scratch_shapes=[pltpu.VMEM((tm, D_MODEL), BF16), pltpu.VMEM((tm, D_MODEL), F32)],
        compiler_params=_params("parallel"),
        name="ffn",
    )(h, nw, mod, mod, mod, wg, wu, wo)


def _head_norm(x, g256, w):
    sq = x * x
    hi = sq.astype(BF16)
    lo = (sq - hi.astype(F32)).astype(BF16)
    ss = jnp.dot(hi, g256, preferred_element_type=F32) + jnp.dot(lo, g256, preferred_element_type=F32)
    return x * lax.rsqrt(ss * (1.0 / HD_F) + EPS) * w


def _inproj_kernel(x_ref, nw_ref, sh_ref, sc_ref, w_ref, g_ref, qnw_ref, knw_ref, sb_ref,
                   z_ref, xbc_ref, q_ref, k_ref, kb_ref, v_ref, vb_ref, ga_ref, gb_ref, sm_ref, u_ref, *, q_scale):
    u_ref[...] = _norm_mod(x_ref[...], nw_ref[...], sh_ref[...], sc_ref[...]).astype(BF16)

    def mm(lo, n):
        return jnp.dot(u_ref[...], w_ref[:, lo:lo + n], preferred_element_type=F32)

    t = 256
    for c in range(D_MODEL // t):
        z_ref[:, c * t:(c + 1) * t] = mm(OFF_Z + c * t, t)
        ga_ref[:, c * t:(c + 1) * t] = mm(OFF_GA + c * t, t)
        gb_ref[:, c * t:(c + 1) * t] = mm(OFF_GB + c * t, t)
        qn = _head_norm(mm(OFF_Q + c * t, t), g_ref[...], qnw_ref[...])
        q_ref[:, c * t:(c + 1) * t] = (qn * q_scale).astype(q_ref.dtype)
    for c in range(CONV_DIM // t):
        xbc_ref[:, c * t:(c + 1) * t] = mm(OFF_X + c * t, t)
    for c in range(KV_F * HD_F // t):
        kn = _head_norm(mm(OFF_K + c * t, t), g_ref[...], knw_ref[...])
        k_ref[:, c * t:(c + 1) * t] = kn
        kb_ref[:, c * t:(c + 1) * t] = kn.astype(BF16)
        vv = mm(OFF_V + c * t, t)
        v_ref[:, c * t:(c + 1) * t] = vv
        vb_ref[:, c * t:(c + 1) * t] = vv.astype(BF16)
    raw = mm(OFF_S, LANE) + sb_ref[...]
    tail = jnp.log1p(jnp.exp(-jnp.abs(raw)))
    lane = lax.broadcasted_iota(jnp.int32, raw.shape, 1)
    softplus = jnp.maximum(raw, 0.0) + tail
    log_sig = jnp.minimum(raw, 0.0) - tail
    sm_ref[...] = jnp.where(lane < H_A, softplus, jnp.where(lane < H_A + H_F, log_sig, 0.0))


def _inproj(h, mod, nw, w, g256, qnw, knw, sbias, tm, tiles_per_seq, mod_rows, q_dtype, q_scale):
    m = h.shape[0]
    row = lambda n: pl.BlockSpec((tm, n), lambda i: (i, 0))
    const = lambda a: _resident(a.shape, lambda i: (0,) * a.ndim)
    kv = KV_F * HD_F
    outs = [(D_MODEL, F32), (CONV_DIM, F32), (H_F * HD_F, q_dtype), (kv, F32), (kv, BF16),
            (kv, F32), (kv, BF16), (D_MODEL, F32), (D_MODEL, F32), (LANE, F32)]
    return pl.pallas_call(
        functools.partial(_inproj_kernel, q_scale=q_scale),
        grid=(m // tm,),
        in_specs=[row(D_MODEL),
                  pl.BlockSpec((1, D_MODEL), lambda i: (0, 0)),
                  _mod_spec(mod_rows, tm, tiles_per_seq, 3),
                  _mod_spec(mod_rows, tm, tiles_per_seq, 4),
                  const(w), const(g256), const(qnw), const(knw), const(sbias)],
        out_specs=[row(n) for n, _ in outs],
        out_shape=[jax.ShapeDtypeStruct((m, n), dt) for n, dt in outs],
        scratch_shapes=[pltpu.VMEM((tm, D_MODEL), BF16)],
        compiler_params=_params("parallel"),
        name="inproj",
    )(h, nw, mod, mod, w, g256, qnw, knw, sbias)


def _fcum_kernel(sm_ref, kb_ref, tri_ref, pq_ref, pk_ref, oq_ref, ok_ref, fq_ref, kcat_ref, carry_ref):
    @pl.when(pl.program_id(1) == 0)
    def _():
        carry_ref[...] = jnp.zeros_like(carry_ref)

    cum = _dot01_l(tri_ref[...], sm_ref[...]) + carry_ref[0:1, :]
    carry_ref[0:1, :] = cum[cum.shape[0] - 1:, :]
    pieces = _split3(cum * LOG2E)

    def place(p_ref):
        return sum(jnp.dot(pieces[i], p_ref[i], preferred_element_type=F32) for i in range(3))

    fq_ref[...] = (place(pq_ref) + oq_ref[...]).astype(BF16)
    fk = (ok_ref[...] - place(pk_ref)).astype(BF16)
    for pair in range(KV_F // 2):
        kcat_ref[:, 2 * pair * LANE:(2 * pair + 1) * LANE] = kb_ref[:, pair * LANE:(pair + 1) * LANE]
        kcat_ref[:, (2 * pair + 1) * LANE:(2 * pair + 2) * LANE] = fk


def _fcum(small, kb, consts, bsz, s_len):
    tri = consts["tri256"]
    t = tri.shape[0]
    nt = s_len // t
    kvw = KV_F * HD_F
    row = lambda n: pl.BlockSpec((t, n), lambda b, i: (b * nt + i, 0))
    const = lambda a: _resident(a.shape, lambda b, i: (0,) * a.ndim)
    cs = [tri, consts["place_q"], consts["place_k"], consts["ones_q"], consts["ones_k"]]
    return pl.pallas_call(
        _fcum_kernel,
        grid=(bsz, nt),
        in_specs=[row(LANE), row(kvw)] + [const(a) for a in cs],
        out_specs=[row(LANE), row(2 * kvw)],
        out_shape=[jax.ShapeDtypeStruct((bsz * s_len, LANE), BF16),
                   jax.ShapeDtypeStruct((bsz * s_len, 2 * kvw), BF16)],
        scratch_shapes=[pltpu.VMEM((8, LANE), F32)],
        compiler_params=_params("parallel", "arbitrary"),
        name="fcum",
    )(small, kb, *cs)


def _mamba_kernel(xbc_ref, z_ref, sm_ref, cprev_ref, h0_ref, cw_ref, cb_ref, alog_ref, dsk_ref, gnw_ref,
                  tri_ref, e_ref, ya_ref, cnew_ref, ssm_ref, hist_ref, dt_ref, *, rows, valid):
    q = CHUNK
    c = pl.program_id(1)
    di = D_MODEL

    @pl.when(c == 0)
    def _():
        hist_ref[0:8, :] = cprev_ref[...]
        ssm_ref[...] = h0_ref[...]
        if rows < q:
            hist_ref[8 + rows:, :] = jnp.zeros((q - rows, CONV_DIM), F32)
            dt_ref[...] = jnp.zeros_like(dt_ref)

    hist_ref[8:8 + rows, :] = xbc_ref[...]
    conv = cb_ref[...] + hist_ref[5:5 + q, :] * cw_ref[0:1, :]
    for j in range(1, CONV_W):
        conv = conv + hist_ref[5 + j:5 + j + q, :] * cw_ref[j:j + 1, :]
    cnew_ref[...] = hist_ref[5 + valid:8 + valid, :]
    hist_ref[5:8, :] = hist_ref[5 + q:8 + q, :]
    act = conv * _sigmoid(conv)
    xs = act[:, :di]
    bmat = [act[:, di + g * N_A:di + (g + 1) * N_A].astype(BF16) for g in range(G_A)]
    cmat = [act[:, di + (G_A + g) * N_A:di + (G_A + g + 1) * N_A].astype(BF16) for g in range(G_A)]

    lane = lax.broadcasted_iota(jnp.int32, (1, LANE), 1)
    a_row = jnp.where(lane < H_A, -jnp.exp(alog_ref[...]), 0.0)
    if rows < q:
        rid = lax.broadcasted_iota(jnp.int32, (rows, LANE), 0)
        dt_ref[0:rows, :] = jnp.where(rid < valid, sm_ref[...], 0.0)
        dt = dt_ref[...]
    else:
        dt = sm_ref[...]
    tri = tri_ref[...]
    acum = _dot01_l(tri, dt * a_row)
    acum_t = acum.T
    expand = e_ref[...]
    dt_e = _dot01_r(dt, expand)
    acum_e = _dot01_r(acum, expand)
    xdt = xs * dt_e

    rq = lax.broadcasted_iota(jnp.int32, (q, q), 0)
    rs = lax.broadcasted_iota(jnp.int32, (q, q), 1)
    causal = rs <= rq
    half = lax.broadcasted_iota(jnp.int32, (q, LANE), 1) < P_A
    cb = [_dot_nt(cmat[g], bmat[g]) for g in range(G_A)]
    xdt_b = xdt.astype(BF16)
    y_parts = []
    for pair in range(H_A // 2):
        res = []
        for o in range(2):
            h = 2 * pair + o
            seg = acum[:, h:h + 1] - acum_t[h:h + 1, :]
            mix = (cb[h // (H_A // G_A)] * jnp.where(causal, jnp.exp(seg), 0.0)).astype(BF16)
            res.append(jnp.dot(mix, xdt_b[:, pair * LANE:(pair + 1) * LANE], preferred_element_type=F32))
        y_parts.append(jnp.where(half, res[0], res[1]))
    y = jnp.concatenate(y_parts, axis=1)

    hg = di // G_A
    state = ssm_ref[...]
    state_b = state.astype(BF16)
    y_off = jnp.concatenate([_dot_nt(cmat[g], state_b[g * hg:(g + 1) * hg, :]) for g in range(G_A)], axis=1)
    y = y + y_off * jnp.exp(acum_e)
    xdt_t = xdt.T
    acum_et = acum_e.T
    a_end = acum_et[:, q - 1:q]
    xdtd_t = (xdt_t * jnp.exp(a_end - acum_et)).astype(BF16)
    upd = jnp.concatenate(
        [jnp.dot(xdtd_t[g * hg:(g + 1) * hg, :], bmat[g], preferred_element_type=F32) for g in range(G_A)], axis=0)
    ssm_ref[...] = jnp.exp(a_end) * state + upd

    y = (y + dsk_ref[...] * xs)[0:rows, :]
    zz = z_ref[...]
    gated = y * (zz * _sigmoid(zz))
    out = gated * lax.rsqrt(jnp.mean(gated * gated, axis=-1, keepdims=True) + EPS) * gnw_ref[...]
    ya_ref[...] = out.astype(ya_ref.dtype)


def _mamba(xbc, z, small, cprev8, h0, cw8, cb, alog, dsk, gnw, tri, expand, nseq, nchunk, rows, valid, ya_dtype):
    m = xbc.shape[0]
    row = lambda n: pl.BlockSpec((rows, n), lambda b, c: (b * nchunk + c, 0))
    const = lambda a: _resident(a.shape, lambda b, c: (0,) * a.ndim)
    di = D_MODEL
    kern = functools.partial(_mamba_kernel, rows=rows, valid=valid)
    return pl.pallas_call(
        kern,
        grid=(nseq, nchunk),
        in_specs=[row(CONV_DIM), row(di), row(LANE),
                  pl.BlockSpec((None, 8, CONV_DIM), lambda b, c: (b, 0, 0)),
                  pl.BlockSpec((None, di, N_A), lambda b, c: (b, 0, 0)),
                  const(cw8), const(cb), const(alog), const(dsk), const(gnw), const(tri), const(expand)],
        out_specs=[row(di),
                   pl.BlockSpec((None, CONV_W - 1, CONV_DIM), lambda b, c: (b, 0, 0)),
                   pl.BlockSpec((None, di, N_A), lambda b, c: (b, 0, 0))],
        out_shape=[jax.ShapeDtypeStruct((m, di), ya_dtype),
                   jax.ShapeDtypeStruct((nseq, CONV_W - 1, CONV_DIM), F32),
                   jax.ShapeDtypeStruct((nseq, di, N_A), F32)],
        scratch_shapes=[pltpu.VMEM((8 + CHUNK, CONV_DIM), F32), pltpu.VMEM((CHUNK, LANE), F32)],
        compiler_params=_params("parallel", "arbitrary"),
        name="mamba",
    )(xbc, z, small, cprev8, h0, cw8, cb, alog, dsk, gnw, tri, expand)


def _attn_kernel(q_ref, fq_ref, k_ref, v_ref, o_ref, qcat_ref, m_ref, l_ref, acc_ref, *, tq, tk):
    qi = pl.program_id(1)
    ki = pl.program_id(2)
    nk = pl.num_programs(2)

    @pl.when(ki == 0)
    def _():
        m_ref[...] = jnp.full_like(m_ref, NEG)
        l_ref[...] = jnp.zeros_like(l_ref)
        acc_ref[...] = jnp.zeros_like(acc_ref)
        lane = lax.broadcasted_iota(jnp.int32, (tq, LANE), 1)
        lo = lane < HD_F
        fq = fq_ref[...]
        for blk in range(H_F // 2):
            qb = q_ref[:, blk * LANE:(blk + 1) * LANE]
            for o in range(2):
                slot = 2 * blk + o
                hq = HEAD_PERM[slot]
                own = jnp.logical_and(lane >= F_LANES * hq, lane < F_LANES * (hq + 1))
                qcat_ref[slot, :, 0:LANE] = jnp.where(lo if o == 0 else jnp.logical_not(lo), qb, jnp.zeros_like(qb))
                qcat_ref[slot, :, LANE:2 * LANE] = jnp.where(own, fq, jnp.zeros_like(fq))

    def step(diagonal):
        if diagonal:
            causal = (lax.broadcasted_iota(jnp.int32, (tk, tq), 0) <= lax.broadcasted_iota(jnp.int32, (tk, tq), 1))
        for slot in range(H_F):
            pair = slot // 4
            s = _dot_nt(k_ref[:, 2 * pair * LANE:(2 * pair + 2) * LANE], qcat_ref[slot])
            if diagonal:
                s = jnp.where(causal, s, NEG)
            m_old = m_ref[slot]
            m_new = jnp.maximum(m_old, jnp.max(s, axis=0, keepdims=True))
            alpha = jnp.exp2(m_old - m_new)
            p = jnp.exp2(s - m_new)
            l_ref[slot] = alpha * l_ref[slot] + jnp.sum(p, axis=0, keepdims=True)
            m_ref[slot] = m_new
            pv = lax.dot_general(v_ref[:, pair * LANE:(pair + 1) * LANE], p.astype(BF16),
                                 (((0,), (0,)), ((), ())), preferred_element_type=F32)
            acc_ref[slot] = alpha * acc_ref[slot] + pv

    pl.when(ki < qi)(functools.partial(step, False))
    pl.when(ki == qi)(functools.partial(step, True))

    @pl.when(ki == nk - 1)
    def _():
        lo = lax.broadcasted_iota(jnp.int32, (LANE, tq), 0) < HD_F
        for blk in range(H_F // 2):
            a0 = acc_ref[2 * blk] / l_ref[2 * blk]
            a1 = acc_ref[2 * blk + 1] / l_ref[2 * blk + 1]
            o_ref[:, blk * LANE:(blk + 1) * LANE] = jnp.where(lo, a0, a1).T.astype(o_ref.dtype)


def _attn(qb, fq, kcat, vb, bsz, s_len, tq):
    tk = tq
    nq = s_len // tq
    kern = functools.partial(_attn_kernel, tq=tq, tk=tk)
    kvw = KV_F * HD_F
    q_spec = lambda n: pl.BlockSpec((tq, n), lambda b, i, j: (b * nq + i, 0))
    k_spec = lambda n: pl.BlockSpec((tk, n), lambda b, i, j: (b * nq + jnp.minimum(i, j), 0))
    return pl.pallas_call(
        kern,
        grid=(bsz, nq, nq),
        in_specs=[q_spec(D_MODEL), q_spec(LANE), k_spec(2 * kvw), k_spec(kvw)],
        out_specs=q_spec(D_MODEL),
        out_shape=jax.ShapeDtypeStruct((bsz * s_len, D_MODEL), BF16),
        scratch_shapes=[pltpu.VMEM((H_F, tq, 2 * LANE), BF16),
                        pltpu.VMEM((H_F, 1, tq), F32), pltpu.VMEM((H_F, 1, tq), F32),
                        pltpu.VMEM((H_F, LANE, tq), F32)],
        compiler_params=_params("parallel", "parallel", "arbitrary"),
        name="attn",
    )(qb, fq, kcat, vb)


def _dec_kernel(pt_ref, q_ref, kn_ref, vn_ref, sm_ref, *rest, npp, tokens, valid):
    k_refs = rest[0:npp]
    v_refs = rest[npp:2 * npp]
    f_refs = rest[2 * npp:3 * npp]
    (et16_ref, et128_ref, tri_ref, slt_ref, o_ref, smp_ref, m_ref, l_ref, gq_ref, carry_ref, acc_ref) = rest[3 * npp:]
    g = pl.program_id(1)
    ng = pl.num_programs(1)
    kvw = KV_F * HD_F
    qbd = q_ref[...].astype(BF16)

    def update(scores, values_t):
        m_old = m_ref[...]
        m_new = m_old
        for s in scores:
            m_new = jnp.maximum(m_new, jnp.max(s, axis=1, keepdims=True))
        alpha = jnp.exp(m_old - m_new)
        l_add = jnp.zeros((LANE, 1), F32)
        pv = jnp.zeros((LANE, kvw), F32)
        for s, vt in zip(scores, values_t):
            p = jnp.exp(s - m_new)
            l_add = l_add + jnp.sum(p, axis=1, keepdims=True)
            pv = pv + _dot_nt(p.astype(BF16), vt.astype(BF16))
        l_ref[...] = alpha * l_ref[...] + l_add
        m_ref[...] = m_new
        acc_ref[...] = alpha * acc_ref[...] + pv

    @pl.when(g == 0)
    def _():
        m_ref[...] = jnp.full_like(m_ref, NEG)
        l_ref[...] = jnp.zeros_like(l_ref)
        carry_ref[...] = jnp.zeros_like(carry_ref)
        acc_ref[...] = jnp.zeros_like(acc_ref)
        smp_ref[...] = jnp.zeros_like(smp_ref)
        smp_ref[0:tokens, :] = sm_ref[...]
        cum_t = _dot01_l(tri_ref[...], smp_ref[...]).T
        cum_e = _dot01_l(et128_ref[...], cum_t)
        key = lax.broadcasted_iota(jnp.int32, (LANE, PAGE), 1)
        qtok = lax.broadcasted_iota(jnp.int32, (LANE, PAGE), 0) % tokens
        g_q = jnp.sum(jnp.where(key == qtok, cum_e, 0.0), axis=1, keepdims=True)
        gq_ref[...] = g_q
        ok = jnp.logical_and(key <= qtok, key < valid)
        s = jnp.dot(qbd, kn_ref[...].astype(BF16), preferred_element_type=F32) + (g_q - cum_e)
        update([jnp.where(ok, s, NEG)], [vn_ref[...]])

    base = gq_ref[...] + carry_ref[...]
    carry = jnp.zeros((LANE, 1), F32)
    scores, values_t = [], []
    for i in reversed(range(npp)):
        lf_e = _dot01_l(et16_ref[...], f_refs[i][...])
        later = _dot01_r(lf_e, slt_ref[...])
        kt = k_refs[i][...].reshape(kvw, PAGE).astype(BF16)
        scores.append(jnp.dot(qbd, kt, preferred_element_type=F32) + (later + (base + carry)))
        values_t.append(v_refs[i][...].reshape(kvw, PAGE))
        carry = carry + jnp.sum(lf_e, axis=1, keepdims=True)
    update(scores, values_t)
    carry_ref[...] = carry_ref[...] + carry

    @pl.when(g == ng - 1)
    def _():
        o_ref[...] = acc_ref[...] / l_ref[...]


def _dec_attn(page_table, qbd, kn_t, vn_t, small, cache_kt, cache_vt, cache_ft, layer, dc, tokens, valid, npp):
    nseq, n_pages = page_table.shape
    ng = n_pages // npp
    kvw = KV_F * HD_F
    rows = H_F * tokens
    kern = functools.partial(_dec_kernel, npp=npp, tokens=tokens, valid=valid)
    per_seq = lambda r, c: pl.BlockSpec((None, r, c), lambda b, g, pt: (b, 0, 0))
    page_idx = lambda b, g, pt, i: pt[b, (ng - 1 - g) * npp + i]
    kv_page = lambda i: pl.BlockSpec((None, None, KV_F, HD_F, PAGE),
                                     lambda b, g, pt: (layer, page_idx(b, g, pt, i), 0, 0, 0))
    f_page = lambda i: pl.BlockSpec((None, None, H_F, PAGE), lambda b, g, pt: (layer, page_idx(b, g, pt, i), 0, 0))
    const = lambda a: _resident(a.shape, lambda b, g, pt: (0,) * a.ndim)
    cs = [dc["et16"], dc["et128"], dc["tri"], dc["slt"]]
    in_specs = ([per_seq(rows, kvw), per_seq(kvw, PAGE), per_seq(kvw, PAGE),
                 pl.BlockSpec((tokens, LANE), lambda b, g, pt: (b, 0))]
                + [kv_page(i) for i in range(npp)] + [kv_page(i) for i in range(npp)]
                + [f_page(i) for i in range(npp)] + [const(a) for a in cs])
    col = pltpu.VMEM((rows, 1), F32)
    grid_spec = pltpu.PrefetchScalarGridSpec(
        num_scalar_prefetch=1,
        grid=(nseq, ng),
        in_specs=in_specs,
        out_specs=per_seq(rows, kvw),
        scratch_shapes=[pltpu.VMEM((PAGE, LANE), F32), col, col, col, col, pltpu.VMEM((rows, kvw), F32)])
    return pl.pallas_call(
        kern,
        grid_spec=grid_spec,
        out_shape=jax.ShapeDtypeStruct((nseq, rows, kvw), F32),
        compiler_params=_params("parallel", "arbitrary"),
        name="dec_attn",
    )(page_table, qbd, kn_t, vn_t, small, *([cache_kt] * npp), *([cache_vt] * npp), *([cache_ft] * npp), *cs)


def _merge_kernel(h_ref, ya_ref, yb_ref, ga_ref, gb_ref, gt_ref, wa_ref, wb_ref, wo_ref, o_ref):
    pa = jnp.dot(ya_ref[...].astype(BF16), wa_ref[...], preferred_element_type=F32)
    pb = jnp.dot(yb_ref[...].astype(BF16), wb_ref[...], preferred_element_type=F32)
    merged = _sigmoid(ga_ref[...]) * pa + _sigmoid(gb_ref[...]) * pb
    out = jnp.dot(merged.astype(BF16), wo_ref[...], preferred_element_type=F32)
    o_ref[...] = h_ref[...] + gt_ref[...] * out


def _merge(h, ya, yb, ga, gb, mod, wa, wb, wo, tm, tiles_per_seq, mod_rows):
    m = h.shape[0]
    row = pl.BlockSpec((tm, D_MODEL), lambda i: (i, 0))
    wspec = _resident((D_MODEL, D_MODEL), lambda i: (0, 0))
    return pl.pallas_call(
        _merge_kernel,
        grid=(m // tm,),
        in_specs=[row, row, row, row, row, _mod_spec(mod_rows, tm, tiles_per_seq, 5), wspec, wspec, wspec],
        out_specs=row,
        out_shape=jax.ShapeDtypeStruct((m, D_MODEL), F32),
        compiler_params=_params("parallel"),
        name="merge",
    )(h, ya, yb, ga, gb, mod, wa, wb, wo)


def _constants():
    f = np.float32
    head = np.arange(D_MODEL) // HD_F
    g256 = (head[:256, None] == head[None, :256]).astype(f)
    tri128 = np.tril(np.ones((CHUNK, CHUNK), f))
    tri256 = np.tril(np.ones((256, 256), f))
    striu = np.triu(np.ones((PAGE, PAGE), f), 1)
    expand = np.zeros((LANE, D_MODEL), f)
    expand[head, np.arange(D_MODEL)] = 1.0
    place_q = np.zeros((3, LANE, LANE), f)
    place_k = np.zeros((3, LANE, LANE), f)
    ones_q = np.zeros((1, LANE), f)
    ones_k = np.zeros((1, LANE), f)
    for h in range(H_F):
        for i in range(3):
            place_q[i, H_A + h, F_LANES * h + i] = 1.0
            place_k[i, H_A + h, F_LANES * h + 3 + i] = 1.0
            ones_q[0, F_LANES * h + 3 + i] = 1.0
            ones_k[0, F_LANES * h + i] = 1.0
    out = {k: jnp.asarray(v, BF16) for k, v in
           dict(g256=g256, tri128=tri128, tri256=tri256, striu=striu, expand=expand,
                place_q=place_q, place_k=place_k).items()}
    out.update(ones_q=jnp.asarray(ones_q), ones_k=jnp.asarray(ones_k))
    return out


def _dec_constants(tokens, consts):
    f = np.float32
    et16 = np.zeros((H_F * tokens, H_F), f)
    for hq in range(H_F):
        et16[hq * tokens:(hq + 1) * tokens, hq] = 1.0
    et128 = np.zeros((H_F * tokens, LANE), f)
    et128[:, H_A:H_A + H_F] = et16
    slt = np.tril(np.ones((PAGE, PAGE), f), -1)
    return dict(et16=jnp.asarray(et16, BF16), et128=jnp.asarray(et128, BF16), tri=consts["tri128"],
                slt=jnp.asarray(slt, BF16))


def _ffn_weights(w_in, w_out):
    depth = w_in.shape[0]
    nc = D_FF // FF_TILE
    wb = w_in.astype(BF16)
    wg = wb[:, :, :D_FF].reshape(depth, D_MODEL, nc, FF_TILE).transpose(0, 2, 1, 3)
    wu = wb[:, :, D_FF:].reshape(depth, D_MODEL, nc, FF_TILE).transpose(0, 2, 1, 3)
    wo = w_out.astype(BF16).reshape(depth, nc, FF_TILE, D_MODEL)
    return wg, wu, wo


def _inproj_weights(w_in):
    depth = w_in.shape[0]
    sizes = (D_MODEL, CONV_DIM, H_A, H_F * HD_F, KV_F * HD_F, KV_F * HD_F, H_F, D_MODEL, D_MODEL)
    pts = np.cumsum(sizes)[:-1].tolist()
    wz, wx, wdt, wq, wk, wv, wf, wga, wgb = jnp.split(w_in, pts, axis=-1)
    wq = wq.reshape(depth, D_MODEL, H_F, HD_F)[:, :, list(HEAD_PERM), :].reshape(depth, D_MODEL, H_F * HD_F)
    pad = jnp.zeros((depth, D_MODEL, LANE - H_A - H_F), w_in.dtype)
    return jnp.concatenate([wz, wx, wq, wk, wv, wga, wgb, wdt, wf, pad], axis=-1).astype(BF16)


def _run_group(x2d, mod, mod_rows, tiles, conv_prev, ssm_prev, lw, consts, fox, nseq, nchunk, rows, valid,
               q_dtype, q_scale, y_dtype):
    tm_ffn, tm_in, tm_mg, tps_ffn, tps_in, tps_mg = tiles
    h = x2d
    states = []
    depth = lw["wg1"].shape[0]
    for l in range(depth):
        ml = mod[l]
        h = _ffn(h, ml, 0, lw["nw"][l, 0:1], lw["wg1"][l], lw["wu1"][l], lw["wo1"][l], tm_ffn, tps_ffn, mod_rows)
        z, xbc, q, k, kb, v, vb, ga, gb, small = _inproj(
            h, ml, lw["nw"][l, 1:2], lw["w_in"][l], consts["g256"], lw["qnw"][l], lw["knw"][l], lw["sbias"][l],
            tm_in, tps_in, mod_rows, q_dtype, q_scale)
        cprev8 = jnp.pad(conv_prev[l], ((0, 0), (8 - (CONV_W - 1), 0), (0, 0)))
        ya, conv_new, ssm_new = _mamba(
            xbc, z, small, cprev8, ssm_prev[l], lw["cw8"][l], lw["cb"][l], lw["alog"][l], lw["dsk"][l], lw["gnw"][l],
            consts["tri128"], consts["expand"], nseq, nchunk, rows, valid, y_dtype)
        yb = fox(l, q, k, kb, v, vb, small)
        h = _merge(h, ya, yb, ga, gb, ml, lw["wa"][l], lw["wb"][l], lw["wout"][l], tm_mg, tps_mg, mod_rows)
        h = _ffn(h, ml, 6, lw["nw"][l, 2:3], lw["wg2"][l], lw["wu2"][l], lw["wo2"][l], tm_ffn, tps_ffn, mod_rows)
        states.append((k, v, small, conv_new, ssm_new))
    return h, states


def kernel(x_prompt, x_sample, cache_k, cache_v, cache_logf, state_conv, state_ssm, page_table,
           c_prompt, c_sample, w_ada, b_ada, norm_w, w_ffn1_in, w_ffn1_out, w_ffn2_in, w_ffn2_out,
           w_in, conv_w, conv_b, dt_bias, a_log, d_skip, gn_w, q_norm_w, k_norm_w, b_f,
           w_proj_a, w_proj_b, w_out):
    bp, s_len, d = x_prompt.shape
    db, t_len, _ = x_sample.shape
    depth = w_in.shape[0]
    n_pool = cache_k.shape[1]
    kvw = KV_F * HD_F
    assert d == D_MODEL and s_len % CHUNK == 0 and t_len <= 8 and cache_k.shape[2] == PAGE
    consts = _constants()

    wg1, wu1, wo1 = _ffn_weights(w_ffn1_in, w_ffn1_out)
    wg2, wu2, wo2 = _ffn_weights(w_ffn2_in, w_ffn2_out)
    pad_lanes = lambda a: jnp.pad(a, ((0, 0), (0, LANE - a.shape[1])))[:, None, :]
    lw = dict(
        nw=norm_w, wg1=wg1, wu1=wu1, wo1=wo1, wg2=wg2, wu2=wu2, wo2=wo2,
        w_in=_inproj_weights(w_in),
        qnw=jnp.tile(q_norm_w, (1, 256 // HD_F))[:, None, :],
        knw=jnp.tile(k_norm_w, (1, 256 // HD_F))[:, None, :],
        sbias=pad_lanes(jnp.concatenate([dt_bias, b_f], axis=1)),
        cw8=jnp.pad(conv_w, ((0, 0), (0, 8 - CONV_W), (0, 0))),
        cb=conv_b[:, None, :],
        alog=pad_lanes(a_log),
        dsk=jnp.repeat(d_skip, P_A, axis=1)[:, None, :],
        gnw=gn_w[:, None, :],
        wa=w_proj_a.astype(BF16),
        wb=w_proj_b.reshape(depth, H_F, HD_F, d)[:, list(HEAD_PERM)].reshape(depth, d, d).astype(BF16),
        wout=w_out.astype(BF16),
    )

    mod_all = _ada(jnp.concatenate([c_prompt, c_sample], axis=0), w_ada, b_ada)
    mod_all = mod_all.reshape(depth, bp + db, N_MOD, d)
    mod_p = mod_all[:, :bp, :, None, :]
    tok = 8
    mod_s = jnp.broadcast_to(mod_all[:, bp:, None], (depth, db, tok, N_MOD, d))
    mod_s = mod_s.reshape(depth, db * tok, N_MOD, d).transpose(0, 2, 1, 3)

    tq = min(512, s_len)

    def fox_p(l, q, k, kb, v, vb, small):
        fq, kcat = _fcum(small, kb, consts, bp, s_len)
        return _attn(q, fq, kcat, vb, bp, s_len, tq)

    tm = min(512, s_len)
    tm_in = min(256, s_len)
    tiles_p = (tm, tm_in, tm, s_len // tm, s_len // tm_in, s_len // tm)
    conv0 = jnp.zeros((depth, bp, CONV_W - 1, CONV_DIM), F32)
    ssm0 = jnp.zeros((depth, bp, D_MODEL, N_A), F32)
    y_p, st_p = _run_group(x_prompt.reshape(bp * s_len, d), mod_p, 1, tiles_p, conv0, ssm0, lw, consts, fox_p,
                           bp, s_len // CHUNK, CHUNK, CHUNK, BF16, ATTN_SCALE * LOG2E, BF16)

    dc = _dec_constants(tok, consts)
    npp = 8 if page_table.shape[1] % 8 == 0 else 1
    head_inv = list(np.argsort(HEAD_PERM))

    cache_kt = jnp.transpose(cache_k, (0, 1, 3, 4, 2))
    cache_vt = jnp.transpose(cache_v, (0, 1, 3, 4, 2))
    cache_ft = jnp.transpose(cache_logf, (0, 1, 3, 2))
    kv_of_head = np.arange(H_F) // (H_F // KV_F)
    own_kv = jnp.asarray(kv_of_head[:, None] == np.arange(KV_F)[None, :], F32)
    assert H_F * tok == LANE

    def fox_s(l, q, k, kb, v, vb, small):
        q2 = q.reshape(db, tok, H_F, HD_F)[:, :, head_inv].transpose(0, 2, 1, 3)
        qbd = (q2[:, :, :, None, :] * own_kv[None, :, None, :, None]).reshape(db, H_F * tok, kvw)
        new_t = lambda a: jnp.pad(a.reshape(db, tok, kvw).transpose(0, 2, 1), ((0, 0), (0, 0), (0, PAGE - tok)))
        out = _dec_attn(page_table, qbd, new_t(k), new_t(v), small, cache_kt, cache_vt, cache_ft, l, dc,
                        tok, t_len, npp)
        out = out.reshape(db, H_F, tok, KV_F, HD_F)
        pick = jnp.broadcast_to(jnp.asarray(kv_of_head)[None, :, None, None, None], (db, H_F, tok, 1, HD_F))
        out = jnp.take_along_axis(out, pick, axis=3)[:, :, :, 0]
        return out[:, list(HEAD_PERM)].transpose(0, 2, 1, 3).reshape(db * tok, d)

    ms = db * tok
    tiles_s = (ms, ms, ms, 1, 1, 1)
    x_s = jnp.pad(x_sample, ((0, 0), (0, tok - t_len), (0, 0))).reshape(ms, d)
    y_s, st_s = _run_group(x_s, mod_s, ms, tiles_s, state_conv, state_ssm.reshape(depth, db, D_MODEL, N_A), lw, consts,
                           fox_s, db, 1, tok, t_len, F32, ATTN_SCALE, F32)

    def gather(states, nseq, length, keep):
        k = jnp.stack([s[0] for s in states]).reshape(depth, nseq, length, KV_F, HD_F)[:, :, :keep]
        v = jnp.stack([s[1] for s in states]).reshape(depth, nseq, length, KV_F, HD_F)[:, :, :keep]
        lf = jnp.stack([s[2][:, H_A:H_A + H_F] for s in states]).reshape(depth, nseq, length, H_F)[:, :, :keep]
        conv = jnp.stack([s[3] for s in states])
        ssm = jnp.stack([s[4] for s in states]).reshape(depth, nseq, H_A, P_A, N_A)
        return k, v, lf, conv, ssm

    k_p, v_p, lf_p, conv_p, ssm_p = gather(st_p, bp, s_len, s_len)
    k_s, v_s, lf_s, conv_s, ssm_s = gather(st_s, db, tok, t_len)
    y_prompt = y_p.reshape(bp, s_len, d)
    y_sample = y_s.reshape(db, tok, d)[:, :t_len]
    return (y_prompt, y_sample, k_p, v_p, lf_p, conv_p, ssm_p, k_s, v_s, lf_s, conv_s, ssm_s)
```

```python
import functools

import numpy as np
import jax
import jax.numpy as jnp
from jax import lax
from jax.experimental import pallas as pl
from jax.experimental.pallas import tpu as pltpu

F32 = jnp.float32
BF16 = jnp.bfloat16

D_MODEL = 1024
D_FF = 2816
P_A = 64
H_A = 16
G_A = 2
N_A = 128
CONV_W = 4
CONV_DIM = D_MODEL + 2 * G_A * N_A
HD_F = 64
H_F = 16
KV_F = 8
N_MOD = 9
RESID_HALF = 0.5
EPS = 1e-6
ATTN_SCALE = HD_F ** -0.5
PAGE = 128
CHUNK = 128
NEG = -1e30
LOG2E = 1.4426950408889634
F_LANES = 6
DEC_ROWS = PAGE * KV_F

LANE = 128
FF_TILE = 256
OFF_Z = 0
OFF_X = OFF_Z + D_MODEL
OFF_Q = OFF_X + CONV_DIM
OFF_K = OFF_Q + H_F * HD_F
OFF_V = OFF_K + KV_F * HD_F
OFF_GA = OFF_V + KV_F * HD_F
OFF_GB = OFF_GA + D_MODEL
OFF_S = OFF_GB + D_MODEL
IN_W = OFF_S + LANE
HEAD_PERM = tuple(4 * j + o for j in range(4) for o in (0, 2, 1, 3))
VMEM_LIMIT = 56 * 1024 * 1024


def _params(*sem):
    return pltpu.CompilerParams(dimension_semantics=sem, vmem_limit_bytes=VMEM_LIMIT)


def _resident(shape, index_map):
    return pl.BlockSpec(shape, index_map, pipeline_mode=pl.Buffered(1))


def _sigmoid(x):
    return 1.0 / (1.0 + jnp.exp(-x))


def _norm_mod(x, nw, shift, scale):
    y = x * lax.rsqrt(jnp.mean(x * x, axis=-1, keepdims=True) + EPS) * nw
    return y * (1.0 + scale) + shift


def _split3(x):
    hi = x.astype(BF16)
    r = x - hi.astype(F32)
    mid = r.astype(BF16)
    lo = (r - mid.astype(F32)).astype(BF16)
    return hi, mid, lo


def _dot01_l(m01, x):
    hi, mid, lo = _split3(x)
    d = lambda p: jnp.dot(m01, p, preferred_element_type=F32)
    return d(hi) + d(mid) + d(lo)


def _dot01_r(x, m01):
    hi, mid, lo = _split3(x)
    d = lambda p: jnp.dot(p, m01, preferred_element_type=F32)
    return d(hi) + d(mid) + d(lo)


def _dot_nt(a, b):
    return lax.dot_general(a, b, (((1,), (1,)), ((), ())), preferred_element_type=F32)


def _ada_kernel(c_ref, w_ref, b_ref, o_ref):
    c = c_ref[...]
    a = (c * _sigmoid(c)).astype(BF16)
    o_ref[...] = jnp.dot(a, w_ref[...].astype(BF16), preferred_element_type=F32) + b_ref[...]


def _ada(c_all, w_ada, b_ada):
    depth, d, n = w_ada.shape
    rows = c_all.shape[0]
    tn = 1024
    return pl.pallas_call(
        _ada_kernel,
        grid=(depth, n // tn),
        in_specs=[pl.BlockSpec((rows, d), lambda l, j: (0, 0)),
                  pl.BlockSpec((None, d, tn), lambda l, j: (l, 0, j)),
                  pl.BlockSpec((None, 1, tn), lambda l, j: (l, 0, j))],
        out_specs=pl.BlockSpec((None, rows, tn), lambda l, j: (l, 0, j)),
        out_shape=jax.ShapeDtypeStruct((depth, rows, n), F32),
        compiler_params=_params("parallel", "parallel"),
        name="ada",
    )(c_all, w_ada, b_ada.reshape(depth, 1, n))


def _ffn_kernel(x_ref, nw_ref, sh_ref, sc_ref, gt_ref, wg_ref, wu_ref, wo_ref, o_ref, u_ref, acc_ref):
    x = x_ref[...]
    u_ref[...] = _norm_mod(x, nw_ref[...], sh_ref[...], sc_ref[...]).astype(BF16)
    for c in range(wg_ref.shape[0]):
        u = u_ref[...]
        g = jnp.dot(u, wg_ref[c], preferred_element_type=F32)
        up = jnp.dot(u, wu_ref[c], preferred_element_type=F32)
        act = ((g * _sigmoid(g)) * up).astype(BF16)
        part = jnp.dot(act, wo_ref[c], preferred_element_type=F32)
        if c == 0:
            acc_ref[...] = part
        else:
            acc_ref[...] += part
    o_ref[...] = x + (RESID_HALF * gt_ref[...]) * acc_ref[...]


def _mod_spec(mod_rows, tm, tiles_per_seq, k):
    if mod_rows == 1:
        return pl.BlockSpec((None, None, 1, D_MODEL), lambda i: (i // tiles_per_seq, k, 0, 0))
    return pl.BlockSpec((None, tm, D_MODEL), lambda i: (k, i, 0))


def _layer_resident(a, layer):
    return _resident((None,) + a.shape[1:], lambda *_: (layer,) + (0,) * (a.ndim - 1))


def _ffn(h, mod, kmod, nw, wg, wu, wo, layer, tm, tiles_per_seq, mod_rows):
    m = h.shape[0]
    row = pl.BlockSpec((tm, D_MODEL), lambda i: (i, 0))
    return pl.pallas_call(
        _ffn_kernel,
        grid=(m // tm,),
        in_specs=[row,
                  pl.BlockSpec((1, D_MODEL), lambda i: (0, 0)),
                  _mod_spec(mod_rows, tm, tiles_per_seq, kmod),
                  _mod_spec(mod_rows, tm, tiles_per_seq, kmod + 1),
                  _mod_spec(mod_rows, tm, tiles_per_seq, kmod + 2),
                  _layer_resident(wg, layer), _layer_resident(wu, layer), _layer_resident(wo, layer)],
        out_specs=row,
        out_shape=jax.ShapeDtypeStruct((m, D_MODEL), F32),
        scratch_shapes=[pltpu.VMEM((tm, D_MODEL), BF16), pltpu.VMEM((tm, D_MODEL), F32)],
        compiler_params=_params("parallel"),
        name="ffn",
    )(h, nw, mod, mod, mod, wg, wu, wo)


def _head_norm(x, g256, w):
    sq = x * x
    hi = sq.astype(BF16)
    lo = (sq - hi.astype(F32)).astype(BF16)
    ss = jnp.dot(hi, g256, preferred_element_type=F32) + jnp.dot(lo, g256, preferred_element_type=F32)
    return x * lax.rsqrt(ss * (1.0 / HD_F) + EPS) * w


def _inproj_kernel(x_ref, nw_ref, sh_ref, sc_ref, w_ref, g_ref, qnw_ref, knw_ref, sb_ref,
                   z_ref, xbc_ref, q_ref, k_ref, kb_ref, v_ref, vb_ref, ga_ref, gb_ref, sm_ref, u_ref, *, q_scale):
    u_ref[...] = _norm_mod(x_ref[...], nw_ref[...], sh_ref[...], sc_ref[...]).astype(BF16)

    def mm(lo, n):
        return jnp.dot(u_ref[...], w_ref[:, lo:lo + n], preferred_element_type=F32)

    t = 256
    for c in range(D_MODEL // t):
        z_ref[:, c * t:(c + 1) * t] = mm(OFF_Z + c * t, t)
        ga_ref[:, c * t:(c + 1) * t] = mm(OFF_GA + c * t, t)
        gb_ref[:, c * t:(c + 1) * t] = mm(OFF_GB + c * t, t)
        qn = _head_norm(mm(OFF_Q + c * t, t), g_ref[...], qnw_ref[...])
        q_ref[:, c * t:(c + 1) * t] = (qn * q_scale).astype(q_ref.dtype)
    for c in range(CONV_DIM // t):
        xbc_ref[:, c * t:(c + 1) * t] = mm(OFF_X + c * t, t)
    for c in range(KV_F * HD_F // t):
        kn = _head_norm(mm(OFF_K + c * t, t), g_ref[...], knw_ref[...])
        k_ref[:, c * t:(c + 1) * t] = kn
        kb_ref[:, c * t:(c + 1) * t] = kn.astype(BF16)
        vv = mm(OFF_V + c * t, t)
        v_ref[:, c * t:(c + 1) * t] = vv
        vb_ref[:, c * t:(c + 1) * t] = vv.astype(BF16)
    raw = mm(OFF_S, LANE) + sb_ref[...]
    tail = jnp.log1p(jnp.exp(-jnp.abs(raw)))
    lane = lax.broadcasted_iota(jnp.int32, raw.shape, 1)
    softplus = jnp.maximum(raw, 0.0) + tail
    log_sig = jnp.minimum(raw, 0.0) - tail
    sm_ref[...] = jnp.where(lane < H_A, softplus, jnp.where(lane < H_A + H_F, log_sig, 0.0))


def _inproj(h, mod, nw, w, layer, g256, qnw, knw, sbias, tm, tiles_per_seq, mod_rows, q_dtype, q_scale):
    m = h.shape[0]
    row = lambda n: pl.BlockSpec((tm, n), lambda i: (i, 0))
    const = lambda a: _resident(a.shape, lambda i: (0,) * a.ndim)
    kv = KV_F * HD_F
    outs = [(D_MODEL, F32), (CONV_DIM, F32), (H_F * HD_F, q_dtype), (kv, F32), (kv, BF16),
            (kv, F32), (kv, BF16), (D_MODEL, F32), (D_MODEL, F32), (LANE, F32)]
    return pl.pallas_call(
        functools.partial(_inproj_kernel, q_scale=q_scale),
        grid=(m // tm,),
        in_specs=[row(D_MODEL),
                  pl.BlockSpec((1, D_MODEL), lambda i: (0, 0)),
                  _mod_spec(mod_rows, tm, tiles_per_seq, 3),
                  _mod_spec(mod_rows, tm, tiles_per_seq, 4),
                  _layer_resident(w, layer), const(g256), const(qnw), const(knw), const(sbias)],
        out_specs=[row(n) for n, _ in outs],
        out_shape=[jax.ShapeDtypeStruct((m, n), dt) for n, dt in outs],
        scratch_shapes=[pltpu.VMEM((tm, D_MODEL), BF16)],
        compiler_params=_params("parallel"),
        name="inproj",
    )(h, nw, mod, mod, w, g256, qnw, knw, sbias)


def _fcum_kernel(sm_ref, kb_ref, tri_ref, pq_ref, pk_ref, oq_ref, ok_ref, fq_ref, kcat_ref, carry_ref):
    @pl.when(pl.program_id(1) == 0)
    def _():
        carry_ref[...] = jnp.zeros_like(carry_ref)

    cum = _dot01_l(tri_ref[...], sm_ref[...]) + carry_ref[0:1, :]
    carry_ref[0:1, :] = cum[cum.shape[0] - 1:, :]
    pieces = _split3(cum * LOG2E)

    def place(p_ref):
        return sum(jnp.dot(pieces[i], p_ref[i], preferred_element_type=F32) for i in range(3))

    fq_ref[...] = (place(pq_ref) + oq_ref[...]).astype(BF16)
    fk = (ok_ref[...] - place(pk_ref)).astype(BF16)
    for pair in range(KV_F // 2):
        kcat_ref[:, 2 * pair * LANE:(2 * pair + 1) * LANE] = kb_ref[:, pair * LANE:(pair + 1) * LANE]
        kcat_ref[:, (2 * pair + 1) * LANE:(2 * pair + 2) * LANE] = fk


def _fcum(small, kb, consts, bsz, s_len):
    tri = consts["tri256"]
    t = tri.shape[0]
    nt = s_len // t
    kvw = KV_F * HD_F
    row = lambda n: pl.BlockSpec((t, n), lambda b, i: (b * nt + i, 0))
    const = lambda a: _resident(a.shape, lambda b, i: (0,) * a.ndim)
    cs = [tri, consts["place_q"], consts["place_k"], consts["ones_q"], consts["ones_k"]]
    return pl.pallas_call(
        _fcum_kernel,
        grid=(bsz, nt),
        in_specs=[row(LANE), row(kvw)] + [const(a) for a in cs],
        out_specs=[row(LANE), row(2 * kvw)],
        out_shape=[jax.ShapeDtypeStruct((bsz * s_len, LANE), BF16),
                   jax.ShapeDtypeStruct((bsz * s_len, 2 * kvw), BF16)],
        scratch_shapes=[pltpu.VMEM((8, LANE), F32)],
        compiler_params=_params("parallel", "arbitrary"),
        name="fcum",
    )(small, kb, *cs)


def _mamba_kernel(xbc_ref, z_ref, sm_ref, cprev_ref, h0_ref, cw_ref, cb_ref, alog_ref, dsk_ref, gnw_ref,
                  tri_ref, e_ref, ya_ref, cnew_ref, ssm_ref, hist_ref, dt_ref, *, rows, valid):
    q = CHUNK
    c = pl.program_id(1)
    di = D_MODEL

    @pl.when(c == 0)
    def _():
        hist_ref[0:8, :] = cprev_ref[...]
        ssm_ref[...] = h0_ref[...]
        if rows < q:
            hist_ref[8 + rows:, :] = jnp.zeros((q - rows, CONV_DIM), F32)
            dt_ref[...] = jnp.zeros_like(dt_ref)

    hist_ref[8:8 + rows, :] = xbc_ref[...]
    conv = cb_ref[...] + hist_ref[5:5 + q, :] * cw_ref[0:1, :]
    for j in range(1, CONV_W):
        conv = conv + hist_ref[5 + j:5 + j + q, :] * cw_ref[j:j + 1, :]
    cnew_ref[...] = hist_ref[5 + valid:8 + valid, :]
    hist_ref[5:8, :] = hist_ref[5 + q:8 + q, :]
    act = conv * _sigmoid(conv)
    xs = act[:, :di]
    bmat = [act[:, di + g * N_A:di + (g + 1) * N_A].astype(BF16) for g in range(G_A)]
    cmat = [act[:, di + (G_A + g) * N_A:di + (G_A + g + 1) * N_A].astype(BF16) for g in range(G_A)]

    lane = lax.broadcasted_iota(jnp.int32, (1, LANE), 1)
    a_row = jnp.where(lane < H_A, -jnp.exp(alog_ref[...]), 0.0)
    if rows < q:
        rid = lax.broadcasted_iota(jnp.int32, (rows, LANE), 0)
        dt_ref[0:rows, :] = jnp.where(rid < valid, sm_ref[...], 0.0)
        dt = dt_ref[...]
    else:
        dt = sm_ref[...]
    tri = tri_ref[...]
    acum = _dot01_l(tri, dt * a_row)
    acum_t = acum.T
    expand = e_ref[...]
    dt_e = _dot01_r(dt, expand)
    acum_e = _dot01_r(acum, expand)
    xdt = xs * dt_e

    rq = lax.broadcasted_iota(jnp.int32, (q, q), 0)
    rs = lax.broadcasted_iota(jnp.int32, (q, q), 1)
    causal = rs <= rq
    half = lax.broadcasted_iota(jnp.int32, (q, LANE), 1) < P_A
    cb = [_dot_nt(cmat[g], bmat[g]) for g in range(G_A)]
    xdt_b = xdt.astype(BF16)
    y_parts = []
    for pair in range(H_A // 2):
        res = []
        for o in range(2):
            h = 2 * pair + o
            seg = acum[:, h:h + 1] - acum_t[h:h + 1, :]
            mix = (cb[h // (H_A // G_A)] * jnp.where(causal, jnp.exp(seg), 0.0)).astype(BF16)
            res.append(jnp.dot(mix, xdt_b[:, pair * LANE:(pair + 1) * LANE], preferred_element_type=F32))
        y_parts.append(jnp.where(half, res[0], res[1]))
    y = jnp.concatenate(y_parts, axis=1)

    hg = di // G_A
    state = ssm_ref[...]
    state_b = state.astype(BF16)
    y_off = jnp.concatenate([_dot_nt(cmat[g], state_b[g * hg:(g + 1) * hg, :]) for g in range(G_A)], axis=1)
    y = y + y_off * jnp.exp(acum_e)
    xdt_t = xdt.T
    acum_et = acum_e.T
    a_end = acum_et[:, q - 1:q]
    xdtd_t = (xdt_t * jnp.exp(a_end - acum_et)).astype(BF16)
    upd = jnp.concatenate(
        [jnp.dot(xdtd_t[g * hg:(g + 1) * hg, :], bmat[g], preferred_element_type=F32) for g in range(G_A)], axis=0)
    ssm_ref[...] = jnp.exp(a_end) * state + upd

    y = (y + dsk_ref[...] * xs)[0:rows, :]
    zz = z_ref[...]
    gated = y * (zz * _sigmoid(zz))
    out = gated * lax.rsqrt(jnp.mean(gated * gated, axis=-1, keepdims=True) + EPS) * gnw_ref[...]
    ya_ref[...] = out.astype(ya_ref.dtype)


def _mamba(xbc, z, small, cprev8, h0, layer, cw8, cb, alog, dsk, gnw, tri, expand, nseq, nchunk, rows, valid,
           ya_dtype):
    m = xbc.shape[0]
    row = lambda n: pl.BlockSpec((rows, n), lambda b, c: (b * nchunk + c, 0))
    const = lambda a: _resident(a.shape, lambda b, c: (0,) * a.ndim)
    di = D_MODEL
    kern = functools.partial(_mamba_kernel, rows=rows, valid=valid)
    return pl.pallas_call(
        kern,
        grid=(nseq, nchunk),
        in_specs=[row(CONV_DIM), row(di), row(LANE),
                  pl.BlockSpec((None, 8, CONV_DIM), lambda b, c: (b, 0, 0)),
                  pl.BlockSpec((None, None, di, N_A), lambda b, c: (layer, b, 0, 0)),
                  const(cw8), const(cb), const(alog), const(dsk), const(gnw), const(tri), const(expand)],
        out_specs=[row(di),
                   pl.BlockSpec((None, CONV_W - 1, CONV_DIM), lambda b, c: (b, 0, 0)),
                   pl.BlockSpec((None, di, N_A), lambda b, c: (b, 0, 0))],
        out_shape=[jax.ShapeDtypeStruct((m, di), ya_dtype),
                   jax.ShapeDtypeStruct((nseq, CONV_W - 1, CONV_DIM), F32),
                   jax.ShapeDtypeStruct((nseq, di, N_A), F32)],
        scratch_shapes=[pltpu.VMEM((8 + CHUNK, CONV_DIM), F32), pltpu.VMEM((CHUNK, LANE), F32)],
        compiler_params=_params("parallel", "arbitrary"),
        name="mamba",
    )(xbc, z, small, cprev8, h0, cw8, cb, alog, dsk, gnw, tri, expand)


def _attn_kernel(q_ref, fq_ref, k_ref, v_ref, o_ref, qcat_ref, m_ref, l_ref, acc_ref, *, tq, tk):
    qi = pl.program_id(1)
    ki = pl.program_id(2)
    nk = pl.num_programs(2)

    @pl.when(ki == 0)
    def _():
        m_ref[...] = jnp.full_like(m_ref, NEG)
        l_ref[...] = jnp.zeros_like(l_ref)
        acc_ref[...] = jnp.zeros_like(acc_ref)
        lane = lax.broadcasted_iota(jnp.int32, (tq, LANE), 1)
        lo = lane < HD_F
        fq = fq_ref[...]
        for blk in range(H_F // 2):
            qb = q_ref[:, blk * LANE:(blk + 1) * LANE]
            for o in range(2):
                slot = 2 * blk + o
                hq = HEAD_PERM[slot]
                own = jnp.logical_and(lane >= F_LANES * hq, lane < F_LANES * (hq + 1))
                qcat_ref[slot, :, 0:LANE] = jnp.where(lo if o == 0 else jnp.logical_not(lo), qb, jnp.zeros_like(qb))
                qcat_ref[slot, :, LANE:2 * LANE] = jnp.where(own, fq, jnp.zeros_like(fq))

    def step(diagonal):
        if diagonal:
            causal = (lax.broadcasted_iota(jnp.int32, (tk, tq), 0) <= lax.broadcasted_iota(jnp.int32, (tk, tq), 1))
        for slot in range(H_F):
            pair = slot // 4
            s = _dot_nt(k_ref[:, 2 * pair * LANE:(2 * pair + 2) * LANE], qcat_ref[slot])
            if diagonal:
                s = jnp.where(causal, s, NEG)
            m_old = m_ref[slot]
            m_new = jnp.maximum(m_old, jnp.max(s, axis=0, keepdims=True))
            alpha = jnp.exp2(m_old - m_new)
            p = jnp.exp2(s - m_new)
            l_ref[slot] = alpha * l_ref[slot] + jnp.sum(p, axis=0, keepdims=True)
            m_ref[slot] = m_new
            pv = lax.dot_general(v_ref[:, pair * LANE:(pair + 1) * LANE], p.astype(BF16),
                                 (((0,), (0,)), ((), ())), preferred_element_type=F32)
            acc_ref[slot] = alpha * acc_ref[slot] + pv

    pl.when(ki < qi)(functools.partial(step, False))
    pl.when(ki == qi)(functools.partial(step, True))

    @pl.when(ki == nk - 1)
    def _():
        lo = lax.broadcasted_iota(jnp.int32, (LANE, tq), 0) < HD_F
        for blk in range(H_F // 2):
            a0 = acc_ref[2 * blk] / l_ref[2 * blk]
            a1 = acc_ref[2 * blk + 1] / l_ref[2 * blk + 1]
            o_ref[:, blk * LANE:(blk + 1) * LANE] = jnp.where(lo, a0, a1).T.astype(o_ref.dtype)


def _attn(qb, fq, kcat, vb, bsz, s_len, tq):
    tk = tq
    nq = s_len // tq
    kern = functools.partial(_attn_kernel, tq=tq, tk=tk)
    kvw = KV_F * HD_F
    q_spec = lambda n: pl.BlockSpec((tq, n), lambda b, i, j: (b * nq + i, 0))
    k_spec = lambda n: pl.BlockSpec((tk, n), lambda b, i, j: (b * nq + jnp.minimum(i, j), 0))
    return pl.pallas_call(
        kern,
        grid=(bsz, nq, nq),
        in_specs=[q_spec(D_MODEL), q_spec(LANE), k_spec(2 * kvw), k_spec(kvw)],
        out_specs=q_spec(D_MODEL),
        out_shape=jax.ShapeDtypeStruct((bsz * s_len, D_MODEL), BF16),
        scratch_shapes=[pltpu.VMEM((H_F, tq, 2 * LANE), BF16),
                        pltpu.VMEM((H_F, 1, tq), F32), pltpu.VMEM((H_F, 1, tq), F32),
                        pltpu.VMEM((H_F, LANE, tq), F32)],
        compiler_params=_params("parallel", "parallel", "arbitrary"),
        name="attn",
    )(qb, fq, kcat, vb)


def _dec_kernel(pt_ref, q_ref, kn_ref, vn_ref, sm_ref, *rest, npp, tokens, valid):
    k_refs = rest[0:npp]
    v_refs = rest[npp:2 * npp]
    f_refs = rest[2 * npp:3 * npp]
    (et128_ref, tri_ref, slt_ref, o_ref, smp_ref, m_ref, l_ref, gq_ref, carry_ref, acc_ref) = rest[3 * npp:]
    g = pl.program_id(1)
    ng = pl.num_programs(1)
    kvw = KV_F * HD_F
    qbd = q_ref[...].astype(BF16)

    def update(scores, values_t):
        m_old = m_ref[...]
        m_new = jnp.maximum(m_old, jnp.max(functools.reduce(jnp.maximum, scores), axis=1, keepdims=True))
        alpha = jnp.exp(m_old - m_new)
        p_sum = jnp.zeros((LANE, PAGE), F32)
        pv = jnp.zeros((LANE, kvw), F32)
        for s, vt in zip(scores, values_t):
            p = jnp.exp(s - m_new)
            p_sum = p_sum + p
            pv = pv + _dot_nt(p.astype(BF16), vt.astype(BF16))
        l_ref[...] = alpha * l_ref[...] + jnp.sum(p_sum, axis=1, keepdims=True)
        m_ref[...] = m_new
        acc_ref[...] = alpha * acc_ref[...] + pv

    @pl.when(g == 0)
    def _():
        m_ref[...] = jnp.full_like(m_ref, NEG)
        l_ref[...] = jnp.zeros_like(l_ref)
        carry_ref[...] = jnp.zeros_like(carry_ref)
        acc_ref[...] = jnp.zeros_like(acc_ref)
        smp_ref[...] = jnp.zeros_like(smp_ref)
        smp_ref[0:tokens, :] = sm_ref[...]
        cum_t = _dot01_l(tri_ref[...], smp_ref[...]).T
        cum_e = _dot01_l(et128_ref[...], cum_t)
        key = lax.broadcasted_iota(jnp.int32, (LANE, PAGE), 1)
        qtok = lax.broadcasted_iota(jnp.int32, (LANE, PAGE), 0) % tokens
        g_q = jnp.sum(jnp.where(key == qtok, cum_e, 0.0), axis=1, keepdims=True)
        gq_ref[...] = g_q
        ok = jnp.logical_and(key <= qtok, key < valid)
        s = jnp.dot(qbd, kn_ref[...].astype(BF16), preferred_element_type=F32) + (g_q - cum_e)
        update([jnp.where(ok, s, NEG)], [vn_ref[...]])

    lf = jnp.concatenate([f_refs[i][...] for i in range(npp)], axis=0)
    later = _dot01_r(lf, slt_ref[...])
    mass = later[:, 0:1] + lf[:, 0:1]
    carry = carry_ref[...]
    g_q = gq_ref[...]
    scores, values_t = [], []
    for i in reversed(range(npp)):
        per_head = later[i * H_F:(i + 1) * H_F, :] + carry
        bias = jnp.concatenate(
            [jnp.broadcast_to(per_head[h:h + 1, :], (tokens, PAGE)) for h in range(H_F)], axis=0) + g_q
        kt = k_refs[i][...].reshape(kvw, PAGE).astype(BF16)
        scores.append(jnp.dot(qbd, kt, preferred_element_type=F32) + bias)
        values_t.append(v_refs[i][...].reshape(kvw, PAGE))
        carry = carry + mass[i * H_F:(i + 1) * H_F, :]
    update(scores, values_t)
    carry_ref[...] = carry

    @pl.when(g == ng - 1)
    def _():
        o_ref[...] = acc_ref[...] / l_ref[...]


def _dec_attn(page_table, qbd, kn_t, vn_t, small, cache_kt, cache_vt, cache_ft, layer, dc, tokens, valid, npp):
    nseq, n_pages = page_table.shape
    ng = n_pages // npp
    kvw = KV_F * HD_F
    rows = H_F * tokens
    kern = functools.partial(_dec_kernel, npp=npp, tokens=tokens, valid=valid)
    per_seq = lambda r, c: pl.BlockSpec((None, r, c), lambda b, g, pt: (b, 0, 0))
    page_idx = lambda b, g, pt, i: pt[b, (ng - 1 - g) * npp + i]
    kv_page = lambda i: pl.BlockSpec((None, None, KV_F, HD_F, PAGE),
                                     lambda b, g, pt: (layer, page_idx(b, g, pt, i), 0, 0, 0))
    f_page = lambda i: pl.BlockSpec((None, None, H_F, PAGE), lambda b, g, pt: (layer, page_idx(b, g, pt, i), 0, 0))
    const = lambda a: _resident(a.shape, lambda b, g, pt: (0,) * a.ndim)
    cs = [dc["et128"], dc["tri"], dc["slt"]]
    in_specs = ([per_seq(rows, kvw), per_seq(kvw, PAGE), per_seq(kvw, PAGE),
                 pl.BlockSpec((tokens, LANE), lambda b, g, pt: (b, 0))]
                + [kv_page(i) for i in range(npp)] + [kv_page(i) for i in range(npp)]
                + [f_page(i) for i in range(npp)] + [const(a) for a in cs])
    col = pltpu.VMEM((rows, 1), F32)
    grid_spec = pltpu.PrefetchScalarGridSpec(
        num_scalar_prefetch=1,
        grid=(nseq, ng),
        in_specs=in_specs,
        out_specs=per_seq(rows, kvw),
        scratch_shapes=[pltpu.VMEM((PAGE, LANE), F32), col, col, col, pltpu.VMEM((H_F, 1), F32),
                        pltpu.VMEM((rows, kvw), F32)])
    return pl.pallas_call(
        kern,
        grid_spec=grid_spec,
        out_shape=jax.ShapeDtypeStruct((nseq, rows, kvw), F32),
        compiler_params=_params("parallel", "arbitrary"),
        name="dec_attn",
    )(page_table, qbd, kn_t, vn_t, small, *([cache_kt] * npp), *([cache_vt] * npp), *([cache_ft] * npp), *cs)


def _merge_kernel(h_ref, ya_ref, yb_ref, ga_ref, gb_ref, gt_ref, wa_ref, wb_ref, wo_ref, o_ref):
    pa = jnp.dot(ya_ref[...].astype(BF16), wa_ref[...], preferred_element_type=F32)
    pb = jnp.dot(yb_ref[...].astype(BF16), wb_ref[...], preferred_element_type=F32)
    merged = _sigmoid(ga_ref[...]) * pa + _sigmoid(gb_ref[...]) * pb
    out = jnp.dot(merged.astype(BF16), wo_ref[...], preferred_element_type=F32)
    o_ref[...] = h_ref[...] + gt_ref[...] * out


def _merge(h, ya, yb, ga, gb, mod, wa, wb, wo, layer, tm, tiles_per_seq, mod_rows):
    m = h.shape[0]
    row = pl.BlockSpec((tm, D_MODEL), lambda i: (i, 0))
    return pl.pallas_call(
        _merge_kernel,
        grid=(m // tm,),
        in_specs=[row, row, row, row, row, _mod_spec(mod_rows, tm, tiles_per_seq, 5),
                  _layer_resident(wa, layer), _layer_resident(wb, layer), _layer_resident(wo, layer)],
        out_specs=row,
        out_shape=jax.ShapeDtypeStruct((m, D_MODEL), F32),
        compiler_params=_params("parallel"),
        name="merge",
    )(h, ya, yb, ga, gb, mod, wa, wb, wo)


def _constants():
    f = np.float32
    head = np.arange(D_MODEL) // HD_F
    g256 = (head[:256, None] == head[None, :256]).astype(f)
    tri128 = np.tril(np.ones((CHUNK, CHUNK), f))
    tri256 = np.tril(np.ones((256, 256), f))
    striu = np.triu(np.ones((PAGE, PAGE), f), 1)
    expand = np.zeros((LANE, D_MODEL), f)
    expand[head, np.arange(D_MODEL)] = 1.0
    place_q = np.zeros((3, LANE, LANE), f)
    place_k = np.zeros((3, LANE, LANE), f)
    ones_q = np.zeros((1, LANE), f)
    ones_k = np.zeros((1, LANE), f)
    for h in range(H_F):
        for i in range(3):
            place_q[i, H_A + h, F_LANES * h + i] = 1.0
            place_k[i, H_A + h, F_LANES * h + 3 + i] = 1.0
            ones_q[0, F_LANES * h + 3 + i] = 1.0
            ones_k[0, F_LANES * h + i] = 1.0
    out = {k: jnp.asarray(v, BF16) for k, v in
           dict(g256=g256, tri128=tri128, tri256=tri256, striu=striu, expand=expand,
                place_q=place_q, place_k=place_k).items()}
    out.update(ones_q=jnp.asarray(ones_q), ones_k=jnp.asarray(ones_k))
    return out


def _dec_constants(tokens, consts):
    f = np.float32
    et128 = np.zeros((H_F * tokens, LANE), f)
    for hq in range(H_F):
        et128[hq * tokens:(hq + 1) * tokens, H_A + hq] = 1.0
    slt = np.tril(np.ones((PAGE, PAGE), f), -1)
    return dict(et128=jnp.asarray(et128, BF16), tri=consts["tri128"], slt=jnp.asarray(slt, BF16))


def _ffn_weights(w_in, w_out):
    depth = w_in.shape[0]
    nc = D_FF // FF_TILE
    wb = w_in.astype(BF16)
    wg = wb[:, :, :D_FF].reshape(depth, D_MODEL, nc, FF_TILE).transpose(0, 2, 1, 3)
    wu = wb[:, :, D_FF:].reshape(depth, D_MODEL, nc, FF_TILE).transpose(0, 2, 1, 3)
    wo = w_out.astype(BF16).reshape(depth, nc, FF_TILE, D_MODEL)
    return wg, wu, wo


def _inproj_weights(w_in):
    depth = w_in.shape[0]
    sizes = (D_MODEL, CONV_DIM, H_A, H_F * HD_F, KV_F * HD_F, KV_F * HD_F, H_F, D_MODEL, D_MODEL)
    pts = np.cumsum(sizes)[:-1].tolist()
    wz, wx, wdt, wq, wk, wv, wf, wga, wgb = jnp.split(w_in, pts, axis=-1)
    wq = wq.reshape(depth, D_MODEL, H_F, HD_F)[:, :, list(HEAD_PERM), :].reshape(depth, D_MODEL, H_F * HD_F)
    pad = jnp.zeros((depth, D_MODEL, LANE - H_A - H_F), w_in.dtype)
    return jnp.concatenate([wz, wx, wq, wk, wv, wga, wgb, wdt, wf, pad], axis=-1).astype(BF16)


def _run_group(x2d, mod, mod_rows, tiles, conv_prev, ssm_prev, lw, consts, fox, nseq, nchunk, rows, valid,
               q_dtype, q_scale, y_dtype):
    tm_ffn, tm_in, tm_mg, tps_ffn, tps_in, tps_mg = tiles
    h = x2d
    states = []
    depth = lw["wg1"].shape[0]
    for l in range(depth):
        ml = mod[l]
        h = _ffn(h, ml, 0, lw["nw"][l, 0:1], lw["wg1"], lw["wu1"], lw["wo1"], l, tm_ffn, tps_ffn, mod_rows)
        z, xbc, q, k, kb, v, vb, ga, gb, small = _inproj(
            h, ml, lw["nw"][l, 1:2], lw["w_in"], l, consts["g256"], lw["qnw"][l], lw["knw"][l], lw["sbias"][l],
            tm_in, tps_in, mod_rows, q_dtype, q_scale)
        cprev8 = jnp.pad(conv_prev[l], ((0, 0), (8 - (CONV_W - 1), 0), (0, 0)))
        ya, conv_new, ssm_new = _mamba(
            xbc, z, small, cprev8, ssm_prev, l, lw["cw8"][l], lw["cb"][l], lw["alog"][l], lw["dsk"][l], lw["gnw"][l],
            consts["tri128"], consts["expand"], nseq, nchunk, rows, valid, y_dtype)
        yb = fox(l, q, k, kb, v, vb, small)
        h = _merge(h, ya, yb, ga, gb, ml, lw["wa"], lw["wb"], lw["wout"], l, tm_mg, tps_mg, mod_rows)
        h = _ffn(h, ml, 6, lw["nw"][l, 2:3], lw["wg2"], lw["wu2"], lw["wo2"], l, tm_ffn, tps_ffn, mod_rows)
        states.append((k, v, small, conv_new, ssm_new))
    return h, states


def kernel(x_prompt, x_sample, cache_k, cache_v, cache_logf, state_conv, state_ssm, page_table,
           c_prompt, c_sample, w_ada, b_ada, norm_w, w_ffn1_in, w_ffn1_out, w_ffn2_in, w_ffn2_out,
           w_in, conv_w, conv_b, dt_bias, a_log, d_skip, gn_w, q_norm_w, k_norm_w, b_f,
           w_proj_a, w_proj_b, w_out):
    bp, s_len, d = x_prompt.shape
    db, t_len, _ = x_sample.shape
    depth = w_in.shape[0]
    n_pool = cache_k.shape[1]
    kvw = KV_F * HD_F
    assert d == D_MODEL and s_len % CHUNK == 0 and t_len <= 8 and cache_k.shape[2] == PAGE
    consts = _constants()

    wg1, wu1, wo1 = _ffn_weights(w_ffn1_in, w_ffn1_out)
    wg2, wu2, wo2 = _ffn_weights(w_ffn2_in, w_ffn2_out)
    pad_lanes = lambda a: jnp.pad(a, ((0, 0), (0, LANE - a.shape[1])))[:, None, :]
    lw = dict(
        nw=norm_w, wg1=wg1, wu1=wu1, wo1=wo1, wg2=wg2, wu2=wu2, wo2=wo2,
        w_in=_inproj_weights(w_in),
        qnw=jnp.tile(q_norm_w, (1, 256 // HD_F))[:, None, :],
        knw=jnp.tile(k_norm_w, (1, 256 // HD_F))[:, None, :],
        sbias=pad_lanes(jnp.concatenate([dt_bias, b_f], axis=1)),
        cw8=jnp.pad(conv_w, ((0, 0), (0, 8 - CONV_W), (0, 0))),
        cb=conv_b[:, None, :],
        alog=pad_lanes(a_log),
        dsk=jnp.repeat(d_skip, P_A, axis=1)[:, None, :],
        gnw=gn_w[:, None, :],
        wa=w_proj_a.astype(BF16),
        wb=w_proj_b.reshape(depth, H_F, HD_F, d)[:, list(HEAD_PERM)].reshape(depth, d, d).astype(BF16),
        wout=w_out.astype(BF16),
    )

    mod_all = _ada(jnp.concatenate([c_prompt, c_sample], axis=0), w_ada, b_ada)
    mod_all = mod_all.reshape(depth, bp + db, N_MOD, d)
    mod_p = mod_all[:, :bp, :, None, :]
    tok = 8
    mod_s = jnp.broadcast_to(mod_all[:, bp:, None], (depth, db, tok, N_MOD, d))
    mod_s = mod_s.reshape(depth, db * tok, N_MOD, d).transpose(0, 2, 1, 3)

    tq = min(512, s_len)

    def fox_p(l, q, k, kb, v, vb, small):
        fq, kcat = _fcum(small, kb, consts, bp, s_len)
        return _attn(q, fq, kcat, vb, bp, s_len, tq)

    tm = min(512, s_len)
    tm_in = min(512, s_len)
    tiles_p = (tm, tm_in, tm, s_len // tm, s_len // tm_in, s_len // tm)
    conv0 = jnp.zeros((depth, bp, CONV_W - 1, CONV_DIM), F32)
    ssm0 = jnp.zeros((depth, bp, D_MODEL, N_A), F32)
    y_p, st_p = _run_group(x_prompt.reshape(bp * s_len, d), mod_p, 1, tiles_p, conv0, ssm0, lw, consts, fox_p,
                           bp, s_len // CHUNK, CHUNK, CHUNK, BF16, ATTN_SCALE * LOG2E, BF16)

    dc = _dec_constants(tok, consts)
    npp = next(n for n in (16, 8, 4, 2, 1) if page_table.shape[1] % n == 0)
    head_inv = list(np.argsort(HEAD_PERM))

    cache_kt = jnp.transpose(cache_k, (0, 1, 3, 4, 2))
    cache_vt = jnp.transpose(cache_v, (0, 1, 3, 4, 2))
    cache_ft = jnp.transpose(cache_logf, (0, 1, 3, 2))
    kv_of_head = np.arange(H_F) // (H_F // KV_F)
    own_kv = jnp.asarray(kv_of_head[:, None] == np.arange(KV_F)[None, :], F32)
    assert H_F * tok == LANE

    def fox_s(l, q, k, kb, v, vb, small):
        q2 = q.reshape(db, tok, H_F, HD_F)[:, :, head_inv].transpose(0, 2, 1, 3)
        qbd = (q2[:, :, :, None, :] * own_kv[None, :, None, :, None]).reshape(db, H_F * tok, kvw)
        new_t = lambda a: jnp.pad(a.reshape(db, tok, kvw).transpose(0, 2, 1), ((0, 0), (0, 0), (0, PAGE - tok)))
        out = _dec_attn(page_table, qbd, new_t(k), new_t(v), small, cache_kt, cache_vt, cache_ft, l, dc,
                        tok, t_len, npp)
        out = out.reshape(db, H_F, tok, KV_F, HD_F)
        pick = jnp.broadcast_to(jnp.asarray(kv_of_head)[None, :, None, None, None], (db, H_F, tok, 1, HD_F))
        out = jnp.take_along_axis(out, pick, axis=3)[:, :, :, 0]
        return out[:, list(HEAD_PERM)].transpose(0, 2, 1, 3).reshape(db * tok, d)

    ms = db * tok
    tiles_s = (ms, ms, ms, 1, 1, 1)
    x_s = jnp.pad(x_sample, ((0, 0), (0, tok - t_len), (0, 0))).reshape(ms, d)
    y_s, st_s = _run_group(x_s, mod_s, ms, tiles_s, state_conv, state_ssm.reshape(depth, db, D_MODEL, N_A), lw, consts,
                           fox_s, db, 1, tok, t_len, F32, ATTN_SCALE, F32)

    def gather(states, nseq, length, keep):
        k = jnp.stack([s[0] for s in states]).reshape(depth, nseq, length, KV_F, HD_F)[:, :, :keep]
        v = jnp.stack([s[1] for s in states]).reshape(depth, nseq, length, KV_F, HD_F)[:, :, :keep]
        lf = jnp.stack([s[2][:, H_A:H_A + H_F] for s in states]).reshape(depth, nseq, length, H_F)[:, :, :keep]
        conv = jnp.stack([s[3] for s in states])
        ssm = jnp.stack([s[4] for s in states]).reshape(depth, nseq, H_A, P_A, N_A)
        return k, v, lf, conv, ssm

    k_p, v_p, lf_p, conv_p, ssm_p = gather(st_p, bp, s_len, s_len)
    k_s, v_s, lf_s, conv_s, ssm_s = gather(st_s, db, tok, t_len)
    y_prompt = y_p.reshape(bp, s_len, d)
    y_sample = y_s.reshape(db, tok, d)[:, :t_len]
    return (y_prompt, y_sample, k_p, v_p, lf_p, conv_p, ssm_p, k_s, v_s, lf_s, conv_s, ssm_s)
```

```python
import functools

import numpy as np
import jax
import jax.numpy as jnp
from jax import lax
from jax.experimental import pallas as pl
from jax.experimental.pallas import tpu as pltpu

F32 = jnp.float32
BF16 = jnp.bfloat16

D_MODEL = 1024
D_FF = 2816
P_A = 64
H_A = 16
G_A = 2
N_A = 128
CONV_W = 4
CONV_DIM = D_MODEL + 2 * G_A * N_A
HD_F = 64
H_F = 16
KV_F = 8
N_MOD = 9
RESID_HALF = 0.5
EPS = 1e-6
ATTN_SCALE = HD_F ** -0.5
PAGE = 128
CHUNK = 128
NEG = -1e30
LOG2E = 1.4426950408889634
F_LANES = 6
DEC_ROWS = PAGE * KV_F

LANE = 128
FF_TILE = 256
OFF_Z = 0
OFF_X = OFF_Z + D_MODEL
OFF_Q = OFF_X + CONV_DIM
OFF_K = OFF_Q + H_F * HD_F
OFF_V = OFF_K + KV_F * HD_F
OFF_GA = OFF_V + KV_F * HD_F
OFF_GB = OFF_GA + D_MODEL
OFF_S = OFF_GB + D_MODEL
IN_W = OFF_S + LANE
HEAD_PERM = tuple(4 * j + o for j in range(4) for o in (0, 2, 1, 3))
VMEM_LIMIT = 56 * 1024 * 1024


def _params(*sem):
    return pltpu.CompilerParams(dimension_semantics=sem, vmem_limit_bytes=VMEM_LIMIT)


def _resident(shape, index_map):
    return pl.BlockSpec(shape, index_map, pipeline_mode=pl.Buffered(1))


def _sigmoid(x):
    return 1.0 / (1.0 + jnp.exp(-x))


def _norm_mod(x, nw, shift, scale):
    y = x * lax.rsqrt(jnp.mean(x * x, axis=-1, keepdims=True) + EPS) * nw
    return y * (1.0 + scale) + shift


def _split3(x):
    hi = x.astype(BF16)
    r = x - hi.astype(F32)
    mid = r.astype(BF16)
    lo = (r - mid.astype(F32)).astype(BF16)
    return hi, mid, lo


def _dot01_l(m01, x):
    hi, mid, lo = _split3(x)
    d = lambda p: jnp.dot(m01, p, preferred_element_type=F32)
    return d(hi) + d(mid) + d(lo)


def _dot01_r(x, m01):
    hi, mid, lo = _split3(x)
    d = lambda p: jnp.dot(p, m01, preferred_element_type=F32)
    return d(hi) + d(mid) + d(lo)


def _dot_nt(a, b):
    return lax.dot_general(a, b, (((1,), (1,)), ((), ())), preferred_element_type=F32)


def _ada_kernel(c_ref, w_ref, b_ref, o_ref):
    c = c_ref[...]
    a = (c * _sigmoid(c)).astype(BF16)
    o_ref[...] = jnp.dot(a, w_ref[...].astype(BF16), preferred_element_type=F32) + b_ref[...]


def _ada(c_all, w_ada, b_ada):
    depth, d, n = w_ada.shape
    rows = c_all.shape[0]
    tn = 1024
    return pl.pallas_call(
        _ada_kernel,
        grid=(depth, n // tn),
        in_specs=[pl.BlockSpec((rows, d), lambda l, j: (0, 0)),
                  pl.BlockSpec((None, d, tn), lambda l, j: (l, 0, j)),
                  pl.BlockSpec((None, 1, tn), lambda l, j: (l, 0, j))],
        out_specs=pl.BlockSpec((None, rows, tn), lambda l, j: (l, 0, j)),
        out_shape=jax.ShapeDtypeStruct((depth, rows, n), F32),
        compiler_params=_params("parallel", "parallel"),
        name="ada",
    )(c_all, w_ada, b_ada.reshape(depth, 1, n))


def _ffn_kernel(x_ref, nw_ref, sh_ref, sc_ref, gt_ref, wi_ref, wo_ref, o_ref, u_ref, acc_ref):
    x = x_ref[...]
    u_ref[...] = _norm_mod(x, nw_ref[...], sh_ref[...], sc_ref[...]).astype(BF16)
    for c in range(D_FF // FF_TILE):
        u = u_ref[...]
        lo = c * FF_TILE
        g = jnp.dot(u, wi_ref[:, lo:lo + FF_TILE], preferred_element_type=F32)
        up = jnp.dot(u, wi_ref[:, D_FF + lo:D_FF + lo + FF_TILE], preferred_element_type=F32)
        act = ((g * _sigmoid(g)) * up).astype(BF16)
        part = jnp.dot(act, wo_ref[lo:lo + FF_TILE, :], preferred_element_type=F32)
        if c == 0:
            acc_ref[...] = part
        else:
            acc_ref[...] += part
    o_ref[...] = x + (RESID_HALF * gt_ref[...]) * acc_ref[...]


def _mod_spec(mod_rows, tm, tiles_per_seq, k):
    if mod_rows == 1:
        return pl.BlockSpec((None, None, 1, D_MODEL), lambda i: (i // tiles_per_seq, k, 0, 0))
    return pl.BlockSpec((None, tm, D_MODEL), lambda i: (k, i, 0))


def _layer_resident(a, layer):
    return _resident((None,) + a.shape[1:], lambda *_: (layer,) + (0,) * (a.ndim - 1))


def _ffn(h, mod, kmod, nw, wi, wo, layer, tm, tiles_per_seq, mod_rows):
    m = h.shape[0]
    row = pl.BlockSpec((tm, D_MODEL), lambda i: (i, 0))
    return pl.pallas_call(
        _ffn_kernel,
        grid=(m // tm,),
        in_specs=[row,
                  pl.BlockSpec((1, D_MODEL), lambda i: (0, 0)),
                  _mod_spec(mod_rows, tm, tiles_per_seq, kmod),
                  _mod_spec(mod_rows, tm, tiles_per_seq, kmod + 1),
                  _mod_spec(mod_rows, tm, tiles_per_seq, kmod + 2),
                  _layer_resident(wi, layer), _layer_resident(wo, layer)],
        out_specs=row,
        out_shape=jax.ShapeDtypeStruct((m, D_MODEL), F32),
        scratch_shapes=[pltpu.VMEM((tm, D_MODEL), BF16), pltpu.VMEM((tm, D_MODEL), F32)],
        compiler_params=_params("parallel"),
        name="ffn",
    )(h, nw, mod, mod, mod, wi, wo)


def _head_norm(x, g256, w):
    sq = x * x
    hi = sq.astype(BF16)
    lo = (sq - hi.astype(F32)).astype(BF16)
    ss = jnp.dot(hi, g256, preferred_element_type=F32) + jnp.dot(lo, g256, preferred_element_type=F32)
    return x * lax.rsqrt(ss * (1.0 / HD_F) + EPS) * w


def _inproj_kernel(x_ref, nw_ref, sh_ref, sc_ref, w_ref, g_ref, qnw_ref, knw_ref, sb_ref, k_all_ref, v_all_ref,
                   z_ref, xbc_ref, q_ref, k_ref, kb_ref, v_ref, vb_ref, ga_ref, gb_ref, sm_ref, u_ref, *, q_scale):
    u_ref[...] = _norm_mod(x_ref[...], nw_ref[...], sh_ref[...], sc_ref[...]).astype(BF16)

    def mm(lo, n):
        return jnp.dot(u_ref[...], w_ref[:, lo:lo + n], preferred_element_type=F32)

    t = 256
    for c in range(D_MODEL // t):
        z_ref[:, c * t:(c + 1) * t] = mm(OFF_Z + c * t, t)
        ga_ref[:, c * t:(c + 1) * t] = mm(OFF_GA + c * t, t)
        gb_ref[:, c * t:(c + 1) * t] = mm(OFF_GB + c * t, t)
        qn = _head_norm(mm(OFF_Q + c * t, t), g_ref[...], qnw_ref[...])
        q_ref[:, c * t:(c + 1) * t] = (qn * q_scale).astype(q_ref.dtype)
    for c in range(CONV_DIM // t):
        xbc_ref[:, c * t:(c + 1) * t] = mm(OFF_X + c * t, t)
    for c in range(KV_F * HD_F // t):
        kn = _head_norm(mm(OFF_K + c * t, t), g_ref[...], knw_ref[...])
        k_ref[:, c * t:(c + 1) * t] = kn
        kb_ref[:, c * t:(c + 1) * t] = kn.astype(BF16)
        vv = mm(OFF_V + c * t, t)
        v_ref[:, c * t:(c + 1) * t] = vv
        vb_ref[:, c * t:(c + 1) * t] = vv.astype(BF16)
    raw = mm(OFF_S, LANE) + sb_ref[...]
    tail = jnp.log1p(jnp.exp(-jnp.abs(raw)))
    lane = lax.broadcasted_iota(jnp.int32, raw.shape, 1)
    softplus = jnp.maximum(raw, 0.0) + tail
    log_sig = jnp.minimum(raw, 0.0) - tail
    sm_ref[...] = jnp.where(lane < H_A, softplus, jnp.where(lane < H_A + H_F, log_sig, 0.0))


def _inproj(h, mod, nw, w, layer, g256, qnw, knw, sbias, k_all, v_all, tm, tiles_per_seq, mod_rows, q_dtype,
            q_scale):
    m = h.shape[0]
    row = lambda n: pl.BlockSpec((tm, n), lambda i: (i, 0))
    const = lambda a: _resident(a.shape, lambda i: (0,) * a.ndim)
    kv = KV_F * HD_F
    stacked = pl.BlockSpec((None, tm, kv), lambda i: (layer, i, 0))
    in_place = pl.BlockSpec(memory_space=pl.ANY)
    outs = [(D_MODEL, F32), (CONV_DIM, F32), (H_F * HD_F, q_dtype), None, (kv, BF16),
            None, (kv, BF16), (D_MODEL, F32), (D_MODEL, F32), (LANE, F32)]
    return pl.pallas_call(
        functools.partial(_inproj_kernel, q_scale=q_scale),
        grid=(m // tm,),
        in_specs=[row(D_MODEL),
                  pl.BlockSpec((1, D_MODEL), lambda i: (0, 0)),
                  _mod_spec(mod_rows, tm, tiles_per_seq, 3),
                  _mod_spec(mod_rows, tm, tiles_per_seq, 4),
                  _layer_resident(w, layer), const(g256), const(qnw), const(knw), const(sbias),
                  in_place, in_place],
        out_specs=[stacked if o is None else row(o[0]) for o in outs],
        out_shape=[jax.ShapeDtypeStruct(k_all.shape, F32) if o is None else jax.ShapeDtypeStruct((m, o[0]), o[1])
                   for o in outs],
        input_output_aliases={9: 3, 10: 5},
        scratch_shapes=[pltpu.VMEM((tm, D_MODEL), BF16)],
        compiler_params=_params("parallel"),
        name="inproj",
    )(h, nw, mod, mod, w, g256, qnw, knw, sbias, k_all, v_all)


def _fcum_kernel(sm_ref, kb_ref, tri_ref, pq_ref, pk_ref, oq_ref, ok_ref, fq_ref, kcat_ref, carry_ref):
    @pl.when(pl.program_id(1) == 0)
    def _():
        carry_ref[...] = jnp.zeros_like(carry_ref)

    cum = _dot01_l(tri_ref[...], sm_ref[...]) + carry_ref[0:1, :]
    carry_ref[0:1, :] = cum[cum.shape[0] - 1:, :]
    pieces = _split3(cum * LOG2E)

    def place(p_ref):
        return sum(jnp.dot(pieces[i], p_ref[i], preferred_element_type=F32) for i in range(3))

    fq_ref[...] = (place(pq_ref) + oq_ref[...]).astype(BF16)
    fk = (ok_ref[...] - place(pk_ref)).astype(BF16)
    for pair in range(KV_F // 2):
        kcat_ref[:, 2 * pair * LANE:(2 * pair + 1) * LANE] = kb_ref[:, pair * LANE:(pair + 1) * LANE]
        kcat_ref[:, (2 * pair + 1) * LANE:(2 * pair + 2) * LANE] = fk


def _fcum(small, kb, consts, bsz, s_len):
    tri = consts["tri256"]
    t = tri.shape[0]
    nt = s_len // t
    kvw = KV_F * HD_F
    row = lambda n: pl.BlockSpec((t, n), lambda b, i: (b * nt + i, 0))
    const = lambda a: _resident(a.shape, lambda b, i: (0,) * a.ndim)
    cs = [tri, consts["place_q"], consts["place_k"], consts["ones_q"], consts["ones_k"]]
    return pl.pallas_call(
        _fcum_kernel,
        grid=(bsz, nt),
        in_specs=[row(LANE), row(kvw)] + [const(a) for a in cs],
        out_specs=[row(LANE), row(2 * kvw)],
        out_shape=[jax.ShapeDtypeStruct((bsz * s_len, LANE), BF16),
                   jax.ShapeDtypeStruct((bsz * s_len, 2 * kvw), BF16)],
        scratch_shapes=[pltpu.VMEM((8, LANE), F32)],
        compiler_params=_params("parallel", "arbitrary"),
        name="fcum",
    )(small, kb, *cs)


def _mamba_kernel(xbc_ref, z_ref, sm_ref, cprev_ref, h0_ref, cw_ref, cb_ref, alog_ref, dsk_ref, gnw_ref,
                  tri_ref, e_ref, ya_ref, cnew_ref, ssm_ref, hist_ref, dt_ref, *, rows, valid):
    q = CHUNK
    c = pl.program_id(1)
    di = D_MODEL

    @pl.when(c == 0)
    def _():
        hist_ref[0:8, :] = cprev_ref[...]
        ssm_ref[...] = h0_ref[...]
        if rows < q:
            hist_ref[8 + rows:, :] = jnp.zeros((q - rows, CONV_DIM), F32)
            dt_ref[...] = jnp.zeros_like(dt_ref)

    hist_ref[8:8 + rows, :] = xbc_ref[...]
    conv = cb_ref[...] + hist_ref[5:5 + q, :] * cw_ref[0:1, :]
    for j in range(1, CONV_W):
        conv = conv + hist_ref[5 + j:5 + j + q, :] * cw_ref[j:j + 1, :]
    cnew_ref[...] = hist_ref[5 + valid:8 + valid, :]
    hist_ref[5:8, :] = hist_ref[5 + q:8 + q, :]
    act = conv * _sigmoid(conv)
    xs = act[:, :di]
    bmat = [act[:, di + g * N_A:di + (g + 1) * N_A].astype(BF16) for g in range(G_A)]
    cmat = [act[:, di + (G_A + g) * N_A:di + (G_A + g + 1) * N_A].astype(BF16) for g in range(G_A)]

    lane = lax.broadcasted_iota(jnp.int32, (1, LANE), 1)
    a_row = jnp.where(lane < H_A, -jnp.exp(alog_ref[...]), 0.0)
    if rows < q:
        rid = lax.broadcasted_iota(jnp.int32, (rows, LANE), 0)
        dt_ref[0:rows, :] = jnp.where(rid < valid, sm_ref[...], 0.0)
        dt = dt_ref[...]
    else:
        dt = sm_ref[...]
    tri = tri_ref[...]
    acum = _dot01_l(tri, dt * a_row)
    acum_t = acum.T
    expand = e_ref[...]
    dt_e = _dot01_r(dt, expand)
    acum_e = _dot01_r(acum, expand)
    xdt = xs * dt_e

    rq = lax.broadcasted_iota(jnp.int32, (q, q), 0)
    rs = lax.broadcasted_iota(jnp.int32, (q, q), 1)
    causal = rs <= rq
    half = lax.broadcasted_iota(jnp.int32, (q, LANE), 1) < P_A
    cb = [_dot_nt(cmat[g], bmat[g]) for g in range(G_A)]
    xdt_b = xdt.astype(BF16)
    y_parts = []
    for pair in range(H_A // 2):
        res = []
        for o in range(2):
            h = 2 * pair + o
            seg = acum[:, h:h + 1] - acum_t[h:h + 1, :]
            mix = (cb[h // (H_A // G_A)] * jnp.where(causal, jnp.exp(seg), 0.0)).astype(BF16)
            res.append(jnp.dot(mix, xdt_b[:, pair * LANE:(pair + 1) * LANE], preferred_element_type=F32))
        y_parts.append(jnp.where(half, res[0], res[1]))
    y = jnp.concatenate(y_parts, axis=1)

    hg = di // G_A
    state = ssm_ref[...]
    state_b = state.astype(BF16)
    y_off = jnp.concatenate([_dot_nt(cmat[g], state_b[g * hg:(g + 1) * hg, :]) for g in range(G_A)], axis=1)
    y = y + y_off * jnp.exp(acum_e)
    xdt_t = xdt.T
    acum_et = acum_e.T
    a_end = acum_et[:, q - 1:q]
    xdtd_t = (xdt_t * jnp.exp(a_end - acum_et)).astype(BF16)
    upd = jnp.concatenate(
        [jnp.dot(xdtd_t[g * hg:(g + 1) * hg, :], bmat[g], preferred_element_type=F32) for g in range(G_A)], axis=0)
    ssm_ref[...] = jnp.exp(a_end) * state + upd

    y = (y + dsk_ref[...] * xs)[0:rows, :]
    zz = z_ref[...]
    gated = y * (zz * _sigmoid(zz))
    out = gated * lax.rsqrt(jnp.mean(gated * gated, axis=-1, keepdims=True) + EPS) * gnw_ref[...]
    ya_ref[...] = out.astype(ya_ref.dtype)


def _mamba(xbc, z, small, cprev8, h0, layer, cw8, cb, alog, dsk, gnw, tri, expand, nseq, nchunk, rows, valid,
           ya_dtype):
    m = xbc.shape[0]
    row = lambda n: pl.BlockSpec((rows, n), lambda b, c: (b * nchunk + c, 0))
    const = lambda a: _resident(a.shape, lambda b, c: (0,) * a.ndim)
    di = D_MODEL
    kern = functools.partial(_mamba_kernel, rows=rows, valid=valid)
    return pl.pallas_call(
        kern,
        grid=(nseq, nchunk),
        in_specs=[row(CONV_DIM), row(di), row(LANE),
                  pl.BlockSpec((None, 8, CONV_DIM), lambda b, c: (b, 0, 0)),
                  pl.BlockSpec((None, None, di, N_A), lambda b, c: (layer, b, 0, 0)),
                  const(cw8), const(cb), const(alog), const(dsk), const(gnw), const(tri), const(expand)],
        out_specs=[row(di),
                   pl.BlockSpec((None, CONV_W - 1, CONV_DIM), lambda b, c: (b, 0, 0)),
                   pl.BlockSpec((None, di, N_A), lambda b, c: (b, 0, 0))],
        out_shape=[jax.ShapeDtypeStruct((m, di), ya_dtype),
                   jax.ShapeDtypeStruct((nseq, CONV_W - 1, CONV_DIM), F32),
                   jax.ShapeDtypeStruct((nseq, di, N_A), F32)],
        scratch_shapes=[pltpu.VMEM((8 + CHUNK, CONV_DIM), F32), pltpu.VMEM((CHUNK, LANE), F32)],
        compiler_params=_params("parallel", "arbitrary"),
        name="mamba",
    )(xbc, z, small, cprev8, h0, cw8, cb, alog, dsk, gnw, tri, expand)


def _attn_kernel(q_ref, fq_ref, k_ref, v_ref, o_ref, qcat_ref, m_ref, l_ref, acc_ref, *, tq, tk):
    qi = pl.program_id(1)
    ki = pl.program_id(2)
    nk = pl.num_programs(2)

    @pl.when(ki == 0)
    def _():
        m_ref[...] = jnp.full_like(m_ref, NEG)
        l_ref[...] = jnp.zeros_like(l_ref)
        acc_ref[...] = jnp.zeros_like(acc_ref)
        lane = lax.broadcasted_iota(jnp.int32, (tq, LANE), 1)
        lo = lane < HD_F
        fq = fq_ref[...]
        for blk in range(H_F // 2):
            qb = q_ref[:, blk * LANE:(blk + 1) * LANE]
            for o in range(2):
                slot = 2 * blk + o
                hq = HEAD_PERM[slot]
                own = jnp.logical_and(lane >= F_LANES * hq, lane < F_LANES * (hq + 1))
                qcat_ref[slot, :, 0:LANE] = jnp.where(lo if o == 0 else jnp.logical_not(lo), qb, jnp.zeros_like(qb))
                qcat_ref[slot, :, LANE:2 * LANE] = jnp.where(own, fq, jnp.zeros_like(fq))

    def step(diagonal):
        if diagonal:
            causal = (lax.broadcasted_iota(jnp.int32, (tk, tq), 0) <= lax.broadcasted_iota(jnp.int32, (tk, tq), 1))
        for slot in range(H_F):
            pair = slot // 4
            s = _dot_nt(k_ref[:, 2 * pair * LANE:(2 * pair + 2) * LANE], qcat_ref[slot])
            if diagonal:
                s = jnp.where(causal, s, NEG)
            m_old = m_ref[slot]
            m_new = jnp.maximum(m_old, jnp.max(s, axis=0, keepdims=True))
            alpha = jnp.exp2(m_old - m_new)
            p = jnp.exp2(s - m_new)
            l_ref[slot] = alpha * l_ref[slot] + jnp.sum(p, axis=0, keepdims=True)
            m_ref[slot] = m_new
            pv = lax.dot_general(v_ref[:, pair * LANE:(pair + 1) * LANE], p.astype(BF16),
                                 (((0,), (0,)), ((), ())), preferred_element_type=F32)
            acc_ref[slot] = alpha * acc_ref[slot] + pv

    pl.when(ki < qi)(functools.partial(step, False))
    pl.when(ki == qi)(functools.partial(step, True))

    @pl.when(ki == nk - 1)
    def _():
        lo = lax.broadcasted_iota(jnp.int32, (LANE, tq), 0) < HD_F
        for blk in range(H_F // 2):
            a0 = acc_ref[2 * blk] / l_ref[2 * blk]
            a1 = acc_ref[2 * blk + 1] / l_ref[2 * blk + 1]
            o_ref[:, blk * LANE:(blk + 1) * LANE] = jnp.where(lo, a0, a1).T.astype(o_ref.dtype)


def _attn(qb, fq, kcat, vb, bsz, s_len, tq):
    tk = tq
    nq = s_len // tq
    kern = functools.partial(_attn_kernel, tq=tq, tk=tk)
    kvw = KV_F * HD_F
    q_spec = lambda n: pl.BlockSpec((tq, n), lambda b, i, j: (b * nq + i, 0))
    k_spec = lambda n: pl.BlockSpec((tk, n), lambda b, i, j: (b * nq + jnp.minimum(i, j), 0))
    return pl.pallas_call(
        kern,
        grid=(bsz, nq, nq),
        in_specs=[q_spec(D_MODEL), q_spec(LANE), k_spec(2 * kvw), k_spec(kvw)],
        out_specs=q_spec(D_MODEL),
        out_shape=jax.ShapeDtypeStruct((bsz * s_len, D_MODEL), BF16),
        scratch_shapes=[pltpu.VMEM((H_F, tq, 2 * LANE), BF16),
                        pltpu.VMEM((H_F, 1, tq), F32), pltpu.VMEM((H_F, 1, tq), F32),
                        pltpu.VMEM((H_F, LANE, tq), F32)],
        compiler_params=_params("parallel", "parallel", "arbitrary"),
        name="attn",
    )(qb, fq, kcat, vb)


def _dec_kernel(pt_ref, q_ref, kn_ref, vn_ref, sm_ref, *rest, npp, tokens, valid):
    k_refs = rest[0:npp]
    v_refs = rest[npp:2 * npp]
    f_refs = rest[2 * npp:3 * npp]
    (et128_ref, tri_ref, slt_ref, o_ref, smp_ref, m_ref, l_ref, gq_ref, carry_ref, acc_ref) = rest[3 * npp:]
    g = pl.program_id(1)
    ng = pl.num_programs(1)
    kvw = KV_F * HD_F
    qbd = q_ref[...].astype(BF16)

    def update(scores, values_t):
        m_old = m_ref[...]
        m_new = jnp.maximum(m_old, jnp.max(functools.reduce(jnp.maximum, scores), axis=1, keepdims=True))
        alpha = jnp.exp(m_old - m_new)
        p_sum = jnp.zeros((LANE, PAGE), F32)
        pv = jnp.zeros((LANE, kvw), F32)
        for s, vt in zip(scores, values_t):
            p = jnp.exp(s - m_new)
            p_sum = p_sum + p
            pv = pv + _dot_nt(p.astype(BF16), vt.astype(BF16))
        l_ref[...] = alpha * l_ref[...] + jnp.sum(p_sum, axis=1, keepdims=True)
        m_ref[...] = m_new
        acc_ref[...] = alpha * acc_ref[...] + pv

    @pl.when(g == 0)
    def _():
        m_ref[...] = jnp.full_like(m_ref, NEG)
        l_ref[...] = jnp.zeros_like(l_ref)
        carry_ref[...] = jnp.zeros_like(carry_ref)
        acc_ref[...] = jnp.zeros_like(acc_ref)
        smp_ref[...] = jnp.zeros_like(smp_ref)
        smp_ref[0:tokens, :] = sm_ref[...]
        cum_t = _dot01_l(tri_ref[...], smp_ref[...]).T
        cum_e = _dot01_l(et128_ref[...], cum_t)
        key = lax.broadcasted_iota(jnp.int32, (LANE, PAGE), 1)
        qtok = lax.broadcasted_iota(jnp.int32, (LANE, PAGE), 0) % tokens
        g_q = jnp.sum(jnp.where(key == qtok, cum_e, 0.0), axis=1, keepdims=True)
        gq_ref[...] = g_q
        ok = jnp.logical_and(key <= qtok, key < valid)
        s = jnp.dot(qbd, kn_ref[...].astype(BF16), preferred_element_type=F32) + (g_q - cum_e)
        update([jnp.where(ok, s, NEG)], [vn_ref[...]])

    lf = jnp.concatenate([f_refs[i][...] for i in range(npp)], axis=0)
    later = _dot01_r(lf, slt_ref[...])
    mass = later[:, 0:1] + lf[:, 0:1]
    carry = carry_ref[...]
    g_q = gq_ref[...]
    scores, values_t = [], []
    for i in reversed(range(npp)):
        per_head = later[i * H_F:(i + 1) * H_F, :] + carry
        bias = jnp.concatenate(
            [jnp.broadcast_to(per_head[h:h + 1, :], (tokens, PAGE)) for h in range(H_F)], axis=0) + g_q
        kt = k_refs[i][...].reshape(kvw, PAGE).astype(BF16)
        scores.append(jnp.dot(qbd, kt, preferred_element_type=F32) + bias)
        values_t.append(v_refs[i][...].reshape(kvw, PAGE))
        carry = carry + mass[i * H_F:(i + 1) * H_F, :]
    update(scores, values_t)
    carry_ref[...] = carry

    @pl.when(g == ng - 1)
    def _():
        o_ref[...] = acc_ref[...] / l_ref[...]


def _dec_attn(page_table, qbd, kn_t, vn_t, small, cache_kt, cache_vt, cache_ft, layer, dc, tokens, valid, npp):
    nseq, n_pages = page_table.shape
    ng = n_pages // npp
    kvw = KV_F * HD_F
    rows = H_F * tokens
    kern = functools.partial(_dec_kernel, npp=npp, tokens=tokens, valid=valid)
    per_seq = lambda r, c: pl.BlockSpec((None, r, c), lambda b, g, pt: (b, 0, 0))
    page_idx = lambda b, g, pt, i: pt[b, (ng - 1 - g) * npp + i]
    kv_page = lambda i: pl.BlockSpec((None, None, KV_F, HD_F, PAGE),
                                     lambda b, g, pt: (layer, page_idx(b, g, pt, i), 0, 0, 0))
    f_page = lambda i: pl.BlockSpec((None, None, H_F, PAGE), lambda b, g, pt: (layer, page_idx(b, g, pt, i), 0, 0))
    const = lambda a: _resident(a.shape, lambda b, g, pt: (0,) * a.ndim)
    cs = [dc["et128"], dc["tri"], dc["slt"]]
    in_specs = ([per_seq(rows, kvw), per_seq(kvw, PAGE), per_seq(kvw, PAGE),
                 pl.BlockSpec((tokens, LANE), lambda b, g, pt: (b, 0))]
                + [kv_page(i) for i in range(npp)] + [kv_page(i) for i in range(npp)]
                + [f_page(i) for i in range(npp)] + [const(a) for a in cs])
    col = pltpu.VMEM((rows, 1), F32)
    grid_spec = pltpu.PrefetchScalarGridSpec(
        num_scalar_prefetch=1,
        grid=(nseq, ng),
        in_specs=in_specs,
        out_specs=per_seq(rows, kvw),
        scratch_shapes=[pltpu.VMEM((PAGE, LANE), F32), col, col, col, pltpu.VMEM((H_F, 1), F32),
                        pltpu.VMEM((rows, kvw), F32)])
    return pl.pallas_call(
        kern,
        grid_spec=grid_spec,
        out_shape=jax.ShapeDtypeStruct((nseq, rows, kvw), F32),
        compiler_params=_params("parallel", "arbitrary"),
        name="dec_attn",
    )(page_table, qbd, kn_t, vn_t, small, *([cache_kt] * npp), *([cache_vt] * npp), *([cache_ft] * npp), *cs)


def _merge_kernel(h_ref, ya_ref, yb_ref, ga_ref, gb_ref, gt_ref, wa_ref, wb_ref, wo_ref, o_ref):
    pa = jnp.dot(ya_ref[...].astype(BF16), wa_ref[...], preferred_element_type=F32)
    pb = jnp.dot(yb_ref[...].astype(BF16), wb_ref[...], preferred_element_type=F32)
    merged = _sigmoid(ga_ref[...]) * pa + _sigmoid(gb_ref[...]) * pb
    out = jnp.dot(merged.astype(BF16), wo_ref[...], preferred_element_type=F32)
    o_ref[...] = h_ref[...] + gt_ref[...] * out


def _merge(h, ya, yb, ga, gb, mod, wa, wb, wo, layer, tm, tiles_per_seq, mod_rows):
    m = h.shape[0]
    row = pl.BlockSpec((tm, D_MODEL), lambda i: (i, 0))
    return pl.pallas_call(
        _merge_kernel,
        grid=(m // tm,),
        in_specs=[row, row, row, row, row, _mod_spec(mod_rows, tm, tiles_per_seq, 5),
                  _layer_resident(wa, layer), _layer_resident(wb, layer), _layer_resident(wo, layer)],
        out_specs=row,
        out_shape=jax.ShapeDtypeStruct((m, D_MODEL), F32),
        compiler_params=_params("parallel"),
        name="merge",
    )(h, ya, yb, ga, gb, mod, wa, wb, wo)


def _constants():
    f = np.float32
    head = np.arange(D_MODEL) // HD_F
    g256 = (head[:256, None] == head[None, :256]).astype(f)
    tri128 = np.tril(np.ones((CHUNK, CHUNK), f))
    tri256 = np.tril(np.ones((256, 256), f))
    striu = np.triu(np.ones((PAGE, PAGE), f), 1)
    expand = np.zeros((LANE, D_MODEL), f)
    expand[head, np.arange(D_MODEL)] = 1.0
    place_q = np.zeros((3, LANE, LANE), f)
    place_k = np.zeros((3, LANE, LANE), f)
    ones_q = np.zeros((1, LANE), f)
    ones_k = np.zeros((1, LANE), f)
    for h in range(H_F):
        for i in range(3):
            place_q[i, H_A + h, F_LANES * h + i] = 1.0
            place_k[i, H_A + h, F_LANES * h + 3 + i] = 1.0
            ones_q[0, F_LANES * h + 3 + i] = 1.0
            ones_k[0, F_LANES * h + i] = 1.0
    out = {k: jnp.asarray(v, BF16) for k, v in
           dict(g256=g256, tri128=tri128, tri256=tri256, striu=striu, expand=expand,
                place_q=place_q, place_k=place_k).items()}
    out.update(ones_q=jnp.asarray(ones_q), ones_k=jnp.asarray(ones_k))
    return out


def _dec_constants(tokens, consts):
    f = np.float32
    et128 = np.zeros((H_F * tokens, LANE), f)
    for hq in range(H_F):
        et128[hq * tokens:(hq + 1) * tokens, H_A + hq] = 1.0
    slt = np.tril(np.ones((PAGE, PAGE), f), -1)
    return dict(et128=jnp.asarray(et128, BF16), tri=consts["tri128"], slt=jnp.asarray(slt, BF16))


def _inproj_weights(w_in):
    depth = w_in.shape[0]
    sizes = (D_MODEL, CONV_DIM, H_A, H_F * HD_F, KV_F * HD_F, KV_F * HD_F, H_F, D_MODEL, D_MODEL)
    pts = np.cumsum(sizes)[:-1].tolist()
    wz, wx, wdt, wq, wk, wv, wf, wga, wgb = jnp.split(w_in.astype(BF16), pts, axis=-1)
    wq = wq.reshape(depth, D_MODEL, H_F, HD_F)[:, :, list(HEAD_PERM), :].reshape(depth, D_MODEL, H_F * HD_F)
    pad = jnp.zeros((depth, D_MODEL, LANE - H_A - H_F), BF16)
    return jnp.concatenate([wz, wx, wq, wk, wv, wga, wgb, wdt, wf, pad], axis=-1)


def _run_group(x2d, mod, mod_rows, tiles, conv_prev, ssm_prev, lw, consts, fox, nseq, nchunk, rows, valid,
               q_dtype, q_scale, y_dtype):
    tm_ffn, tm_in, tm_mg, tps_ffn, tps_in, tps_mg = tiles
    h = x2d
    states = []
    depth = lw["w_in"].shape[0]
    k_all = jnp.zeros((depth, x2d.shape[0], KV_F * HD_F), F32)
    v_all = jnp.zeros((depth, x2d.shape[0], KV_F * HD_F), F32)
    for l in range(depth):
        ml = mod[l]
        h = _ffn(h, ml, 0, lw["nw"][l, 0:1], lw["wi1"], lw["wo1"], l, tm_ffn, tps_ffn, mod_rows)
        z, xbc, q, k_all, kb, v_all, vb, ga, gb, small = _inproj(
            h, ml, lw["nw"][l, 1:2], lw["w_in"], l, consts["g256"], lw["qnw"][l], lw["knw"][l], lw["sbias"][l],
            k_all, v_all, tm_in, tps_in, mod_rows, q_dtype, q_scale)
        cprev8 = jnp.pad(conv_prev[l], ((0, 0), (8 - (CONV_W - 1), 0), (0, 0)))
        ya, conv_new, ssm_new = _mamba(
            xbc, z, small, cprev8, ssm_prev, l, lw["cw8"][l], lw["cb"][l], lw["alog"][l], lw["dsk"][l], lw["gnw"][l],
            consts["tri128"], consts["expand"], nseq, nchunk, rows, valid, y_dtype)
        yb = fox(l, q, k_all, kb, v_all, vb, small)
        h = _merge(h, ya, yb, ga, gb, ml, lw["wa"], lw["wb"], lw["wout"], l, tm_mg, tps_mg, mod_rows)
        h = _ffn(h, ml, 6, lw["nw"][l, 2:3], lw["wi2"], lw["wo2"], l, tm_ffn, tps_ffn, mod_rows)
        states.append((small, conv_new, ssm_new))
    return h, k_all, v_all, states


def kernel(x_prompt, x_sample, cache_k, cache_v, cache_logf, state_conv, state_ssm, page_table,
           c_prompt, c_sample, w_ada, b_ada, norm_w, w_ffn1_in, w_ffn1_out, w_ffn2_in, w_ffn2_out,
           w_in, conv_w, conv_b, dt_bias, a_log, d_skip, gn_w, q_norm_w, k_norm_w, b_f,
           w_proj_a, w_proj_b, w_out):
    bp, s_len, d = x_prompt.shape
    db, t_len, _ = x_sample.shape
    depth = w_in.shape[0]
    n_pool = cache_k.shape[1]
    kvw = KV_F * HD_F
    assert d == D_MODEL and s_len % CHUNK == 0 and t_len <= 8 and cache_k.shape[2] == PAGE
    consts = _constants()

    pad_lanes = lambda a: jnp.pad(a, ((0, 0), (0, LANE - a.shape[1])))[:, None, :]
    lw = dict(
        nw=norm_w,
        wi1=w_ffn1_in.astype(BF16), wo1=w_ffn1_out.astype(BF16),
        wi2=w_ffn2_in.astype(BF16), wo2=w_ffn2_out.astype(BF16),
        w_in=_inproj_weights(w_in),
        qnw=jnp.tile(q_norm_w, (1, 256 // HD_F))[:, None, :],
        knw=jnp.tile(k_norm_w, (1, 256 // HD_F))[:, None, :],
        sbias=pad_lanes(jnp.concatenate([dt_bias, b_f], axis=1)),
        cw8=jnp.pad(conv_w, ((0, 0), (0, 8 - CONV_W), (0, 0))),
        cb=conv_b[:, None, :],
        alog=pad_lanes(a_log),
        dsk=jnp.repeat(d_skip, P_A, axis=1)[:, None, :],
        gnw=gn_w[:, None, :],
        wa=w_proj_a.astype(BF16),
        wb=w_proj_b.reshape(depth, H_F, HD_F, d)[:, list(HEAD_PERM)].reshape(depth, d, d).astype(BF16),
        wout=w_out.astype(BF16),
    )

    mod_all = _ada(jnp.concatenate([c_prompt, c_sample], axis=0), w_ada, b_ada)
    mod_all = mod_all.reshape(depth, bp + db, N_MOD, d)
    mod_p = mod_all[:, :bp, :, None, :]
    tok = 8
    mod_s = jnp.broadcast_to(mod_all[:, bp:, None], (depth, db, tok, N_MOD, d))
    mod_s = mod_s.reshape(depth, db * tok, N_MOD, d).transpose(0, 2, 1, 3)

    tq = min(512, s_len)

    def fox_p(l, q, k, kb, v, vb, small):
        fq, kcat = _fcum(small, kb, consts, bp, s_len)
        return _attn(q, fq, kcat, vb, bp, s_len, tq)

    tm = min(512, s_len)
    tm_in = min(512, s_len)
    tiles_p = (tm, tm_in, tm, s_len // tm, s_len // tm_in, s_len // tm)
    conv0 = jnp.zeros((depth, bp, CONV_W - 1, CONV_DIM), F32)
    ssm0 = jnp.zeros((depth, bp, D_MODEL, N_A), F32)
    y_p, kp_all, vp_all, st_p = _run_group(x_prompt.reshape(bp * s_len, d), mod_p, 1, tiles_p, conv0, ssm0, lw, consts, fox_p,
                           bp, s_len // CHUNK, CHUNK, CHUNK, BF16, ATTN_SCALE * LOG2E, BF16)

    dc = _dec_constants(tok, consts)
    npp = next(n for n in (16, 8, 4, 2, 1) if page_table.shape[1] % n == 0)
    head_inv = list(np.argsort(HEAD_PERM))

    cache_kt = jnp.transpose(cache_k, (0, 1, 3, 4, 2))
    cache_vt = jnp.transpose(cache_v, (0, 1, 3, 4, 2))
    cache_ft = jnp.transpose(cache_logf, (0, 1, 3, 2))
    kv_of_head = np.arange(H_F) // (H_F // KV_F)
    own_kv = jnp.asarray(kv_of_head[:, None] == np.arange(KV_F)[None, :], F32)
    assert H_F * tok == LANE

    def fox_s(l, q, k, kb, v, vb, small):
        q2 = q.reshape(db, tok, H_F, HD_F)[:, :, head_inv].transpose(0, 2, 1, 3)
        qbd = (q2[:, :, :, None, :] * own_kv[None, :, None, :, None]).reshape(db, H_F * tok, kvw)
        new_t = lambda a: jnp.pad(a.reshape(db, tok, kvw).transpose(0, 2, 1), ((0, 0), (0, 0), (0, PAGE - tok)))
        out = _dec_attn(page_table, qbd, new_t(k[l]), new_t(v[l]), small, cache_kt, cache_vt, cache_ft, l, dc,
                        tok, t_len, npp)
        out = out.reshape(db, H_F, tok, KV_F, HD_F)
        pick = jnp.broadcast_to(jnp.asarray(kv_of_head)[None, :, None, None, None], (db, H_F, tok, 1, HD_F))
        out = jnp.take_along_axis(out, pick, axis=3)[:, :, :, 0]
        return out[:, list(HEAD_PERM)].transpose(0, 2, 1, 3).reshape(db * tok, d)

    ms = db * tok
    tiles_s = (ms, ms, ms, 1, 1, 1)
    x_s = jnp.pad(x_sample, ((0, 0), (0, tok - t_len), (0, 0))).reshape(ms, d)
    y_s, ks_all, vs_all, st_s = _run_group(x_s, mod_s, ms, tiles_s, state_conv, state_ssm.reshape(depth, db, D_MODEL, N_A), lw, consts,
                           fox_s, db, 1, tok, t_len, F32, ATTN_SCALE, F32)

    def gather(k_all, v_all, states, nseq, length, keep):
        k = k_all.reshape(depth, nseq, length, KV_F, HD_F)[:, :, :keep]
        v = v_all.reshape(depth, nseq, length, KV_F, HD_F)[:, :, :keep]
        lf = jnp.stack([s[0][:, H_A:H_A + H_F] for s in states]).reshape(depth, nseq, length, H_F)[:, :, :keep]
        conv = jnp.stack([s[1] for s in states])
        ssm = jnp.stack([s[2] for s in states]).reshape(depth, nseq, H_A, P_A, N_A)
        return k, v, lf, conv, ssm

    k_p, v_p, lf_p, conv_p, ssm_p = gather(kp_all, vp_all, st_p, bp, s_len, s_len)
    k_s, v_s, lf_s, conv_s, ssm_s = gather(ks_all, vs_all, st_s, db, tok, t_len)
    y_prompt = y_p.reshape(bp, s_len, d)
    y_sample = y_s.reshape(db, tok, d)[:, :t_len]
    return (y_prompt, y_sample, k_p, v_p, lf_p, conv_p, ssm_p, k_s, v_s, lf_s, conv_s, ssm_s)
```

```python
import functools

import numpy as np
import jax
import jax.numpy as jnp
from jax import lax
from jax.experimental import pallas as pl
from jax.experimental.pallas import tpu as pltpu

F32 = jnp.float32
BF16 = jnp.bfloat16

D_MODEL = 1024
D_FF = 2816
P_A = 64
H_A = 16
G_A = 2
N_A = 128
CONV_W = 4
CONV_DIM = D_MODEL + 2 * G_A * N_A
HD_F = 64
H_F = 16
KV_F = 8
N_MOD = 9
RESID_HALF = 0.5
EPS = 1e-6
ATTN_SCALE = HD_F ** -0.5
PAGE = 128
CHUNK = 128
NEG = -1e30
LOG2E = 1.4426950408889634
F_LANES = 6
DEC_ROWS = PAGE * KV_F

LANE = 128
FF_TILE = 256
OFF_Z = 0
OFF_X = OFF_Z + D_MODEL
OFF_Q = OFF_X + CONV_DIM
OFF_K = OFF_Q + H_F * HD_F
OFF_V = OFF_K + KV_F * HD_F
OFF_GA = OFF_V + KV_F * HD_F
OFF_GB = OFF_GA + D_MODEL
OFF_S = OFF_GB + D_MODEL
IN_W = OFF_S + LANE
HEAD_PERM = tuple(4 * j + o for j in range(4) for o in (0, 2, 1, 3))
VMEM_LIMIT = 56 * 1024 * 1024


def _params(*sem):
    return pltpu.CompilerParams(dimension_semantics=sem, vmem_limit_bytes=VMEM_LIMIT)


def _resident(shape, index_map):
    return pl.BlockSpec(shape, index_map, pipeline_mode=pl.Buffered(1))


def _sigmoid(x):
    return 1.0 / (1.0 + jnp.exp(-x))


def _norm_mod(x, nw, shift, scale):
    y = x * lax.rsqrt(jnp.mean(x * x, axis=-1, keepdims=True) + EPS) * nw
    return y * (1.0 + scale) + shift


def _split3(x):
    hi = x.astype(BF16)
    r = x - hi.astype(F32)
    mid = r.astype(BF16)
    lo = (r - mid.astype(F32)).astype(BF16)
    return hi, mid, lo


def _dot01_l(m01, x):
    hi, mid, lo = _split3(x)
    d = lambda p: jnp.dot(m01, p, preferred_element_type=F32)
    return d(hi) + d(mid) + d(lo)


def _dot01_r(x, m01):
    hi, mid, lo = _split3(x)
    d = lambda p: jnp.dot(p, m01, preferred_element_type=F32)
    return d(hi) + d(mid) + d(lo)


def _dot_nt(a, b):
    return lax.dot_general(a, b, (((1,), (1,)), ((), ())), preferred_element_type=F32)


def _ada_kernel(c_ref, w_ref, b_ref, o_ref):
    c = c_ref[...]
    a = (c * _sigmoid(c)).astype(BF16)
    o_ref[...] = jnp.dot(a, w_ref[...].astype(BF16), preferred_element_type=F32) + b_ref[...]


def _ada(c_all, w_ada, b_ada):
    depth, d, n = w_ada.shape
    rows = c_all.shape[0]
    tn = 1024
    return pl.pallas_call(
        _ada_kernel,
        grid=(depth, n // tn),
        in_specs=[pl.BlockSpec((rows, d), lambda l, j: (0, 0)),
                  pl.BlockSpec((None, d, tn), lambda l, j: (l, 0, j)),
                  pl.BlockSpec((None, 1, tn), lambda l, j: (l, 0, j))],
        out_specs=pl.BlockSpec((None, rows, tn), lambda l, j: (l, 0, j)),
        out_shape=jax.ShapeDtypeStruct((depth, rows, n), F32),
        compiler_params=_params("parallel", "parallel"),
        name="ada",
    )(c_all, w_ada, b_ada.reshape(depth, 1, n))


def _ffn_kernel(x_ref, nw_ref, sh_ref, sc_ref, gt_ref, wi_ref, wo_ref, o_ref, u_ref, acc_ref):
    x = x_ref[...]
    u_ref[...] = _norm_mod(x, nw_ref[...], sh_ref[...], sc_ref[...]).astype(BF16)
    for c in range(D_FF // FF_TILE):
        u = u_ref[...]
        lo = c * FF_TILE
        g = jnp.dot(u, wi_ref[:, lo:lo + FF_TILE], preferred_element_type=F32)
        up = jnp.dot(u, wi_ref[:, D_FF + lo:D_FF + lo + FF_TILE], preferred_element_type=F32)
        act = ((g * _sigmoid(g)) * up).astype(BF16)
        part = jnp.dot(act, wo_ref[lo:lo + FF_TILE, :], preferred_element_type=F32)
        if c == 0:
            acc_ref[...] = part
        else:
            acc_ref[...] += part
    o_ref[...] = x + (RESID_HALF * gt_ref[...]) * acc_ref[...]


def _mod_spec(mod_rows, tm, tiles_per_seq, k):
    if mod_rows == 1:
        return pl.BlockSpec((None, None, 1, D_MODEL), lambda i: (i // tiles_per_seq, k, 0, 0))
    return pl.BlockSpec((None, tm, D_MODEL), lambda i: (k, i, 0))


def _layer_resident(a, layer):
    return _resident((None,) + a.shape[1:], lambda *_: (layer,) + (0,) * (a.ndim - 1))


def _ffn(h, mod, kmod, nw, wi, wo, layer, tm, tiles_per_seq, mod_rows):
    m = h.shape[0]
    row = pl.BlockSpec((tm, D_MODEL), lambda i: (i, 0))
    return pl.pallas_call(
        _ffn_kernel,
        grid=(m // tm,),
        in_specs=[row,
                  pl.BlockSpec((1, D_MODEL), lambda i: (0, 0)),
                  _mod_spec(mod_rows, tm, tiles_per_seq, kmod),
                  _mod_spec(mod_rows, tm, tiles_per_seq, kmod + 1),
                  _mod_spec(mod_rows, tm, tiles_per_seq, kmod + 2),
                  _layer_resident(wi, layer), _layer_resident(wo, layer)],
        out_specs=row,
        out_shape=jax.ShapeDtypeStruct((m, D_MODEL), F32),
        scratch_shapes=[pltpu.VMEM((tm, D_MODEL), BF16), pltpu.VMEM((tm, D_MODEL), F32)],
        compiler_params=_params("parallel"),
        name="ffn",
    )(h, nw, mod, mod, mod, wi, wo)


def _head_norm(x, g256, w):
    sq = x * x
    hi = sq.astype(BF16)
    lo = (sq - hi.astype(F32)).astype(BF16)
    ss = jnp.dot(hi, g256, preferred_element_type=F32) + jnp.dot(lo, g256, preferred_element_type=F32)
    return x * lax.rsqrt(ss * (1.0 / HD_F) + EPS) * w


def _inproj_kernel(x_ref, nw_ref, sh_ref, sc_ref, w_ref, g_ref, qnw_ref, knw_ref, sb_ref, k_all_ref, v_all_ref,
                   z_ref, xbc_ref, q_ref, k_ref, kb_ref, v_ref, vb_ref, ga_ref, gb_ref, sm_ref, u_ref, *, q_scale,
                   kv_major):
    u_ref[...] = _norm_mod(x_ref[...], nw_ref[...], sh_ref[...], sc_ref[...]).astype(BF16)

    def mm(lo, n):
        return jnp.dot(u_ref[...], w_ref[:, lo:lo + n], preferred_element_type=F32)

    t = 256
    for c in range(D_MODEL // t):
        z_ref[:, c * t:(c + 1) * t] = mm(OFF_Z + c * t, t)
        ga_ref[:, c * t:(c + 1) * t] = mm(OFF_GA + c * t, t)
        gb_ref[:, c * t:(c + 1) * t] = mm(OFF_GB + c * t, t)
        qn = _head_norm(mm(OFF_Q + c * t, t), g_ref[...], qnw_ref[...])
        q_ref[:, c * t:(c + 1) * t] = (qn * q_scale).astype(q_ref.dtype)
    for c in range(CONV_DIM // t):
        xbc_ref[:, c * t:(c + 1) * t] = mm(OFF_X + c * t, t)
    for c in range(KV_F * HD_F // t):
        kn = _head_norm(mm(OFF_K + c * t, t), g_ref[...], knw_ref[...])
        kb_ref[:, c * t:(c + 1) * t] = kn.astype(BF16)
        vv = mm(OFF_V + c * t, t)
        vb_ref[:, c * t:(c + 1) * t] = vv.astype(BF16)
        if kv_major:
            k_ref[c * t:(c + 1) * t, :] = kn.T
            v_ref[c * t:(c + 1) * t, :] = vv.T
        else:
            k_ref[:, c * t:(c + 1) * t] = kn
            v_ref[:, c * t:(c + 1) * t] = vv
    raw = mm(OFF_S, LANE) + sb_ref[...]
    tail = jnp.log1p(jnp.exp(-jnp.abs(raw)))
    lane = lax.broadcasted_iota(jnp.int32, raw.shape, 1)
    softplus = jnp.maximum(raw, 0.0) + tail
    log_sig = jnp.minimum(raw, 0.0) - tail
    sm_ref[...] = jnp.where(lane < H_A, softplus, jnp.where(lane < H_A + H_F, log_sig, 0.0))


def _inproj(h, mod, nw, w, layer, g256, qnw, knw, sbias, k_all, v_all, tm, tiles_per_seq, mod_rows, q_dtype,
            q_scale):
    m = h.shape[0]
    row = lambda n: pl.BlockSpec((tm, n), lambda i: (i, 0))
    const = lambda a: _resident(a.shape, lambda i: (0,) * a.ndim)
    kv = KV_F * HD_F
    kv_major = k_all.ndim == 4
    if kv_major:
        stacked = pl.BlockSpec((None, None, kv, tm), lambda i: (layer, i // tiles_per_seq, 0, i % tiles_per_seq))
    else:
        stacked = pl.BlockSpec((None, tm, kv), lambda i: (layer, i, 0))
    in_place = pl.BlockSpec(memory_space=pl.ANY)
    outs = [(D_MODEL, F32), (CONV_DIM, F32), (H_F * HD_F, q_dtype), None, (kv, BF16),
            None, (kv, BF16), (D_MODEL, F32), (D_MODEL, F32), (LANE, F32)]
    return pl.pallas_call(
        functools.partial(_inproj_kernel, q_scale=q_scale, kv_major=kv_major),
        grid=(m // tm,),
        in_specs=[row(D_MODEL),
                  pl.BlockSpec((1, D_MODEL), lambda i: (0, 0)),
                  _mod_spec(mod_rows, tm, tiles_per_seq, 3),
                  _mod_spec(mod_rows, tm, tiles_per_seq, 4),
                  _layer_resident(w, layer), const(g256), const(qnw), const(knw), const(sbias),
                  in_place, in_place],
        out_specs=[stacked if o is None else row(o[0]) for o in outs],
        out_shape=[jax.ShapeDtypeStruct(k_all.shape, F32) if o is None else jax.ShapeDtypeStruct((m, o[0]), o[1])
                   for o in outs],
        input_output_aliases={9: 3, 10: 5},
        scratch_shapes=[pltpu.VMEM((tm, D_MODEL), BF16)],
        compiler_params=_params("parallel"),
        name="inproj",
    )(h, nw, mod, mod, w, g256, qnw, knw, sbias, k_all, v_all)


def _fcum_kernel(sm_ref, kb_ref, tri_ref, pq_ref, pk_ref, oq_ref, ok_ref, fq_ref, kcat_ref, carry_ref):
    @pl.when(pl.program_id(1) == 0)
    def _():
        carry_ref[...] = jnp.zeros_like(carry_ref)

    cum = _dot01_l(tri_ref[...], sm_ref[...]) + carry_ref[0:1, :]
    carry_ref[0:1, :] = cum[cum.shape[0] - 1:, :]
    pieces = _split3(cum * LOG2E)

    def place(p_ref):
        return sum(jnp.dot(pieces[i], p_ref[i], preferred_element_type=F32) for i in range(3))

    fq_ref[...] = (place(pq_ref) + oq_ref[...]).astype(BF16)
    fk = (ok_ref[...] - place(pk_ref)).astype(BF16)
    for pair in range(KV_F // 2):
        kcat_ref[:, 2 * pair * LANE:(2 * pair + 1) * LANE] = kb_ref[:, pair * LANE:(pair + 1) * LANE]
        kcat_ref[:, (2 * pair + 1) * LANE:(2 * pair + 2) * LANE] = fk


def _fcum(small, kb, consts, bsz, s_len):
    tri = consts["tri256"]
    t = tri.shape[0]
    nt = s_len // t
    kvw = KV_F * HD_F
    row = lambda n: pl.BlockSpec((t, n), lambda b, i: (b * nt + i, 0))
    const = lambda a: _resident(a.shape, lambda b, i: (0,) * a.ndim)
    cs = [tri, consts["place_q"], consts["place_k"], consts["ones_q"], consts["ones_k"]]
    return pl.pallas_call(
        _fcum_kernel,
        grid=(bsz, nt),
        in_specs=[row(LANE), row(kvw)] + [const(a) for a in cs],
        out_specs=[row(LANE), row(2 * kvw)],
        out_shape=[jax.ShapeDtypeStruct((bsz * s_len, LANE), BF16),
                   jax.ShapeDtypeStruct((bsz * s_len, 2 * kvw), BF16)],
        scratch_shapes=[pltpu.VMEM((8, LANE), F32)],
        compiler_params=_params("parallel", "arbitrary"),
        name="fcum",
    )(small, kb, *cs)


def _mamba_kernel(xbc_ref, z_ref, sm_ref, cprev_ref, h0_ref, cw_ref, cb_ref, alog_ref, dsk_ref, gnw_ref,
                  tri_ref, e_ref, ya_ref, cnew_ref, ssm_ref, hist_ref, dt_ref, *, rows, valid):
    q = CHUNK
    c = pl.program_id(1)
    di = D_MODEL

    @pl.when(c == 0)
    def _():
        hist_ref[0:8, :] = cprev_ref[...]
        ssm_ref[...] = h0_ref[...]
        if rows < q:
            hist_ref[8 + rows:, :] = jnp.zeros((q - rows, CONV_DIM), F32)
            dt_ref[...] = jnp.zeros_like(dt_ref)

    hist_ref[8:8 + rows, :] = xbc_ref[...]
    conv = cb_ref[...] + hist_ref[5:5 + q, :] * cw_ref[0:1, :]
    for j in range(1, CONV_W):
        conv = conv + hist_ref[5 + j:5 + j + q, :] * cw_ref[j:j + 1, :]
    cnew_ref[...] = hist_ref[5 + valid:8 + valid, :]
    hist_ref[5:8, :] = hist_ref[5 + q:8 + q, :]
    act = conv * _sigmoid(conv)
    xs = act[:, :di]
    bmat = [act[:, di + g * N_A:di + (g + 1) * N_A].astype(BF16) for g in range(G_A)]
    cmat = [act[:, di + (G_A + g) * N_A:di + (G_A + g + 1) * N_A].astype(BF16) for g in range(G_A)]

    lane = lax.broadcasted_iota(jnp.int32, (1, LANE), 1)
    a_row = jnp.where(lane < H_A, -jnp.exp(alog_ref[...]), 0.0)
    if rows < q:
        rid = lax.broadcasted_iota(jnp.int32, (rows, LANE), 0)
        dt_ref[0:rows, :] = jnp.where(rid < valid, sm_ref[...], 0.0)
        dt = dt_ref[...]
    else:
        dt = sm_ref[...]
    tri = tri_ref[...]
    acum = _dot01_l(tri, dt * a_row)
    acum_t = acum.T
    expand = e_ref[...]
    dt_e = _dot01_r(dt, expand)
    acum_e = _dot01_r(acum, expand)
    xdt = xs * dt_e

    rq = lax.broadcasted_iota(jnp.int32, (q, q), 0)
    rs = lax.broadcasted_iota(jnp.int32, (q, q), 1)
    causal = rs <= rq
    half = lax.broadcasted_iota(jnp.int32, (q, LANE), 1) < P_A
    cb = [_dot_nt(cmat[g], bmat[g]) for g in range(G_A)]
    xdt_b = xdt.astype(BF16)
    y_parts = []
    for pair in range(H_A // 2):
        res = []
        for o in range(2):
            h = 2 * pair + o
            seg = acum[:, h:h + 1] - acum_t[h:h + 1, :]
            mix = (cb[h // (H_A // G_A)] * jnp.where(causal, jnp.exp(seg), 0.0)).astype(BF16)
            res.append(jnp.dot(mix, xdt_b[:, pair * LANE:(pair + 1) * LANE], preferred_element_type=F32))
        y_parts.append(jnp.where(half, res[0], res[1]))
    y = jnp.concatenate(y_parts, axis=1)

    hg = di // G_A
    state = ssm_ref[...]
    state_b = state.astype(BF16)
    y_off = jnp.concatenate([_dot_nt(cmat[g], state_b[g * hg:(g + 1) * hg, :]) for g in range(G_A)], axis=1)
    y = y + y_off * jnp.exp(acum_e)
    xdt_t = xdt.T
    acum_et = acum_e.T
    a_end = acum_et[:, q - 1:q]
    xdtd_t = (xdt_t * jnp.exp(a_end - acum_et)).astype(BF16)
    upd = jnp.concatenate(
        [jnp.dot(xdtd_t[g * hg:(g + 1) * hg, :], bmat[g], preferred_element_type=F32) for g in range(G_A)], axis=0)
    ssm_ref[...] = jnp.exp(a_end) * state + upd

    y = (y + dsk_ref[...] * xs)[0:rows, :]
    zz = z_ref[...]
    gated = y * (zz * _sigmoid(zz))
    out = gated * lax.rsqrt(jnp.mean(gated * gated, axis=-1, keepdims=True) + EPS) * gnw_ref[...]
    ya_ref[...] = out.astype(ya_ref.dtype)


def _mamba(xbc, z, small, cprev8, h0, layer, cw8, cb, alog, dsk, gnw, tri, expand, nseq, nchunk, rows, valid,
           ya_dtype):
    m = xbc.shape[0]
    row = lambda n: pl.BlockSpec((rows, n), lambda b, c: (b * nchunk + c, 0))
    const = lambda a: _resident(a.shape, lambda b, c: (0,) * a.ndim)
    di = D_MODEL
    kern = functools.partial(_mamba_kernel, rows=rows, valid=valid)
    return pl.pallas_call(
        kern,
        grid=(nseq, nchunk),
        in_specs=[row(CONV_DIM), row(di), row(LANE),
                  pl.BlockSpec((None, 8, CONV_DIM), lambda b, c: (b, 0, 0)),
                  pl.BlockSpec((None, None, di, N_A), lambda b, c: (layer, b, 0, 0)),
                  const(cw8), const(cb), const(alog), const(dsk), const(gnw), const(tri), const(expand)],
        out_specs=[row(di),
                   pl.BlockSpec((None, CONV_W - 1, CONV_DIM), lambda b, c: (b, 0, 0)),
                   pl.BlockSpec((None, di, N_A), lambda b, c: (b, 0, 0))],
        out_shape=[jax.ShapeDtypeStruct((m, di), ya_dtype),
                   jax.ShapeDtypeStruct((nseq, CONV_W - 1, CONV_DIM), F32),
                   jax.ShapeDtypeStruct((nseq, di, N_A), F32)],
        scratch_shapes=[pltpu.VMEM((8 + CHUNK, CONV_DIM), F32), pltpu.VMEM((CHUNK, LANE), F32)],
        compiler_params=_params("parallel", "arbitrary"),
        name="mamba",
    )(xbc, z, small, cprev8, h0, cw8, cb, alog, dsk, gnw, tri, expand)


def _attn_kernel(q_ref, fq_ref, k_ref, v_ref, o_ref, qcat_ref, m_ref, l_ref, acc_ref, *, tq, tk):
    qi = pl.program_id(1)
    ki = pl.program_id(2)
    nk = pl.num_programs(2)

    @pl.when(ki == 0)
    def _():
        m_ref[...] = jnp.full_like(m_ref, NEG)
        l_ref[...] = jnp.zeros_like(l_ref)
        acc_ref[...] = jnp.zeros_like(acc_ref)
        lane = lax.broadcasted_iota(jnp.int32, (tq, LANE), 1)
        lo = lane < HD_F
        fq = fq_ref[...]
        for blk in range(H_F // 2):
            qb = q_ref[:, blk * LANE:(blk + 1) * LANE]
            for o in range(2):
                slot = 2 * blk + o
                hq = HEAD_PERM[slot]
                own = jnp.logical_and(lane >= F_LANES * hq, lane < F_LANES * (hq + 1))
                qcat_ref[slot, :, 0:LANE] = jnp.where(lo if o == 0 else jnp.logical_not(lo), qb, jnp.zeros_like(qb))
                qcat_ref[slot, :, LANE:2 * LANE] = jnp.where(own, fq, jnp.zeros_like(fq))

    def step(diagonal):
        if diagonal:
            causal = (lax.broadcasted_iota(jnp.int32, (tk, tq), 0) <= lax.broadcasted_iota(jnp.int32, (tk, tq), 1))
        for slot in range(H_F):
            pair = slot // 4
            s = _dot_nt(k_ref[:, 2 * pair * LANE:(2 * pair + 2) * LANE], qcat_ref[slot])
            if diagonal:
                s = jnp.where(causal, s, NEG)
            m_old = m_ref[slot]
            m_new = jnp.maximum(m_old, jnp.max(s, axis=0, keepdims=True))
            alpha = jnp.exp2(m_old - m_new)
            p = jnp.exp2(s - m_new)
            l_ref[slot] = alpha * l_ref[slot] + jnp.sum(p, axis=0, keepdims=True)
            m_ref[slot] = m_new
            pv = lax.dot_general(v_ref[:, pair * LANE:(pair + 1) * LANE], p.astype(BF16),
                                 (((0,), (0,)), ((), ())), preferred_element_type=F32)
            acc_ref[slot] = alpha * acc_ref[slot] + pv

    pl.when(ki < qi)(functools.partial(step, False))
    pl.when(ki == qi)(functools.partial(step, True))

    @pl.when(ki == nk - 1)
    def _():
        lo = lax.broadcasted_iota(jnp.int32, (LANE, tq), 0) < HD_F
        for blk in range(H_F // 2):
            a0 = acc_ref[2 * blk] / l_ref[2 * blk]
            a1 = acc_ref[2 * blk + 1] / l_ref[2 * blk + 1]
            o_ref[:, blk * LANE:(blk + 1) * LANE] = jnp.where(lo, a0, a1).T.astype(o_ref.dtype)


def _attn(qb, fq, kcat, vb, bsz, s_len, tq):
    tk = tq
    nq = s_len // tq
    kern = functools.partial(_attn_kernel, tq=tq, tk=tk)
    kvw = KV_F * HD_F
    q_spec = lambda n: pl.BlockSpec((tq, n), lambda b, i, j: (b * nq + i, 0))
    k_spec = lambda n: pl.BlockSpec((tk, n), lambda b, i, j: (b * nq + jnp.minimum(i, j), 0))
    return pl.pallas_call(
        kern,
        grid=(bsz, nq, nq),
        in_specs=[q_spec(D_MODEL), q_spec(LANE), k_spec(2 * kvw), k_spec(kvw)],
        out_specs=q_spec(D_MODEL),
        out_shape=jax.ShapeDtypeStruct((bsz * s_len, D_MODEL), BF16),
        scratch_shapes=[pltpu.VMEM((H_F, tq, 2 * LANE), BF16),
                        pltpu.VMEM((H_F, 1, tq), F32), pltpu.VMEM((H_F, 1, tq), F32),
                        pltpu.VMEM((H_F, LANE, tq), F32)],
        compiler_params=_params("parallel", "parallel", "arbitrary"),
        name="attn",
    )(qb, fq, kcat, vb)


def _dec_kernel(pt_ref, q_ref, kn_ref, vn_ref, sm_ref, *rest, npp, tokens, valid):
    k_refs = rest[0:npp]
    v_refs = rest[npp:2 * npp]
    f_refs = rest[2 * npp:3 * npp]
    (et128_ref, tri_ref, slt_ref, o_ref, smp_ref, m_ref, l_ref, gq_ref, carry_ref, acc_ref) = rest[3 * npp:]
    g = pl.program_id(1)
    ng = pl.num_programs(1)
    kvw = KV_F * HD_F
    qbd = q_ref[...].astype(BF16)

    def update(scores, values_t):
        m_old = m_ref[...]
        m_new = jnp.maximum(m_old, jnp.max(functools.reduce(jnp.maximum, scores), axis=1, keepdims=True))
        alpha = jnp.exp(m_old - m_new)
        p_sum = jnp.zeros((LANE, PAGE), F32)
        pv = jnp.zeros((LANE, kvw), F32)
        for s, vt in zip(scores, values_t):
            p = jnp.exp(s - m_new)
            p_sum = p_sum + p
            pv = pv + _dot_nt(p.astype(BF16), vt.astype(BF16))
        l_ref[...] = alpha * l_ref[...] + jnp.sum(p_sum, axis=1, keepdims=True)
        m_ref[...] = m_new
        acc_ref[...] = alpha * acc_ref[...] + pv

    @pl.when(g == 0)
    def _():
        m_ref[...] = jnp.full_like(m_ref, NEG)
        l_ref[...] = jnp.zeros_like(l_ref)
        carry_ref[...] = jnp.zeros_like(carry_ref)
        acc_ref[...] = jnp.zeros_like(acc_ref)
        smp_ref[...] = jnp.zeros_like(smp_ref)
        smp_ref[0:tokens, :] = sm_ref[...]
        cum_t = _dot01_l(tri_ref[...], smp_ref[...]).T
        cum_e = _dot01_l(et128_ref[...], cum_t)
        key = lax.broadcasted_iota(jnp.int32, (LANE, PAGE), 1)
        qtok = lax.broadcasted_iota(jnp.int32, (LANE, PAGE), 0) % tokens
        g_q = jnp.sum(jnp.where(key == qtok, cum_e, 0.0), axis=1, keepdims=True)
        gq_ref[...] = g_q
        ok = jnp.logical_and(key <= qtok, key < valid)
        s = jnp.dot(qbd, kn_ref[...].astype(BF16), preferred_element_type=F32) + (g_q - cum_e)
        update([jnp.where(ok, s, NEG)], [vn_ref[...]])

    lf = jnp.concatenate([f_refs[i][...] for i in range(npp)], axis=0)
    later = _dot01_r(lf, slt_ref[...])
    mass = later[:, 0:1] + lf[:, 0:1]
    carry = carry_ref[...]
    g_q = gq_ref[...]
    scores, values_t = [], []
    for i in reversed(range(npp)):
        per_head = later[i * H_F:(i + 1) * H_F, :] + carry
        bias = jnp.concatenate(
            [jnp.broadcast_to(per_head[h:h + 1, :], (tokens, PAGE)) for h in range(H_F)], axis=0) + g_q
        kt = k_refs[i][...].reshape(kvw, PAGE).astype(BF16)
        scores.append(jnp.dot(qbd, kt, preferred_element_type=F32) + bias)
        values_t.append(v_refs[i][...].reshape(kvw, PAGE))
        carry = carry + mass[i * H_F:(i + 1) * H_F, :]
    update(scores, values_t)
    carry_ref[...] = carry

    @pl.when(g == ng - 1)
    def _():
        o_ref[...] = acc_ref[...] / l_ref[...]


def _dec_attn(page_table, qbd, kn_t, vn_t, small, cache_kt, cache_vt, cache_ft, layer, dc, tokens, valid, npp):
    nseq, n_pages = page_table.shape
    ng = n_pages // npp
    kvw = KV_F * HD_F
    rows = H_F * tokens
    kern = functools.partial(_dec_kernel, npp=npp, tokens=tokens, valid=valid)
    per_seq = lambda r, c: pl.BlockSpec((None, r, c), lambda b, g, pt: (b, 0, 0))
    page_idx = lambda b, g, pt, i: pt[b, (ng - 1 - g) * npp + i]
    kv_page = lambda i: pl.BlockSpec((None, None, KV_F, HD_F, PAGE),
                                     lambda b, g, pt: (layer, page_idx(b, g, pt, i), 0, 0, 0))
    f_page = lambda i: pl.BlockSpec((None, None, H_F, PAGE), lambda b, g, pt: (layer, page_idx(b, g, pt, i), 0, 0))
    const = lambda a: _resident(a.shape, lambda b, g, pt: (0,) * a.ndim)
    cs = [dc["et128"], dc["tri"], dc["slt"]]
    in_specs = ([per_seq(rows, kvw), per_seq(kvw, PAGE), per_seq(kvw, PAGE),
                 pl.BlockSpec((tokens, LANE), lambda b, g, pt: (b, 0))]
                + [kv_page(i) for i in range(npp)] + [kv_page(i) for i in range(npp)]
                + [f_page(i) for i in range(npp)] + [const(a) for a in cs])
    col = pltpu.VMEM((rows, 1), F32)
    grid_spec = pltpu.PrefetchScalarGridSpec(
        num_scalar_prefetch=1,
        grid=(nseq, ng),
        in_specs=in_specs,
        out_specs=per_seq(rows, kvw),
        scratch_shapes=[pltpu.VMEM((PAGE, LANE), F32), col, col, col, pltpu.VMEM((H_F, 1), F32),
                        pltpu.VMEM((rows, kvw), F32)])
    return pl.pallas_call(
        kern,
        grid_spec=grid_spec,
        out_shape=jax.ShapeDtypeStruct((nseq, rows, kvw), F32),
        compiler_params=_params("parallel", "arbitrary"),
        name="dec_attn",
    )(page_table, qbd, kn_t, vn_t, small, *([cache_kt] * npp), *([cache_vt] * npp), *([cache_ft] * npp), *cs)


def _merge_kernel(h_ref, ya_ref, yb_ref, ga_ref, gb_ref, gt_ref, wa_ref, wb_ref, wo_ref, o_ref):
    pa = jnp.dot(ya_ref[...].astype(BF16), wa_ref[...], preferred_element_type=F32)
    pb = jnp.dot(yb_ref[...].astype(BF16), wb_ref[...], preferred_element_type=F32)
    merged = _sigmoid(ga_ref[...]) * pa + _sigmoid(gb_ref[...]) * pb
    out = jnp.dot(merged.astype(BF16), wo_ref[...], preferred_element_type=F32)
    o_ref[...] = h_ref[...] + gt_ref[...] * out


def _merge(h, ya, yb, ga, gb, mod, wa, wb, wo, layer, tm, tiles_per_seq, mod_rows):
    m = h.shape[0]
    row = pl.BlockSpec((tm, D_MODEL), lambda i: (i, 0))
    return pl.pallas_call(
        _merge_kernel,
        grid=(m // tm,),
        in_specs=[row, row, row, row, row, _mod_spec(mod_rows, tm, tiles_per_seq, 5),
                  _layer_resident(wa, layer), _layer_resident(wb, layer), _layer_resident(wo, layer)],
        out_specs=row,
        out_shape=jax.ShapeDtypeStruct((m, D_MODEL), F32),
        compiler_params=_params("parallel"),
        name="merge",
    )(h, ya, yb, ga, gb, mod, wa, wb, wo)


def _constants():
    f = np.float32
    head = np.arange(D_MODEL) // HD_F
    g256 = (head[:256, None] == head[None, :256]).astype(f)
    tri128 = np.tril(np.ones((CHUNK, CHUNK), f))
    tri256 = np.tril(np.ones((256, 256), f))
    striu = np.triu(np.ones((PAGE, PAGE), f), 1)
    expand = np.zeros((LANE, D_MODEL), f)
    expand[head, np.arange(D_MODEL)] = 1.0
    place_q = np.zeros((3, LANE, LANE), f)
    place_k = np.zeros((3, LANE, LANE), f)
    ones_q = np.zeros((1, LANE), f)
    ones_k = np.zeros((1, LANE), f)
    for h in range(H_F):
        for i in range(3):
            place_q[i, H_A + h, F_LANES * h + i] = 1.0
            place_k[i, H_A + h, F_LANES * h + 3 + i] = 1.0
            ones_q[0, F_LANES * h + 3 + i] = 1.0
            ones_k[0, F_LANES * h + i] = 1.0
    out = {k: jnp.asarray(v, BF16) for k, v in
           dict(g256=g256, tri128=tri128, tri256=tri256, striu=striu, expand=expand,
                place_q=place_q, place_k=place_k).items()}
    out.update(ones_q=jnp.asarray(ones_q), ones_k=jnp.asarray(ones_k))
    return out


def _dec_constants(tokens, consts):
    f = np.float32
    et128 = np.zeros((H_F * tokens, LANE), f)
    for hq in range(H_F):
        et128[hq * tokens:(hq + 1) * tokens, H_A + hq] = 1.0
    slt = np.tril(np.ones((PAGE, PAGE), f), -1)
    return dict(et128=jnp.asarray(et128, BF16), tri=consts["tri128"], slt=jnp.asarray(slt, BF16))


def _inproj_weights(w_in):
    depth = w_in.shape[0]
    sizes = (D_MODEL, CONV_DIM, H_A, H_F * HD_F, KV_F * HD_F, KV_F * HD_F, H_F, D_MODEL, D_MODEL)
    pts = np.cumsum(sizes)[:-1].tolist()
    wz, wx, wdt, wq, wk, wv, wf, wga, wgb = jnp.split(w_in.astype(BF16), pts, axis=-1)
    wq = wq.reshape(depth, D_MODEL, H_F, HD_F)[:, :, list(HEAD_PERM), :].reshape(depth, D_MODEL, H_F * HD_F)
    pad = jnp.zeros((depth, D_MODEL, LANE - H_A - H_F), BF16)
    return jnp.concatenate([wz, wx, wq, wk, wv, wga, wgb, wdt, wf, pad], axis=-1)


def _run_group(x2d, mod, mod_rows, tiles, conv_prev, ssm_prev, lw, consts, fox, nseq, nchunk, rows, valid,
               q_dtype, q_scale, y_dtype, kv_major):
    tm_ffn, tm_in, tm_mg, tps_ffn, tps_in, tps_mg = tiles
    h = x2d
    states = []
    depth = lw["w_in"].shape[0]
    kv_shape = ((depth, nseq, KV_F * HD_F, x2d.shape[0] // nseq) if kv_major
                else (depth, x2d.shape[0], KV_F * HD_F))
    k_all = jnp.zeros(kv_shape, F32)
    v_all = jnp.zeros(kv_shape, F32)
    for l in range(depth):
        ml = mod[l]
        h = _ffn(h, ml, 0, lw["nw"][l, 0:1], lw["wi1"], lw["wo1"], l, tm_ffn, tps_ffn, mod_rows)
        z, xbc, q, k_all, kb, v_all, vb, ga, gb, small = _inproj(
            h, ml, lw["nw"][l, 1:2], lw["w_in"], l, consts["g256"], lw["qnw"][l], lw["knw"][l], lw["sbias"][l],
            k_all, v_all, tm_in, tps_in, mod_rows, q_dtype, q_scale)
        cprev8 = jnp.pad(conv_prev[l], ((0, 0), (8 - (CONV_W - 1), 0), (0, 0)))
        ya, conv_new, ssm_new = _mamba(
            xbc, z, small, cprev8, ssm_prev, l, lw["cw8"][l], lw["cb"][l], lw["alog"][l], lw["dsk"][l], lw["gnw"][l],
            consts["tri128"], consts["expand"], nseq, nchunk, rows, valid, y_dtype)
        yb = fox(l, q, k_all, kb, v_all, vb, small)
        h = _merge(h, ya, yb, ga, gb, ml, lw["wa"], lw["wb"], lw["wout"], l, tm_mg, tps_mg, mod_rows)
        h = _ffn(h, ml, 6, lw["nw"][l, 2:3], lw["wi2"], lw["wo2"], l, tm_ffn, tps_ffn, mod_rows)
        states.append((small, conv_new, ssm_new))
    return h, k_all, v_all, states


def kernel(x_prompt, x_sample, cache_k, cache_v, cache_logf, state_conv, state_ssm, page_table,
           c_prompt, c_sample, w_ada, b_ada, norm_w, w_ffn1_in, w_ffn1_out, w_ffn2_in, w_ffn2_out,
           w_in, conv_w, conv_b, dt_bias, a_log, d_skip, gn_w, q_norm_w, k_norm_w, b_f,
           w_proj_a, w_proj_b, w_out):
    bp, s_len, d = x_prompt.shape
    db, t_len, _ = x_sample.shape
    depth = w_in.shape[0]
    n_pool = cache_k.shape[1]
    kvw = KV_F * HD_F
    assert d == D_MODEL and s_len % CHUNK == 0 and t_len <= 8 and cache_k.shape[2] == PAGE
    consts = _constants()

    pad_lanes = lambda a: jnp.pad(a, ((0, 0), (0, LANE - a.shape[1])))[:, None, :]
    lw = dict(
        nw=norm_w,
        wi1=w_ffn1_in.astype(BF16), wo1=w_ffn1_out.astype(BF16),
        wi2=w_ffn2_in.astype(BF16), wo2=w_ffn2_out.astype(BF16),
        w_in=_inproj_weights(w_in),
        qnw=jnp.tile(q_norm_w, (1, 256 // HD_F))[:, None, :],
        knw=jnp.tile(k_norm_w, (1, 256 // HD_F))[:, None, :],
        sbias=pad_lanes(jnp.concatenate([dt_bias, b_f], axis=1)),
        cw8=jnp.pad(conv_w, ((0, 0), (0, 8 - CONV_W), (0, 0))),
        cb=conv_b[:, None, :],
        alog=pad_lanes(a_log),
        dsk=jnp.repeat(d_skip, P_A, axis=1)[:, None, :],
        gnw=gn_w[:, None, :],
        wa=w_proj_a.astype(BF16),
        wb=w_proj_b.reshape(depth, H_F, HD_F, d)[:, list(HEAD_PERM)].reshape(depth, d, d).astype(BF16),
        wout=w_out.astype(BF16),
    )

    mod_all = _ada(jnp.concatenate([c_prompt, c_sample], axis=0), w_ada, b_ada)
    mod_all = mod_all.reshape(depth, bp + db, N_MOD, d)
    mod_p = mod_all[:, :bp, :, None, :]
    tok = 8
    mod_s = jnp.broadcast_to(mod_all[:, bp:, None], (depth, db, tok, N_MOD, d))
    mod_s = mod_s.reshape(depth, db * tok, N_MOD, d).transpose(0, 2, 1, 3)

    tq = min(512, s_len)

    def fox_p(l, q, k, kb, v, vb, small):
        fq, kcat = _fcum(small, kb, consts, bp, s_len)
        return _attn(q, fq, kcat, vb, bp, s_len, tq)

    tm = min(512, s_len)
    tm_in = min(512, s_len)
    tiles_p = (tm, tm_in, tm, s_len // tm, s_len // tm_in, s_len // tm)
    conv0 = jnp.zeros((depth, bp, CONV_W - 1, CONV_DIM), F32)
    ssm0 = jnp.zeros((depth, bp, D_MODEL, N_A), F32)
    y_p, kp_all, vp_all, st_p = _run_group(x_prompt.reshape(bp * s_len, d), mod_p, 1, tiles_p, conv0, ssm0, lw, consts, fox_p,
                           bp, s_len // CHUNK, CHUNK, CHUNK, BF16, ATTN_SCALE * LOG2E, BF16, True)

    dc = _dec_constants(tok, consts)
    npp = next(n for n in (16, 8, 4, 2, 1) if page_table.shape[1] % n == 0)
    head_inv = list(np.argsort(HEAD_PERM))

    cache_kt = jnp.transpose(cache_k, (0, 1, 3, 4, 2))
    cache_vt = jnp.transpose(cache_v, (0, 1, 3, 4, 2))
    cache_ft = jnp.transpose(cache_logf, (0, 1, 3, 2))
    kv_of_head = np.arange(H_F) // (H_F // KV_F)
    own_kv = jnp.asarray(kv_of_head[:, None] == np.arange(KV_F)[None, :], F32)
    assert H_F * tok == LANE

    def fox_s(l, q, k, kb, v, vb, small):
        q2 = q.reshape(db, tok, H_F, HD_F)[:, :, head_inv].transpose(0, 2, 1, 3)
        qbd = (q2[:, :, :, None, :] * own_kv[None, :, None, :, None]).reshape(db, H_F * tok, kvw)
        new_t = lambda a: jnp.pad(a.reshape(db, tok, kvw).transpose(0, 2, 1), ((0, 0), (0, 0), (0, PAGE - tok)))
        out = _dec_attn(page_table, qbd, new_t(k[l]), new_t(v[l]), small, cache_kt, cache_vt, cache_ft, l, dc,
                        tok, t_len, npp)
        out = out.reshape(db, H_F, tok, KV_F, HD_F)
        pick = jnp.broadcast_to(jnp.asarray(kv_of_head)[None, :, None, None, None], (db, H_F, tok, 1, HD_F))
        out = jnp.take_along_axis(out, pick, axis=3)[:, :, :, 0]
        return out[:, list(HEAD_PERM)].transpose(0, 2, 1, 3).reshape(db * tok, d)

    ms = db * tok
    tiles_s = (ms, ms, ms, 1, 1, 1)
    x_s = jnp.pad(x_sample, ((0, 0), (0, tok - t_len), (0, 0))).reshape(ms, d)
    y_s, ks_all, vs_all, st_s = _run_group(x_s, mod_s, ms, tiles_s, state_conv, state_ssm.reshape(depth, db, D_MODEL, N_A), lw, consts,
                           fox_s, db, 1, tok, t_len, F32, ATTN_SCALE, F32, False)

    def gather(k_all, v_all, states, nseq, length, keep):
        if k_all.ndim == 4:
            k = k_all.reshape(depth, nseq, KV_F, HD_F, length).transpose(0, 1, 4, 2, 3)
            v = v_all.reshape(depth, nseq, KV_F, HD_F, length).transpose(0, 1, 4, 2, 3)
        else:
            k = k_all.reshape(depth, nseq, length, KV_F, HD_F)[:, :, :keep]
            v = v_all.reshape(depth, nseq, length, KV_F, HD_F)[:, :, :keep]
        lf = jnp.stack([s[0][:, H_A:H_A + H_F] for s in states]).reshape(depth, nseq, length, H_F)[:, :, :keep]
        conv = jnp.stack([s[1] for s in states])
        ssm = jnp.stack([s[2] for s in states]).reshape(depth, nseq, H_A, P_A, N_A)
        return k, v, lf, conv, ssm

    k_p, v_p, lf_p, conv_p, ssm_p = gather(kp_all, vp_all, st_p, bp, s_len, s_len)
    k_s, v_s, lf_s, conv_s, ssm_s = gather(ks_all, vs_all, st_s, db, tok, t_len)
    y_prompt = y_p.reshape(bp, s_len, d)
    y_sample = y_s.reshape(db, tok, d)[:, :t_len]
    return (y_prompt, y_sample, k_p, v_p, lf_p, conv_p, ssm_p, k_s, v_s, lf_s, conv_s, ssm_s)
```

```python
import functools

import numpy as np
import jax
import jax.numpy as jnp
from jax import lax
from jax.experimental import pallas as pl
from jax.experimental.pallas import tpu as pltpu

F32 = jnp.float32
BF16 = jnp.bfloat16

D_MODEL = 1024
D_FF = 2816
P_A = 64
H_A = 16
G_A = 2
N_A = 128
CONV_W = 4
CONV_DIM = D_MODEL + 2 * G_A * N_A
HD_F = 64
H_F = 16
KV_F = 8
N_MOD = 9
RESID_HALF = 0.5
EPS = 1e-6
ATTN_SCALE = HD_F ** -0.5
PAGE = 128
CHUNK = 128
NEG = -1e30
LOG2E = 1.4426950408889634
F_LANES = 6
DEC_ROWS = PAGE * KV_F
FCUM_TILE = 512

LANE = 128
FF_TILE = 256
OFF_Z = 0
OFF_X = OFF_Z + D_MODEL
OFF_Q = OFF_X + CONV_DIM
OFF_K = OFF_Q + H_F * HD_F
OFF_V = OFF_K + KV_F * HD_F
OFF_GA = OFF_V + KV_F * HD_F
OFF_GB = OFF_GA + D_MODEL
OFF_S = OFF_GB + D_MODEL
IN_W = OFF_S + LANE
HEAD_PERM = tuple(4 * j + o for j in range(4) for o in (0, 2, 1, 3))
VMEM_LIMIT = 56 * 1024 * 1024


def _params(*sem):
    return pltpu.CompilerParams(dimension_semantics=sem, vmem_limit_bytes=VMEM_LIMIT)


def _resident(shape, index_map):
    return pl.BlockSpec(shape, index_map, pipeline_mode=pl.Buffered(1))


def _sigmoid(x):
    return 1.0 / (1.0 + jnp.exp(-x))


def _norm_mod(x, nw, shift, scale):
    y = x * lax.rsqrt(jnp.mean(x * x, axis=-1, keepdims=True) + EPS) * nw
    return y * (1.0 + scale) + shift


def _split3(x):
    hi = x.astype(BF16)
    r = x - hi.astype(F32)
    mid = r.astype(BF16)
    lo = (r - mid.astype(F32)).astype(BF16)
    return hi, mid, lo


def _dot01_l(m01, x):
    hi, mid, lo = _split3(x)
    d = lambda p: jnp.dot(m01, p, preferred_element_type=F32)
    return d(hi) + d(mid) + d(lo)


def _dot01_r(x, m01):
    hi, mid, lo = _split3(x)
    d = lambda p: jnp.dot(p, m01, preferred_element_type=F32)
    return d(hi) + d(mid) + d(lo)


def _dot_nt(a, b):
    return lax.dot_general(a, b, (((1,), (1,)), ((), ())), preferred_element_type=F32)


def _ada_kernel(c_ref, w_ref, b_ref, o_ref):
    c = c_ref[...]
    a = (c * _sigmoid(c)).astype(BF16)
    o_ref[...] = jnp.dot(a, w_ref[...].astype(BF16), preferred_element_type=F32) + b_ref[...]


def _ada(c_all, w_ada, b_ada):
    depth, d, n = w_ada.shape
    rows = c_all.shape[0]
    tn = 1024
    return pl.pallas_call(
        _ada_kernel,
        grid=(depth, n // tn),
        in_specs=[pl.BlockSpec((rows, d), lambda l, j: (0, 0)),
                  pl.BlockSpec((None, d, tn), lambda l, j: (l, 0, j)),
                  pl.BlockSpec((None, 1, tn), lambda l, j: (l, 0, j))],
        out_specs=pl.BlockSpec((None, rows, tn), lambda l, j: (l, 0, j)),
        out_shape=jax.ShapeDtypeStruct((depth, rows, n), F32),
        compiler_params=_params("parallel", "parallel"),
        name="ada",
    )(c_all, w_ada, b_ada.reshape(depth, 1, n))


def _ffn_kernel(x_ref, nw_ref, sh_ref, sc_ref, gt_ref, wi_ref, wo_ref, o_ref, u_ref, acc_ref):
    x = x_ref[...]
    u_ref[...] = _norm_mod(x, nw_ref[...], sh_ref[...], sc_ref[...]).astype(BF16)
    for c in range(D_FF // FF_TILE):
        u = u_ref[...]
        lo = c * FF_TILE
        g = jnp.dot(u, wi_ref[:, lo:lo + FF_TILE], preferred_element_type=F32)
        up = jnp.dot(u, wi_ref[:, D_FF + lo:D_FF + lo + FF_TILE], preferred_element_type=F32)
        act = ((g * _sigmoid(g)) * up).astype(BF16)
        part = jnp.dot(act, wo_ref[lo:lo + FF_TILE, :], preferred_element_type=F32)
        if c == 0:
            acc_ref[...] = part
        else:
            acc_ref[...] += part
    o_ref[...] = x + (RESID_HALF * gt_ref[...]) * acc_ref[...]


def _mod_spec(mod_rows, tm, tiles_per_seq, k):
    if mod_rows == 1:
        return pl.BlockSpec((None, None, 1, D_MODEL), lambda i: (i // tiles_per_seq, k, 0, 0))
    return pl.BlockSpec((None, tm, D_MODEL), lambda i: (k, i, 0))


def _layer_resident(a, layer):
    return _resident((None,) + a.shape[1:], lambda *_: (layer,) + (0,) * (a.ndim - 1))


def _ffn(h, mod, kmod, nw, wi, wo, layer, tm, tiles_per_seq, mod_rows):
    m = h.shape[0]
    row = pl.BlockSpec((tm, D_MODEL), lambda i: (i, 0))
    return pl.pallas_call(
        _ffn_kernel,
        grid=(m // tm,),
        in_specs=[row,
                  pl.BlockSpec((1, D_MODEL), lambda i: (0, 0)),
                  _mod_spec(mod_rows, tm, tiles_per_seq, kmod),
                  _mod_spec(mod_rows, tm, tiles_per_seq, kmod + 1),
                  _mod_spec(mod_rows, tm, tiles_per_seq, kmod + 2),
                  _layer_resident(wi, layer), _layer_resident(wo, layer)],
        out_specs=row,
        out_shape=jax.ShapeDtypeStruct((m, D_MODEL), F32),
        scratch_shapes=[pltpu.VMEM((tm, D_MODEL), BF16), pltpu.VMEM((tm, D_MODEL), F32)],
        compiler_params=_params("parallel"),
        name="ffn",
    )(h, nw, mod, mod, mod, wi, wo)


def _head_norm(x, g256, w):
    sq = x * x
    hi = sq.astype(BF16)
    lo = (sq - hi.astype(F32)).astype(BF16)
    ss = jnp.dot(hi, g256, preferred_element_type=F32) + jnp.dot(lo, g256, preferred_element_type=F32)
    return x * lax.rsqrt(ss * (1.0 / HD_F) + EPS) * w


def _inproj_kernel(x_ref, nw_ref, sh_ref, sc_ref, w_ref, g_ref, qnw_ref, knw_ref, sb_ref, k_all_ref, v_all_ref,
                   z_ref, xbc_ref, q_ref, k_ref, kb_ref, v_ref, vb_ref, ga_ref, gb_ref, sm_ref, u_ref, *, q_scale,
                   kv_major):
    u_ref[...] = _norm_mod(x_ref[...], nw_ref[...], sh_ref[...], sc_ref[...]).astype(BF16)

    def mm(lo, n):
        return jnp.dot(u_ref[...], w_ref[:, lo:lo + n], preferred_element_type=F32)

    t = 256

    def plain(ref):
        def write(c, y):
            ref[:, c * t:(c + 1) * t] = y
        return write

    def write_q(c, y):
        qn = _head_norm(y, g_ref[...], qnw_ref[...])
        q_ref[:, c * t:(c + 1) * t] = (qn * q_scale).astype(q_ref.dtype)

    def write_kv(ref, ref_b, norm):
        def write(c, y):
            if norm:
                y = _head_norm(y, g_ref[...], knw_ref[...])
            ref_b[:, c * t:(c + 1) * t] = y.astype(BF16)
            if kv_major:
                ref[c * t:(c + 1) * t, :] = y.T
            else:
                ref[:, c * t:(c + 1) * t] = y
        return write

    def write_small(c, y):
        raw = y + sb_ref[...]
        tail = jnp.log1p(jnp.exp(-jnp.abs(raw)))
        lane = lax.broadcasted_iota(jnp.int32, raw.shape, 1)
        softplus = jnp.maximum(raw, 0.0) + tail
        log_sig = jnp.minimum(raw, 0.0) - tail
        sm_ref[...] = jnp.where(lane < H_A, softplus, jnp.where(lane < H_A + H_F, log_sig, 0.0))

    sections = [(OFF_Z, D_MODEL, plain(z_ref)), (OFF_GA, D_MODEL, plain(ga_ref)), (OFF_GB, D_MODEL, plain(gb_ref)),
                (OFF_Q, H_F * HD_F, write_q), (OFF_X, CONV_DIM, plain(xbc_ref)),
                (OFF_K, KV_F * HD_F, write_kv(k_ref, kb_ref, True)),
                (OFF_V, KV_F * HD_F, write_kv(v_ref, vb_ref, False))]
    work = [(off + c * t, t, c, fn) for off, width, fn in sections for c in range(width // t)]
    work.append((OFF_S, LANE, 0, write_small))
    pending = mm(work[0][0], work[0][1])
    for i, (_, _, c, fn) in enumerate(work):
        y = pending
        if i + 1 < len(work):
            pending = mm(work[i + 1][0], work[i + 1][1])
        fn(c, y)


def _inproj(h, mod, nw, w, layer, g256, qnw, knw, sbias, k_all, v_all, tm, tiles_per_seq, mod_rows, q_dtype,
            q_scale):
    m = h.shape[0]
    row = lambda n: pl.BlockSpec((tm, n), lambda i: (i, 0))
    const = lambda a: _resident(a.shape, lambda i: (0,) * a.ndim)
    kv = KV_F * HD_F
    kv_major = k_all.ndim == 4
    if kv_major:
        stacked = pl.BlockSpec((None, None, kv, tm), lambda i: (layer, i // tiles_per_seq, 0, i % tiles_per_seq))
    else:
        stacked = pl.BlockSpec((None, tm, kv), lambda i: (layer, i, 0))
    in_place = pl.BlockSpec(memory_space=pl.ANY)
    outs = [(D_MODEL, F32), (CONV_DIM, F32), (H_F * HD_F, q_dtype), None, (kv, BF16),
            None, (kv, BF16), (D_MODEL, F32), (D_MODEL, F32), (LANE, F32)]
    return pl.pallas_call(
        functools.partial(_inproj_kernel, q_scale=q_scale, kv_major=kv_major),
        grid=(m // tm,),
        in_specs=[row(D_MODEL),
                  pl.BlockSpec((1, D_MODEL), lambda i: (0, 0)),
                  _mod_spec(mod_rows, tm, tiles_per_seq, 3),
                  _mod_spec(mod_rows, tm, tiles_per_seq, 4),
                  _layer_resident(w, layer), const(g256), const(qnw), const(knw), const(sbias),
                  in_place, in_place],
        out_specs=[stacked if o is None else row(o[0]) for o in outs],
        out_shape=[jax.ShapeDtypeStruct(k_all.shape, F32) if o is None else jax.ShapeDtypeStruct((m, o[0]), o[1])
                   for o in outs],
        input_output_aliases={9: 3, 10: 5},
        scratch_shapes=[pltpu.VMEM((tm, D_MODEL), BF16)],
        compiler_params=_params("parallel"),
        name="inproj",
    )(h, nw, mod, mod, w, g256, qnw, knw, sbias, k_all, v_all)


def _fcum_kernel(sm_ref, kb_ref, tri_ref, pq_ref, pk_ref, oq_ref, ok_ref, fq_ref, kcat_ref, carry_ref):
    @pl.when(pl.program_id(1) == 0)
    def _():
        carry_ref[...] = jnp.zeros_like(carry_ref)

    cum = _dot01_l(tri_ref[...], sm_ref[...]) + carry_ref[0:1, :]
    carry_ref[0:1, :] = cum[cum.shape[0] - 1:, :]
    pieces = _split3(cum * LOG2E)

    def place(p_ref):
        return sum(jnp.dot(pieces[i], p_ref[i], preferred_element_type=F32) for i in range(3))

    fq_ref[...] = (place(pq_ref) + oq_ref[...]).astype(BF16)
    fk = (ok_ref[...] - place(pk_ref)).astype(BF16)
    for pair in range(KV_F // 2):
        kcat_ref[:, 2 * pair * LANE:(2 * pair + 1) * LANE] = kb_ref[:, pair * LANE:(pair + 1) * LANE]
        kcat_ref[:, (2 * pair + 1) * LANE:(2 * pair + 2) * LANE] = fk


def _fcum(small, kb, consts, bsz, s_len):
    tri = consts["tri_f"]
    t = tri.shape[0]
    nt = s_len // t
    kvw = KV_F * HD_F
    row = lambda n: pl.BlockSpec((t, n), lambda b, i: (b * nt + i, 0))
    const = lambda a: _resident(a.shape, lambda b, i: (0,) * a.ndim)
    cs = [tri, consts["place_q"], consts["place_k"], consts["ones_q"], consts["ones_k"]]
    return pl.pallas_call(
        _fcum_kernel,
        grid=(bsz, nt),
        in_specs=[row(LANE), row(kvw)] + [const(a) for a in cs],
        out_specs=[row(LANE), row(2 * kvw)],
        out_shape=[jax.ShapeDtypeStruct((bsz * s_len, LANE), BF16),
                   jax.ShapeDtypeStruct((bsz * s_len, 2 * kvw), BF16)],
        scratch_shapes=[pltpu.VMEM((8, LANE), F32)],
        compiler_params=_params("parallel", "arbitrary"),
        name="fcum",
    )(small, kb, *cs)


def _mamba_kernel(xbc_ref, z_ref, sm_ref, cprev_ref, h0_ref, cw_ref, cb_ref, alog_ref, dsk_ref, gnw_ref,
                  tri_ref, e_ref, ya_ref, cnew_ref, ssm_ref, hist_ref, dt_ref, *, rows, valid):
    q = CHUNK
    c = pl.program_id(1)
    di = D_MODEL

    @pl.when(c == 0)
    def _():
        hist_ref[0:8, :] = cprev_ref[...]
        ssm_ref[...] = h0_ref[...]
        if rows < q:
            hist_ref[8 + rows:, :] = jnp.zeros((q - rows, CONV_DIM), F32)
            dt_ref[...] = jnp.zeros_like(dt_ref)

    hist_ref[8:8 + rows, :] = xbc_ref[...]
    conv = cb_ref[...] + hist_ref[5:5 + q, :] * cw_ref[0:1, :]
    for j in range(1, CONV_W):
        conv = conv + hist_ref[5 + j:5 + j + q, :] * cw_ref[j:j + 1, :]
    cnew_ref[...] = hist_ref[5 + valid:8 + valid, :]
    hist_ref[5:8, :] = hist_ref[5 + q:8 + q, :]
    act = conv * _sigmoid(conv)
    xs = act[:, :di]
    bmat = [act[:, di + g * N_A:di + (g + 1) * N_A].astype(BF16) for g in range(G_A)]
    cmat = [act[:, di + (G_A + g) * N_A:di + (G_A + g + 1) * N_A].astype(BF16) for g in range(G_A)]

    lane = lax.broadcasted_iota(jnp.int32, (1, LANE), 1)
    a_row = jnp.where(lane < H_A, -jnp.exp(alog_ref[...]), 0.0)
    if rows < q:
        rid = lax.broadcasted_iota(jnp.int32, (rows, LANE), 0)
        dt_ref[0:rows, :] = jnp.where(rid < valid, sm_ref[...], 0.0)
        dt = dt_ref[...]
    else:
        dt = sm_ref[...]
    tri = tri_ref[...]
    acum = _dot01_l(tri, dt * a_row)
    acum_t = acum.T
    expand = e_ref[...]
    dt_e = _dot01_r(dt, expand)
    acum_e = _dot01_r(acum, expand)
    xdt = xs * dt_e

    rq = lax.broadcasted_iota(jnp.int32, (q, q), 0)
    rs = lax.broadcasted_iota(jnp.int32, (q, q), 1)
    causal = rs <= rq
    half = lax.broadcasted_iota(jnp.int32, (q, LANE), 1) < P_A
    cb = [_dot_nt(cmat[g], bmat[g]) for g in range(G_A)]
    xdt_b = xdt.astype(BF16)
    y_parts = []
    for pair in range(H_A // 2):
        res = []
        for o in range(2):
            h = 2 * pair + o
            seg = acum[:, h:h + 1] - acum_t[h:h + 1, :]
            mix = (cb[h // (H_A // G_A)] * jnp.where(causal, jnp.exp(seg), 0.0)).astype(BF16)
            res.append(jnp.dot(mix, xdt_b[:, pair * LANE:(pair + 1) * LANE], preferred_element_type=F32))
        y_parts.append(jnp.where(half, res[0], res[1]))
    y = jnp.concatenate(y_parts, axis=1)

    hg = di // G_A
    state = ssm_ref[...]
    state_b = state.astype(BF16)
    y_off = jnp.concatenate([_dot_nt(cmat[g], state_b[g * hg:(g + 1) * hg, :]) for g in range(G_A)], axis=1)
    y = y + y_off * jnp.exp(acum_e)
    xdt_t = xdt.T
    acum_et = acum_e.T
    a_end = acum_et[:, q - 1:q]
    xdtd_t = (xdt_t * jnp.exp(a_end - acum_et)).astype(BF16)
    upd = jnp.concatenate(
        [jnp.dot(xdtd_t[g * hg:(g + 1) * hg, :], bmat[g], preferred_element_type=F32) for g in range(G_A)], axis=0)
    ssm_ref[...] = jnp.exp(a_end) * state + upd

    y = (y + dsk_ref[...] * xs)[0:rows, :]
    zz = z_ref[...]
    gated = y * (zz * _sigmoid(zz))
    out = gated * lax.rsqrt(jnp.mean(gated * gated, axis=-1, keepdims=True) + EPS) * gnw_ref[...]
    ya_ref[...] = out.astype(ya_ref.dtype)


def _mamba(xbc, z, small, cprev8, h0, layer, cw8, cb, alog, dsk, gnw, tri, expand, nseq, nchunk, rows, valid,
           ya_dtype):
    m = xbc.shape[0]
    row = lambda n: pl.BlockSpec((rows, n), lambda b, c: (b * nchunk + c, 0))
    const = lambda a: _resident(a.shape, lambda b, c: (0,) * a.ndim)
    di = D_MODEL
    kern = functools.partial(_mamba_kernel, rows=rows, valid=valid)
    return pl.pallas_call(
        kern,
        grid=(nseq, nchunk),
        in_specs=[row(CONV_DIM), row(di), row(LANE),
                  pl.BlockSpec((None, 8, CONV_DIM), lambda b, c: (b, 0, 0)),
                  pl.BlockSpec((None, None, di, N_A), lambda b, c: (layer, b, 0, 0)),
                  const(cw8), const(cb), const(alog), const(dsk), const(gnw), const(tri), const(expand)],
        out_specs=[row(di),
                   pl.BlockSpec((None, CONV_W - 1, CONV_DIM), lambda b, c: (b, 0, 0)),
                   pl.BlockSpec((None, di, N_A), lambda b, c: (b, 0, 0))],
        out_shape=[jax.ShapeDtypeStruct((m, di), ya_dtype),
                   jax.ShapeDtypeStruct((nseq, CONV_W - 1, CONV_DIM), F32),
                   jax.ShapeDtypeStruct((nseq, di, N_A), F32)],
        scratch_shapes=[pltpu.VMEM((8 + CHUNK, CONV_DIM), F32), pltpu.VMEM((CHUNK, LANE), F32)],
        compiler_params=_params("parallel", "arbitrary"),
        name="mamba",
    )(xbc, z, small, cprev8, h0, cw8, cb, alog, dsk, gnw, tri, expand)


def _attn_kernel(q_ref, fq_ref, k_ref, v_ref, o_ref, qcat_ref, m_ref, l_ref, acc_ref, *, tq, tk):
    qi = pl.program_id(1)
    ki = pl.program_id(2)
    nk = pl.num_programs(2)

    @pl.when(ki == 0)
    def _():
        m_ref[...] = jnp.full_like(m_ref, NEG)
        l_ref[...] = jnp.zeros_like(l_ref)
        acc_ref[...] = jnp.zeros_like(acc_ref)
        lane = lax.broadcasted_iota(jnp.int32, (tq, LANE), 1)
        lo = lane < HD_F
        fq = fq_ref[...]
        for blk in range(H_F // 2):
            qb = q_ref[:, blk * LANE:(blk + 1) * LANE]
            for o in range(2):
                slot = 2 * blk + o
                hq = HEAD_PERM[slot]
                own = jnp.logical_and(lane >= F_LANES * hq, lane < F_LANES * (hq + 1))
                qcat_ref[slot, :, 0:LANE] = jnp.where(lo if o == 0 else jnp.logical_not(lo), qb, jnp.zeros_like(qb))
                qcat_ref[slot, :, LANE:2 * LANE] = jnp.where(own, fq, jnp.zeros_like(fq))

    tn = (((0,), (0,)), ((), ()))

    def full_step():
        def scores(slot):
            pair = slot // 4
            return _dot_nt(k_ref[:, 2 * pair * LANE:(2 * pair + 2) * LANE], qcat_ref[slot])

        ahead = [scores(0), scores(1)]
        for slot in range(H_F):
            pair = slot // 4
            s = ahead.pop(0)
            if slot + 2 < H_F:
                ahead.append(scores(slot + 2))
            m_old = m_ref[slot]
            m_new = jnp.maximum(m_old, jnp.max(s, axis=0, keepdims=True))
            alpha = jnp.exp2(m_old - m_new)
            p = jnp.exp2(s - m_new)
            l_ref[slot] = alpha * l_ref[slot] + jnp.sum(p, axis=0, keepdims=True)
            m_ref[slot] = m_new
            pv = lax.dot_general(v_ref[:, pair * LANE:(pair + 1) * LANE], p.astype(BF16), tn,
                                 preferred_element_type=F32)
            acc_ref[slot] = alpha * acc_ref[slot] + pv

    def diagonal_step():
        hk, hq = tk // 2, tq // 2
        causal_a = (lax.broadcasted_iota(jnp.int32, (hk, tq), 0) <= lax.broadcasted_iota(jnp.int32, (hk, tq), 1))
        causal_b = (lax.broadcasted_iota(jnp.int32, (hk, hq), 0) <= lax.broadcasted_iota(jnp.int32, (hk, hq), 1))
        def scores(slot):
            kl = slice(2 * (slot // 4) * LANE, (2 * (slot // 4) + 2) * LANE)
            return (_dot_nt(k_ref[0:hk, kl], qcat_ref[slot]),
                    _dot_nt(k_ref[hk:tk, kl], qcat_ref[slot, hq:tq, :]))

        s_next = scores(0)
        for slot in range(H_F):
            pair = slot // 4
            vl = slice(pair * LANE, (pair + 1) * LANE)
            s_a, s_b = s_next
            if slot + 1 < H_F:
                s_next = scores(slot + 1)
            s_a = jnp.where(causal_a, s_a, NEG)
            s_b = jnp.where(causal_b, s_b, NEG)
            m_old = m_ref[slot]
            m_a = jnp.max(s_a, axis=0, keepdims=True)
            m_b = jnp.max(s_b, axis=0, keepdims=True)
            m_new = jnp.maximum(m_old, jnp.concatenate([m_a[:, 0:hq], jnp.maximum(m_a[:, hq:tq], m_b)], axis=1))
            alpha = jnp.exp2(m_old - m_new)
            p_a = jnp.exp2(s_a - m_new)
            p_b = jnp.exp2(s_b - m_new[:, hq:tq])
            l_add = jnp.sum(p_a, axis=0, keepdims=True)
            l_b = jnp.sum(p_b, axis=0, keepdims=True)
            l_ref[slot] = alpha * l_ref[slot] + jnp.concatenate([l_add[:, 0:hq], l_add[:, hq:tq] + l_b], axis=1)
            m_ref[slot] = m_new
            pv_a = lax.dot_general(v_ref[0:hk, vl], p_a.astype(BF16), tn, preferred_element_type=F32)
            pv_b = lax.dot_general(v_ref[hk:tk, vl], p_b.astype(BF16), tn, preferred_element_type=F32)
            acc_ref[slot] = alpha * acc_ref[slot] + jnp.concatenate(
                [pv_a[:, 0:hq], pv_a[:, hq:tq] + pv_b], axis=1)

    pl.when(ki < qi)(full_step)
    pl.when(ki == qi)(diagonal_step)

    @pl.when(ki == nk - 1)
    def _():
        lo = lax.broadcasted_iota(jnp.int32, (LANE, tq), 0) < HD_F
        for blk in range(H_F // 2):
            a0 = acc_ref[2 * blk] / l_ref[2 * blk]
            a1 = acc_ref[2 * blk + 1] / l_ref[2 * blk + 1]
            o_ref[:, blk * LANE:(blk + 1) * LANE] = jnp.where(lo, a0, a1).T.astype(o_ref.dtype)


def _attn(qb, fq, kcat, vb, bsz, s_len, tq):
    tk = tq
    nq = s_len // tq
    kern = functools.partial(_attn_kernel, tq=tq, tk=tk)
    kvw = KV_F * HD_F
    q_spec = lambda n: pl.BlockSpec((tq, n), lambda b, i, j: (b * nq + i, 0))
    k_spec = lambda n: pl.BlockSpec((tk, n), lambda b, i, j: (b * nq + jnp.minimum(i, j), 0))
    return pl.pallas_call(
        kern,
        grid=(bsz, nq, nq),
        in_specs=[q_spec(D_MODEL), q_spec(LANE), k_spec(2 * kvw), k_spec(kvw)],
        out_specs=q_spec(D_MODEL),
        out_shape=jax.ShapeDtypeStruct((bsz * s_len, D_MODEL), BF16),
        scratch_shapes=[pltpu.VMEM((H_F, tq, 2 * LANE), BF16),
                        pltpu.VMEM((H_F, 1, tq), F32), pltpu.VMEM((H_F, 1, tq), F32),
                        pltpu.VMEM((H_F, LANE, tq), F32)],
        compiler_params=_params("parallel", "parallel", "arbitrary"),
        name="attn",
    )(qb, fq, kcat, vb)


def _dec_kernel(pt_ref, q_ref, kn_ref, vn_ref, sm_ref, *rest, npp, tokens, valid):
    k_refs = rest[0:npp]
    v_refs = rest[npp:2 * npp]
    f_refs = rest[2 * npp:3 * npp]
    (et128_ref, tri_ref, slt_ref, o_ref, smp_ref, m_ref, l_ref, gq_ref, carry_ref, acc_ref) = rest[3 * npp:]
    g = pl.program_id(1)
    ng = pl.num_programs(1)
    kvw = KV_F * HD_F
    qbd = q_ref[...].astype(BF16)

    def update(scores, values_t):
        m_old = m_ref[...]
        m_new = jnp.maximum(m_old, jnp.max(functools.reduce(jnp.maximum, scores), axis=1, keepdims=True))
        alpha = jnp.exp(m_old - m_new)
        p_sum = jnp.zeros((LANE, PAGE), F32)
        pv = jnp.zeros((LANE, kvw), F32)
        for s, vt in zip(scores, values_t):
            p = jnp.exp(s - m_new)
            p_sum = p_sum + p
            pv = pv + _dot_nt(p.astype(BF16), vt.astype(BF16))
        l_ref[...] = alpha * l_ref[...] + jnp.sum(p_sum, axis=1, keepdims=True)
        m_ref[...] = m_new
        acc_ref[...] = alpha * acc_ref[...] + pv

    @pl.when(g == 0)
    def _():
        m_ref[...] = jnp.full_like(m_ref, NEG)
        l_ref[...] = jnp.zeros_like(l_ref)
        carry_ref[...] = jnp.zeros_like(carry_ref)
        acc_ref[...] = jnp.zeros_like(acc_ref)
        smp_ref[...] = jnp.zeros_like(smp_ref)
        smp_ref[0:tokens, :] = sm_ref[...]
        cum_t = _dot01_l(tri_ref[...], smp_ref[...]).T
        cum_e = _dot01_l(et128_ref[...], cum_t)
        key = lax.broadcasted_iota(jnp.int32, (LANE, PAGE), 1)
        qtok = lax.broadcasted_iota(jnp.int32, (LANE, PAGE), 0) % tokens
        g_q = jnp.sum(jnp.where(key == qtok, cum_e, 0.0), axis=1, keepdims=True)
        gq_ref[...] = g_q
        ok = jnp.logical_and(key <= qtok, key < valid)
        s = jnp.dot(qbd, kn_ref[...].astype(BF16), preferred_element_type=F32) + (g_q - cum_e)
        update([jnp.where(ok, s, NEG)], [vn_ref[...]])

    lf = jnp.concatenate([f_refs[i][...] for i in range(npp)], axis=0)
    later = _dot01_r(lf, slt_ref[...])
    mass = later[:, 0:1] + lf[:, 0:1]
    carry = carry_ref[...]
    g_q = gq_ref[...]
    scores, values_t = [], []
    for i in reversed(range(npp)):
        per_head = later[i * H_F:(i + 1) * H_F, :] + carry
        bias = jnp.concatenate(
            [jnp.broadcast_to(per_head[h:h + 1, :], (tokens, PAGE)) for h in range(H_F)], axis=0) + g_q
        kt = k_refs[i][...].reshape(kvw, PAGE).astype(BF16)
        scores.append(jnp.dot(qbd, kt, preferred_element_type=F32) + bias)
        values_t.append(v_refs[i][...].reshape(kvw, PAGE))
        carry = carry + mass[i * H_F:(i + 1) * H_F, :]
    update(scores, values_t)
    carry_ref[...] = carry

    @pl.when(g == ng - 1)
    def _():
        o_ref[...] = acc_ref[...] / l_ref[...]


def _dec_attn(page_table, qbd, kn_t, vn_t, small, cache_kt, cache_vt, cache_ft, layer, dc, tokens, valid, npp):
    nseq, n_pages = page_table.shape
    ng = n_pages // npp
    kvw = KV_F * HD_F
    rows = H_F * tokens
    kern = functools.partial(_dec_kernel, npp=npp, tokens=tokens, valid=valid)
    per_seq = lambda r, c: pl.BlockSpec((None, r, c), lambda b, g, pt: (b, 0, 0))
    page_idx = lambda b, g, pt, i: pt[b, (ng - 1 - g) * npp + i]
    kv_page = lambda i: pl.BlockSpec((None, None, KV_F, HD_F, PAGE),
                                     lambda b, g, pt: (layer, page_idx(b, g, pt, i), 0, 0, 0))
    f_page = lambda i: pl.BlockSpec((None, None, H_F, PAGE), lambda b, g, pt: (layer, page_idx(b, g, pt, i), 0, 0))
    const = lambda a: _resident(a.shape, lambda b, g, pt: (0,) * a.ndim)
    cs = [dc["et128"], dc["tri"], dc["slt"]]
    in_specs = ([per_seq(rows, kvw), per_seq(kvw, PAGE), per_seq(kvw, PAGE),
                 pl.BlockSpec((tokens, LANE), lambda b, g, pt: (b, 0))]
                + [kv_page(i) for i in range(npp)] + [kv_page(i) for i in range(npp)]
                + [f_page(i) for i in range(npp)] + [const(a) for a in cs])
    col = pltpu.VMEM((rows, 1), F32)
    grid_spec = pltpu.PrefetchScalarGridSpec(
        num_scalar_prefetch=1,
        grid=(nseq, ng),
        in_specs=in_specs,
        out_specs=per_seq(rows, kvw),
        scratch_shapes=[pltpu.VMEM((PAGE, LANE), F32), col, col, col, pltpu.VMEM((H_F, 1), F32),
                        pltpu.VMEM((rows, kvw), F32)])
    return pl.pallas_call(
        kern,
        grid_spec=grid_spec,
        out_shape=jax.ShapeDtypeStruct((nseq, rows, kvw), F32),
        compiler_params=_params("parallel", "arbitrary"),
        name="dec_attn",
    )(page_table, qbd, kn_t, vn_t, small, *([cache_kt] * npp), *([cache_vt] * npp), *([cache_ft] * npp), *cs)


def _merge_kernel(h_ref, ya_ref, yb_ref, ga_ref, gb_ref, gt_ref, wa_ref, wb_ref, wo_ref, o_ref):
    tm = h_ref.shape[0]
    halves = [slice(0, tm // 2), slice(tm // 2, tm)] if tm % 16 == 0 else [slice(0, tm)]
    proj = [(jnp.dot(ya_ref[r, :].astype(BF16), wa_ref[...], preferred_element_type=F32),
             jnp.dot(yb_ref[r, :].astype(BF16), wb_ref[...], preferred_element_type=F32)) for r in halves]
    for r, (pa, pb) in zip(halves, proj):
        merged = _sigmoid(ga_ref[r, :]) * pa + _sigmoid(gb_ref[r, :]) * pb
        out = jnp.dot(merged.astype(BF16), wo_ref[...], preferred_element_type=F32)
        gate = gt_ref[...] if gt_ref.shape[0] == 1 else gt_ref[r, :]
        o_ref[r, :] = h_ref[r, :] + gate * out


def _merge(h, ya, yb, ga, gb, mod, wa, wb, wo, layer, tm, tiles_per_seq, mod_rows):
    m = h.shape[0]
    row = pl.BlockSpec((tm, D_MODEL), lambda i: (i, 0))
    return pl.pallas_call(
        _merge_kernel,
        grid=(m // tm,),
        in_specs=[row, row, row, row, row, _mod_spec(mod_rows, tm, tiles_per_seq, 5),
                  _layer_resident(wa, layer), _layer_resident(wb, layer), _layer_resident(wo, layer)],
        out_specs=row,
        out_shape=jax.ShapeDtypeStruct((m, D_MODEL), F32),
        compiler_params=_params("parallel"),
        name="merge",
    )(h, ya, yb, ga, gb, mod, wa, wb, wo)


def _constants():
    f = np.float32
    head = np.arange(D_MODEL) // HD_F
    g256 = (head[:256, None] == head[None, :256]).astype(f)
    tri128 = np.tril(np.ones((CHUNK, CHUNK), f))
    tri_f = np.tril(np.ones((FCUM_TILE, FCUM_TILE), f))
    striu = np.triu(np.ones((PAGE, PAGE), f), 1)
    expand = np.zeros((LANE, D_MODEL), f)
    expand[head, np.arange(D_MODEL)] = 1.0
    place_q = np.zeros((3, LANE, LANE), f)
    place_k = np.zeros((3, LANE, LANE), f)
    ones_q = np.zeros((1, LANE), f)
    ones_k = np.zeros((1, LANE), f)
    for h in range(H_F):
        for i in range(3):
            place_q[i, H_A + h, F_LANES * h + i] = 1.0
            place_k[i, H_A + h, F_LANES * h + 3 + i] = 1.0
            ones_q[0, F_LANES * h + 3 + i] = 1.0
            ones_k[0, F_LANES * h + i] = 1.0
    out = {k: jnp.asarray(v, BF16) for k, v in
           dict(g256=g256, tri128=tri128, tri_f=tri_f, striu=striu, expand=expand,
                place_q=place_q, place_k=place_k).items()}
    out.update(ones_q=jnp.asarray(ones_q), ones_k=jnp.asarray(ones_k))
    return out


def _dec_constants(tokens, consts):
    f = np.float32
    et128 = np.zeros((H_F * tokens, LANE), f)
    for hq in range(H_F):
        et128[hq * tokens:(hq + 1) * tokens, H_A + hq] = 1.0
    slt = np.tril(np.ones((PAGE, PAGE), f), -1)
    return dict(et128=jnp.asarray(et128, BF16), tri=consts["tri128"], slt=jnp.asarray(slt, BF16))


def _inproj_weights(w_in):
    depth = w_in.shape[0]
    sizes = (D_MODEL, CONV_DIM, H_A, H_F * HD_F, KV_F * HD_F, KV_F * HD_F, H_F, D_MODEL, D_MODEL)
    pts = np.cumsum(sizes)[:-1].tolist()
    wz, wx, wdt, wq, wk, wv, wf, wga, wgb = jnp.split(w_in.astype(BF16), pts, axis=-1)
    wq = wq.reshape(depth, D_MODEL, H_F, HD_F)[:, :, list(HEAD_PERM), :].reshape(depth, D_MODEL, H_F * HD_F)
    pad = jnp.zeros((depth, D_MODEL, LANE - H_A - H_F), BF16)
    return jnp.concatenate([wz, wx, wq, wk, wv, wga, wgb, wdt, wf, pad], axis=-1)


def _run_group(x2d, mod, mod_rows, tiles, conv_prev, ssm_prev, lw, consts, fox, nseq, nchunk, rows, valid,
               q_dtype, q_scale, y_dtype, kv_major):
    tm_ffn, tm_in, tm_mg, tps_ffn, tps_in, tps_mg = tiles
    h = x2d
    states = []
    depth = lw["w_in"].shape[0]
    kv_shape = ((depth, nseq, KV_F * HD_F, x2d.shape[0] // nseq) if kv_major
                else (depth, x2d.shape[0], KV_F * HD_F))
    k_all = jnp.zeros(kv_shape, F32)
    v_all = jnp.zeros(kv_shape, F32)
    for l in range(depth):
        ml = mod[l]
        h = _ffn(h, ml, 0, lw["nw"][l, 0:1], lw["wi1"], lw["wo1"], l, tm_ffn, tps_ffn, mod_rows)
        z, xbc, q, k_all, kb, v_all, vb, ga, gb, small = _inproj(
            h, ml, lw["nw"][l, 1:2], lw["w_in"], l, consts["g256"], lw["qnw"][l], lw["knw"][l], lw["sbias"][l],
            k_all, v_all, tm_in, tps_in, mod_rows, q_dtype, q_scale)
        cprev8 = jnp.pad(conv_prev[l], ((0, 0), (8 - (CONV_W - 1), 0), (0, 0)))
        ya, conv_new, ssm_new = _mamba(
            xbc, z, small, cprev8, ssm_prev, l, lw["cw8"][l], lw["cb"][l], lw["alog"][l], lw["dsk"][l], lw["gnw"][l],
            consts["tri128"], consts["expand"], nseq, nchunk, rows, valid, y_dtype)
        yb = fox(l, q, k_all, kb, v_all, vb, small)
        h = _merge(h, ya, yb, ga, gb, ml, lw["wa"], lw["wb"], lw["wout"], l, tm_mg, tps_mg, mod_rows)
        h = _ffn(h, ml, 6, lw["nw"][l, 2:3], lw["wi2"], lw["wo2"], l, tm_ffn, tps_ffn, mod_rows)
        states.append((small, conv_new, ssm_new))
    return h, k_all, v_all, states


def kernel(x_prompt, x_sample, cache_k, cache_v, cache_logf, state_conv, state_ssm, page_table,
           c_prompt, c_sample, w_ada, b_ada, norm_w, w_ffn1_in, w_ffn1_out, w_ffn2_in, w_ffn2_out,
           w_in, conv_w, conv_b, dt_bias, a_log, d_skip, gn_w, q_norm_w, k_norm_w, b_f,
           w_proj_a, w_proj_b, w_out):
    bp, s_len, d = x_prompt.shape
    db, t_len, _ = x_sample.shape
    depth = w_in.shape[0]
    n_pool = cache_k.shape[1]
    kvw = KV_F * HD_F
    assert d == D_MODEL and s_len % CHUNK == 0 and t_len <= 8 and cache_k.shape[2] == PAGE
    consts = _constants()

    pad_lanes = lambda a: jnp.pad(a, ((0, 0), (0, LANE - a.shape[1])))[:, None, :]
    lw = dict(
        nw=norm_w,
        wi1=w_ffn1_in.astype(BF16), wo1=w_ffn1_out.astype(BF16),
        wi2=w_ffn2_in.astype(BF16), wo2=w_ffn2_out.astype(BF16),
        w_in=_inproj_weights(w_in),
        qnw=jnp.tile(q_norm_w, (1, 256 // HD_F))[:, None, :],
        knw=jnp.tile(k_norm_w, (1, 256 // HD_F))[:, None, :],
        sbias=pad_lanes(jnp.concatenate([dt_bias, b_f], axis=1)),
        cw8=jnp.pad(conv_w, ((0, 0), (0, 8 - CONV_W), (0, 0))),
        cb=conv_b[:, None, :],
        alog=pad_lanes(a_log),
        dsk=jnp.repeat(d_skip, P_A, axis=1)[:, None, :],
        gnw=gn_w[:, None, :],
        wa=w_proj_a.astype(BF16),
        wb=w_proj_b.reshape(depth, H_F, HD_F, d)[:, list(HEAD_PERM)].reshape(depth, d, d).astype(BF16),
        wout=w_out.astype(BF16),
    )

    mod_all = _ada(jnp.concatenate([c_prompt, c_sample], axis=0), w_ada, b_ada)
    mod_all = mod_all.reshape(depth, bp + db, N_MOD, d)
    mod_p = mod_all[:, :bp, :, None, :]
    tok = 8
    mod_s = jnp.broadcast_to(mod_all[:, bp:, None], (depth, db, tok, N_MOD, d))
    mod_s = mod_s.reshape(depth, db * tok, N_MOD, d).transpose(0, 2, 1, 3)

    tq = min(512, s_len)

    def fox_p(l, q, k, kb, v, vb, small):
        fq, kcat = _fcum(small, kb, consts, bp, s_len)
        return _attn(q, fq, kcat, vb, bp, s_len, tq)

    tm = min(512, s_len)
    tm_in = min(512, s_len)
    tiles_p = (tm, tm_in, tm, s_len // tm, s_len // tm_in, s_len // tm)
    conv0 = jnp.zeros((depth, bp, CONV_W - 1, CONV_DIM), F32)
    ssm0 = jnp.zeros((depth, bp, D_MODEL, N_A), F32)
    y_p, kp_all, vp_all, st_p = _run_group(x_prompt.reshape(bp * s_len, d), mod_p, 1, tiles_p, conv0, ssm0, lw, consts, fox_p,
                           bp, s_len // CHUNK, CHUNK, CHUNK, BF16, ATTN_SCALE * LOG2E, BF16, True)

    dc = _dec_constants(tok, consts)
    npp = next(n for n in (16, 8, 4, 2, 1) if page_table.shape[1] % n == 0)
    head_inv = list(np.argsort(HEAD_PERM))

    cache_kt = jnp.transpose(cache_k, (0, 1, 3, 4, 2))
    cache_vt = jnp.transpose(cache_v, (0, 1, 3, 4, 2))
    cache_ft = jnp.transpose(cache_logf, (0, 1, 3, 2))
    kv_of_head = np.arange(H_F) // (H_F // KV_F)
    own_kv = jnp.asarray(kv_of_head[:, None] == np.arange(KV_F)[None, :], F32)
    assert H_F * tok == LANE

    def fox_s(l, q, k, kb, v, vb, small):
        q2 = q.reshape(db, tok, H_F, HD_F)[:, :, head_inv].transpose(0, 2, 1, 3)
        qbd = (q2[:, :, :, None, :] * own_kv[None, :, None, :, None]).reshape(db, H_F * tok, kvw)
        new_t = lambda a: jnp.pad(a.reshape(db, tok, kvw).transpose(0, 2, 1), ((0, 0), (0, 0), (0, PAGE - tok)))
        out = _dec_attn(page_table, qbd, new_t(k[l]), new_t(v[l]), small, cache_kt, cache_vt, cache_ft, l, dc,
                        tok, t_len, npp)
        out = out.reshape(db, H_F, tok, KV_F, HD_F)
        pick = jnp.broadcast_to(jnp.asarray(kv_of_head)[None, :, None, None, None], (db, H_F, tok, 1, HD_F))
        out = jnp.take_along_axis(out, pick, axis=3)[:, :, :, 0]
        return out[:, list(HEAD_PERM)].transpose(0, 2, 1, 3).reshape(db * tok, d)

    ms = db * tok
    tiles_s = (ms, ms, ms, 1, 1, 1)
    x_s = jnp.pad(x_sample, ((0, 0), (0, tok - t_len), (0, 0))).reshape(ms, d)
    y_s, ks_all, vs_all, st_s = _run_group(x_s, mod_s, ms, tiles_s, state_conv, state_ssm.reshape(depth, db, D_MODEL, N_A), lw, consts,
                           fox_s, db, 1, tok, t_len, F32, ATTN_SCALE, F32, False)

    def gather(k_all, v_all, states, nseq, length, keep):
        if k_all.ndim == 4:
            k = k_all.reshape(depth, nseq, KV_F, HD_F, length).transpose(0, 1, 4, 2, 3)
            v = v_all.reshape(depth, nseq, KV_F, HD_F, length).transpose(0, 1, 4, 2, 3)
        else:
            k = k_all.reshape(depth, nseq, length, KV_F, HD_F)[:, :, :keep]
            v = v_all.reshape(depth, nseq, length, KV_F, HD_F)[:, :, :keep]
        lf = jnp.stack([s[0][:, H_A:H_A + H_F] for s in states]).reshape(depth, nseq, length, H_F)[:, :, :keep]
        conv = jnp.stack([s[1] for s in states])
        ssm = jnp.stack([s[2] for s in states]).reshape(depth, nseq, H_A, P_A, N_A)
        return k, v, lf, conv, ssm

    k_p, v_p, lf_p, conv_p, ssm_p = gather(kp_all, vp_all, st_p, bp, s_len, s_len)
    k_s, v_s, lf_s, conv_s, ssm_s = gather(ks_all, vs_all, st_s, db, tok, t_len)
    y_prompt = y_p.reshape(bp, s_len, d)
    y_sample = y_s.reshape(db, tok, d)[:, :t_len]
    return (y_prompt, y_sample, k_p, v_p, lf_p, conv_p, ssm_p, k_s, v_s, lf_s, conv_s, ssm_s)
```

```python
import functools

import numpy as np
import jax
import jax.numpy as jnp
from jax import lax
from jax.experimental import pallas as pl
from jax.experimental.pallas import tpu as pltpu

F32 = jnp.float32
BF16 = jnp.bfloat16

D_MODEL = 1024
D_FF = 2816
P_A = 64
H_A = 16
G_A = 2
N_A = 128
CONV_W = 4
CONV_DIM = D_MODEL + 2 * G_A * N_A
HD_F = 64
H_F = 16
KV_F = 8
N_MOD = 9
RESID_HALF = 0.5
EPS = 1e-6
ATTN_SCALE = HD_F ** -0.5
PAGE = 128
CHUNK = 128
NEG = -1e30
LOG2E = 1.4426950408889634
F_LANES = 6
DEC_ROWS = PAGE * KV_F
FCUM_TILE = 512

LANE = 128
FF_TILE = 256
OFF_Z = 0
OFF_X = OFF_Z + D_MODEL
OFF_Q = OFF_X + CONV_DIM
OFF_K = OFF_Q + H_F * HD_F
OFF_V = OFF_K + KV_F * HD_F
OFF_GA = OFF_V + KV_F * HD_F
OFF_GB = OFF_GA + D_MODEL
OFF_S = OFF_GB + D_MODEL
IN_W = OFF_S + LANE
HEAD_PERM = tuple(4 * j + o for j in range(4) for o in (0, 2, 1, 3))
VMEM_LIMIT = 56 * 1024 * 1024


def _params(*sem):
    return pltpu.CompilerParams(dimension_semantics=sem, vmem_limit_bytes=VMEM_LIMIT)


def _resident(shape, index_map):
    return pl.BlockSpec(shape, index_map, pipeline_mode=pl.Buffered(1))


def _sigmoid(x):
    return 1.0 / (1.0 + jnp.exp(-x))


def _norm_mod(x, nw, shift, scale):
    y = x * lax.rsqrt(jnp.mean(x * x, axis=-1, keepdims=True) + EPS) * nw
    return y * (1.0 + scale) + shift


def _split3(x):
    hi = x.astype(BF16)
    r = x - hi.astype(F32)
    mid = r.astype(BF16)
    lo = (r - mid.astype(F32)).astype(BF16)
    return hi, mid, lo


def _dot01_l(m01, x):
    hi, mid, lo = _split3(x)
    d = lambda p: jnp.dot(m01, p, preferred_element_type=F32)
    return d(hi) + d(mid) + d(lo)


def _dot01_r(x, m01):
    hi, mid, lo = _split3(x)
    d = lambda p: jnp.dot(p, m01, preferred_element_type=F32)
    return d(hi) + d(mid) + d(lo)


def _dot_nt(a, b):
    return lax.dot_general(a, b, (((1,), (1,)), ((), ())), preferred_element_type=F32)


def _ada_kernel(c_ref, w_ref, b_ref, o_ref):
    c = c_ref[...]
    a = (c * _sigmoid(c)).astype(BF16)
    o_ref[...] = jnp.dot(a, w_ref[...].astype(BF16), preferred_element_type=F32) + b_ref[...]


def _ada(c_all, w_ada, b_ada):
    depth, d, n = w_ada.shape
    rows = c_all.shape[0]
    tn = 1024
    return pl.pallas_call(
        _ada_kernel,
        grid=(depth, n // tn),
        in_specs=[pl.BlockSpec((rows, d), lambda l, j: (0, 0)),
                  pl.BlockSpec((None, d, tn), lambda l, j: (l, 0, j)),
                  pl.BlockSpec((None, 1, tn), lambda l, j: (l, 0, j))],
        out_specs=pl.BlockSpec((None, rows, tn), lambda l, j: (l, 0, j)),
        out_shape=jax.ShapeDtypeStruct((depth, rows, n), F32),
        compiler_params=_params("parallel", "parallel"),
        name="ada",
    )(c_all, w_ada, b_ada.reshape(depth, 1, n))


def _ffn_kernel(x_ref, nw_ref, sh_ref, sc_ref, gt_ref, wi_ref, wo_ref, o_ref, u_ref, acc_ref):
    x = x_ref[...]
    u_ref[...] = _norm_mod(x, nw_ref[...], sh_ref[...], sc_ref[...]).astype(BF16)
    for c in range(D_FF // FF_TILE):
        u = u_ref[...]
        lo = c * FF_TILE
        g = jnp.dot(u, wi_ref[:, lo:lo + FF_TILE], preferred_element_type=F32)
        up = jnp.dot(u, wi_ref[:, D_FF + lo:D_FF + lo + FF_TILE], preferred_element_type=F32)
        act = ((g * _sigmoid(g)) * up).astype(BF16)
        part = jnp.dot(act, wo_ref[lo:lo + FF_TILE, :], preferred_element_type=F32)
        if c == 0:
            acc_ref[...] = part
        else:
            acc_ref[...] += part
    o_ref[...] = x + (RESID_HALF * gt_ref[...]) * acc_ref[...]


def _mod_spec(mod_rows, tm, tiles_per_seq, k):
    if mod_rows == 1:
        return pl.BlockSpec((None, None, 1, D_MODEL), lambda i: (i // tiles_per_seq, k, 0, 0))
    return pl.BlockSpec((None, tm, D_MODEL), lambda i: (k, i, 0))


def _layer_resident(a, layer):
    return _resident((None,) + a.shape[1:], lambda *_: (layer,) + (0,) * (a.ndim - 1))


def _ffn(h, mod, kmod, nw, wi, wo, layer, tm, tiles_per_seq, mod_rows):
    m = h.shape[0]
    row = pl.BlockSpec((tm, D_MODEL), lambda i: (i, 0))
    return pl.pallas_call(
        _ffn_kernel,
        grid=(m // tm,),
        in_specs=[row,
                  pl.BlockSpec((1, D_MODEL), lambda i: (0, 0)),
                  _mod_spec(mod_rows, tm, tiles_per_seq, kmod),
                  _mod_spec(mod_rows, tm, tiles_per_seq, kmod + 1),
                  _mod_spec(mod_rows, tm, tiles_per_seq, kmod + 2),
                  _layer_resident(wi, layer), _layer_resident(wo, layer)],
        out_specs=row,
        out_shape=jax.ShapeDtypeStruct((m, D_MODEL), F32),
        scratch_shapes=[pltpu.VMEM((tm, D_MODEL), BF16), pltpu.VMEM((tm, D_MODEL), F32)],
        compiler_params=_params("parallel"),
        name="ffn",
    )(h, nw, mod, mod, mod, wi, wo)


def _head_norm(x, g256, w):
    sq = x * x
    hi = sq.astype(BF16)
    lo = (sq - hi.astype(F32)).astype(BF16)
    ss = jnp.dot(hi, g256, preferred_element_type=F32) + jnp.dot(lo, g256, preferred_element_type=F32)
    return x * lax.rsqrt(ss * (1.0 / HD_F) + EPS) * w


def _inproj_kernel(x_ref, nw_ref, sh_ref, sc_ref, w_ref, g_ref, qnw_ref, knw_ref, sb_ref, k_all_ref, v_all_ref,
                   z_ref, xbc_ref, q_ref, k_ref, kb_ref, v_ref, vb_ref, ga_ref, gb_ref, sm_ref, u_ref, *, q_scale,
                   kv_major):
    u_ref[...] = _norm_mod(x_ref[...], nw_ref[...], sh_ref[...], sc_ref[...]).astype(BF16)

    def mm(lo, n):
        return jnp.dot(u_ref[...], w_ref[:, lo:lo + n], preferred_element_type=F32)

    t = 256

    def plain(ref):
        def write(c, y):
            ref[:, c * t:(c + 1) * t] = y
        return write

    def write_q(c, y):
        qn = _head_norm(y, g_ref[...], qnw_ref[...])
        q_ref[:, c * t:(c + 1) * t] = (qn * q_scale).astype(q_ref.dtype)

    def write_kv(ref, ref_b, norm):
        def write(c, y):
            if norm:
                y = _head_norm(y, g_ref[...], knw_ref[...])
            ref_b[:, c * t:(c + 1) * t] = y.astype(BF16)
            if kv_major:
                ref[c * t:(c + 1) * t, :] = y.T
            else:
                ref[:, c * t:(c + 1) * t] = y
        return write

    def write_small(c, y):
        raw = y + sb_ref[...]
        tail = jnp.log1p(jnp.exp(-jnp.abs(raw)))
        lane = lax.broadcasted_iota(jnp.int32, raw.shape, 1)
        softplus = jnp.maximum(raw, 0.0) + tail
        log_sig = jnp.minimum(raw, 0.0) - tail
        sm_ref[...] = jnp.where(lane < H_A, softplus, jnp.where(lane < H_A + H_F, log_sig, 0.0))

    sections = [(OFF_Z, D_MODEL, plain(z_ref)), (OFF_GA, D_MODEL, plain(ga_ref)), (OFF_GB, D_MODEL, plain(gb_ref)),
                (OFF_Q, H_F * HD_F, write_q), (OFF_X, CONV_DIM, plain(xbc_ref)),
                (OFF_K, KV_F * HD_F, write_kv(k_ref, kb_ref, True)),
                (OFF_V, KV_F * HD_F, write_kv(v_ref, vb_ref, False))]
    work = [(off + c * t, t, c, fn) for off, width, fn in sections for c in range(width // t)]
    work.append((OFF_S, LANE, 0, write_small))
    depth_ahead = 1
    pending = [mm(off, n) for off, n, _, _ in work[:depth_ahead]]
    for i, (_, _, c, fn) in enumerate(work):
        y = pending.pop(0)
        if i + depth_ahead < len(work):
            pending.append(mm(work[i + depth_ahead][0], work[i + depth_ahead][1]))
        fn(c, y)


def _inproj(h, mod, nw, w, layer, g256, qnw, knw, sbias, k_all, v_all, tm, tiles_per_seq, mod_rows, q_dtype,
            q_scale):
    m = h.shape[0]
    row = lambda n: pl.BlockSpec((tm, n), lambda i: (i, 0))
    const = lambda a: _resident(a.shape, lambda i: (0,) * a.ndim)
    kv = KV_F * HD_F
    kv_major = k_all.ndim == 4
    if kv_major:
        stacked = pl.BlockSpec((None, None, kv, tm), lambda i: (layer, i // tiles_per_seq, 0, i % tiles_per_seq))
    else:
        stacked = pl.BlockSpec((None, tm, kv), lambda i: (layer, i, 0))
    in_place = pl.BlockSpec(memory_space=pl.ANY)
    outs = [(D_MODEL, F32), (CONV_DIM, F32), (H_F * HD_F, q_dtype), None, (kv, BF16),
            None, (kv, BF16), (D_MODEL, F32), (D_MODEL, F32), (LANE, F32)]
    return pl.pallas_call(
        functools.partial(_inproj_kernel, q_scale=q_scale, kv_major=kv_major),
        grid=(m // tm,),
        in_specs=[row(D_MODEL),
                  pl.BlockSpec((1, D_MODEL), lambda i: (0, 0)),
                  _mod_spec(mod_rows, tm, tiles_per_seq, 3),
                  _mod_spec(mod_rows, tm, tiles_per_seq, 4),
                  _layer_resident(w, layer), const(g256), const(qnw), const(knw), const(sbias),
                  in_place, in_place],
        out_specs=[stacked if o is None else row(o[0]) for o in outs],
        out_shape=[jax.ShapeDtypeStruct(k_all.shape, F32) if o is None else jax.ShapeDtypeStruct((m, o[0]), o[1])
                   for o in outs],
        input_output_aliases={9: 3, 10: 5},
        scratch_shapes=[pltpu.VMEM((tm, D_MODEL), BF16)],
        compiler_params=_params("parallel"),
        name="inproj",
    )(h, nw, mod, mod, w, g256, qnw, knw, sbias, k_all, v_all)


def _fcum_kernel(sm_ref, kb_ref, tri_ref, pq_ref, pk_ref, oq_ref, ok_ref, fq_ref, kcat_ref, carry_ref):
    @pl.when(pl.program_id(1) == 0)
    def _():
        carry_ref[...] = jnp.zeros_like(carry_ref)

    cum = _dot01_l(tri_ref[...], sm_ref[...]) + carry_ref[0:1, :]
    carry_ref[0:1, :] = cum[cum.shape[0] - 1:, :]
    pieces = _split3(cum * LOG2E)

    def place(p_ref):
        return sum(jnp.dot(pieces[i], p_ref[i], preferred_element_type=F32) for i in range(3))

    fq_ref[...] = (place(pq_ref) + oq_ref[...]).astype(BF16)
    fk = (ok_ref[...] - place(pk_ref)).astype(BF16)
    for pair in range(KV_F // 2):
        kcat_ref[:, 2 * pair * LANE:(2 * pair + 1) * LANE] = kb_ref[:, pair * LANE:(pair + 1) * LANE]
        kcat_ref[:, (2 * pair + 1) * LANE:(2 * pair + 2) * LANE] = fk


def _fcum(small, kb, consts, bsz, s_len):
    tri = consts["tri_f"]
    t = tri.shape[0]
    nt = s_len // t
    kvw = KV_F * HD_F
    row = lambda n: pl.BlockSpec((t, n), lambda b, i: (b * nt + i, 0))
    const = lambda a: _resident(a.shape, lambda b, i: (0,) * a.ndim)
    cs = [tri, consts["place_q"], consts["place_k"], consts["ones_q"], consts["ones_k"]]
    return pl.pallas_call(
        _fcum_kernel,
        grid=(bsz, nt),
        in_specs=[row(LANE), row(kvw)] + [const(a) for a in cs],
        out_specs=[row(LANE), row(2 * kvw)],
        out_shape=[jax.ShapeDtypeStruct((bsz * s_len, LANE), BF16),
                   jax.ShapeDtypeStruct((bsz * s_len, 2 * kvw), BF16)],
        scratch_shapes=[pltpu.VMEM((8, LANE), F32)],
        compiler_params=_params("parallel", "arbitrary"),
        name="fcum",
    )(small, kb, *cs)


def _mamba_kernel(xbc_ref, z_ref, sm_ref, cprev_ref, h0_ref, cw_ref, cb_ref, alog_ref, dsk_ref, gnw_ref,
                  tri_ref, e_ref, ya_ref, cnew_ref, ssm_ref, hist_ref, dt_ref, *, rows, valid):
    q = CHUNK
    c = pl.program_id(1)
    di = D_MODEL

    @pl.when(c == 0)
    def _():
        hist_ref[0:8, :] = cprev_ref[...]
        ssm_ref[...] = h0_ref[...]
        if rows < q:
            hist_ref[8 + rows:, :] = jnp.zeros((q - rows, CONV_DIM), F32)
            dt_ref[...] = jnp.zeros_like(dt_ref)

    hist_ref[8:8 + rows, :] = xbc_ref[...]
    conv = cb_ref[...] + hist_ref[5:5 + q, :] * cw_ref[0:1, :]
    for j in range(1, CONV_W):
        conv = conv + hist_ref[5 + j:5 + j + q, :] * cw_ref[j:j + 1, :]
    cnew_ref[...] = hist_ref[5 + valid:8 + valid, :]
    hist_ref[5:8, :] = hist_ref[5 + q:8 + q, :]
    act = conv * _sigmoid(conv)
    xs = act[:, :di]
    bmat = [act[:, di + g * N_A:di + (g + 1) * N_A].astype(BF16) for g in range(G_A)]
    cmat = [act[:, di + (G_A + g) * N_A:di + (G_A + g + 1) * N_A].astype(BF16) for g in range(G_A)]

    lane = lax.broadcasted_iota(jnp.int32, (1, LANE), 1)
    a_row = jnp.where(lane < H_A, -jnp.exp(alog_ref[...]), 0.0)
    if rows < q:
        rid = lax.broadcasted_iota(jnp.int32, (rows, LANE), 0)
        dt_ref[0:rows, :] = jnp.where(rid < valid, sm_ref[...], 0.0)
        dt = dt_ref[...]
    else:
        dt = sm_ref[...]
    tri = tri_ref[...]
    acum = _dot01_l(tri, dt * a_row)
    acum_t = acum.T
    expand = e_ref[...]
    dt_e = _dot01_r(dt, expand)
    acum_e = _dot01_r(acum, expand)
    xdt = xs * dt_e

    rq = lax.broadcasted_iota(jnp.int32, (q, q), 0)
    rs = lax.broadcasted_iota(jnp.int32, (q, q), 1)
    causal = rs <= rq
    half = lax.broadcasted_iota(jnp.int32, (q, LANE), 1) < P_A
    cb = [_dot_nt(cmat[g], bmat[g]) for g in range(G_A)]
    xdt_b = xdt.astype(BF16)
    y_parts = []
    for pair in range(H_A // 2):
        res = []
        for o in range(2):
            h = 2 * pair + o
            seg = acum[:, h:h + 1] - acum_t[h:h + 1, :]
            mix = (cb[h // (H_A // G_A)] * jnp.where(causal, jnp.exp(seg), 0.0)).astype(BF16)
            res.append(jnp.dot(mix, xdt_b[:, pair * LANE:(pair + 1) * LANE], preferred_element_type=F32))
        y_parts.append(jnp.where(half, res[0], res[1]))
    y = jnp.concatenate(y_parts, axis=1)

    hg = di // G_A
    state = ssm_ref[...]
    state_b = state.astype(BF16)
    y_off = jnp.concatenate([_dot_nt(cmat[g], state_b[g * hg:(g + 1) * hg, :]) for g in range(G_A)], axis=1)
    y = y + y_off * jnp.exp(acum_e)
    xdt_t = xdt.T
    acum_et = acum_e.T
    a_end = acum_et[:, q - 1:q]
    xdtd_t = (xdt_t * jnp.exp(a_end - acum_et)).astype(BF16)
    upd = jnp.concatenate(
        [jnp.dot(xdtd_t[g * hg:(g + 1) * hg, :], bmat[g], preferred_element_type=F32) for g in range(G_A)], axis=0)
    ssm_ref[...] = jnp.exp(a_end) * state + upd

    y = (y + dsk_ref[...] * xs)[0:rows, :]
    zz = z_ref[...]
    gated = y * (zz * _sigmoid(zz))
    out = gated * lax.rsqrt(jnp.mean(gated * gated, axis=-1, keepdims=True) + EPS) * gnw_ref[...]
    ya_ref[...] = out.astype(ya_ref.dtype)


def _mamba(xbc, z, small, cprev8, h0, layer, cw8, cb, alog, dsk, gnw, tri, expand, nseq, nchunk, rows, valid,
           ya_dtype):
    m = xbc.shape[0]
    row = lambda n: pl.BlockSpec((rows, n), lambda b, c: (b * nchunk + c, 0))
    const = lambda a: _resident(a.shape, lambda b, c: (0,) * a.ndim)
    di = D_MODEL
    kern = functools.partial(_mamba_kernel, rows=rows, valid=valid)
    return pl.pallas_call(
        kern,
        grid=(nseq, nchunk),
        in_specs=[row(CONV_DIM), row(di), row(LANE),
                  pl.BlockSpec((None, 8, CONV_DIM), lambda b, c: (b, 0, 0)),
                  pl.BlockSpec((None, None, di, N_A), lambda b, c: (layer, b, 0, 0)),
                  const(cw8), const(cb), const(alog), const(dsk), const(gnw), const(tri), const(expand)],
        out_specs=[row(di),
                   pl.BlockSpec((None, CONV_W - 1, CONV_DIM), lambda b, c: (b, 0, 0)),
                   pl.BlockSpec((None, di, N_A), lambda b, c: (b, 0, 0))],
        out_shape=[jax.ShapeDtypeStruct((m, di), ya_dtype),
                   jax.ShapeDtypeStruct((nseq, CONV_W - 1, CONV_DIM), F32),
                   jax.ShapeDtypeStruct((nseq, di, N_A), F32)],
        scratch_shapes=[pltpu.VMEM((8 + CHUNK, CONV_DIM), F32), pltpu.VMEM((CHUNK, LANE), F32)],
        compiler_params=_params("parallel", "arbitrary"),
        name="mamba",
    )(xbc, z, small, cprev8, h0, cw8, cb, alog, dsk, gnw, tri, expand)


def _attn_kernel(q_ref, fq_ref, k_ref, v_ref, o_ref, qcat_ref, m_ref, l_ref, acc_ref, *, tq, tk):
    qi = pl.program_id(1)
    ki = pl.program_id(2)
    nk = pl.num_programs(2)

    @pl.when(ki == 0)
    def _():
        m_ref[...] = jnp.full_like(m_ref, NEG)
        l_ref[...] = jnp.zeros_like(l_ref)
        acc_ref[...] = jnp.zeros_like(acc_ref)
        lane = lax.broadcasted_iota(jnp.int32, (tq, LANE), 1)
        lo = lane < HD_F
        fq = fq_ref[...]
        for blk in range(H_F // 2):
            qb = q_ref[:, blk * LANE:(blk + 1) * LANE]
            for o in range(2):
                slot = 2 * blk + o
                hq = HEAD_PERM[slot]
                own = jnp.logical_and(lane >= F_LANES * hq, lane < F_LANES * (hq + 1))
                qcat_ref[slot, :, 0:LANE] = jnp.where(lo if o == 0 else jnp.logical_not(lo), qb, jnp.zeros_like(qb))
                qcat_ref[slot, :, LANE:2 * LANE] = jnp.where(own, fq, jnp.zeros_like(fq))

    tn = (((0,), (0,)), ((), ()))

    def full_step():
        def scores(slot):
            pair = slot // 4
            return _dot_nt(k_ref[:, 2 * pair * LANE:(2 * pair + 2) * LANE], qcat_ref[slot])

        ahead = [scores(0), scores(1)]
        for slot in range(H_F):
            pair = slot // 4
            s = ahead.pop(0)
            if slot + 2 < H_F:
                ahead.append(scores(slot + 2))
            m_old = m_ref[slot]
            m_new = jnp.maximum(m_old, jnp.max(s, axis=0, keepdims=True))
            alpha = jnp.exp2(m_old - m_new)
            p = jnp.exp2(s - m_new)
            l_ref[slot] = alpha * l_ref[slot] + jnp.sum(p, axis=0, keepdims=True)
            m_ref[slot] = m_new
            pv = lax.dot_general(v_ref[:, pair * LANE:(pair + 1) * LANE], p.astype(BF16), tn,
                                 preferred_element_type=F32)
            acc_ref[slot] = alpha * acc_ref[slot] + pv

    def diagonal_step():
        hk, hq = tk // 2, tq // 2
        causal_a = (lax.broadcasted_iota(jnp.int32, (hk, tq), 0) <= lax.broadcasted_iota(jnp.int32, (hk, tq), 1))
        causal_b = (lax.broadcasted_iota(jnp.int32, (hk, hq), 0) <= lax.broadcasted_iota(jnp.int32, (hk, hq), 1))
        def scores(slot):
            kl = slice(2 * (slot // 4) * LANE, (2 * (slot // 4) + 2) * LANE)
            return (_dot_nt(k_ref[0:hk, kl], qcat_ref[slot]),
                    _dot_nt(k_ref[hk:tk, kl], qcat_ref[slot, hq:tq, :]))

        ahead = [scores(0), scores(1)]
        for slot in range(H_F):
            pair = slot // 4
            vl = slice(pair * LANE, (pair + 1) * LANE)
            s_a, s_b = ahead.pop(0)
            if slot + 2 < H_F:
                ahead.append(scores(slot + 2))
            s_a = jnp.where(causal_a, s_a, NEG)
            s_b = jnp.where(causal_b, s_b, NEG)
            m_old = m_ref[slot]
            m_a = jnp.max(s_a, axis=0, keepdims=True)
            m_b = jnp.max(s_b, axis=0, keepdims=True)
            m_new = jnp.maximum(m_old, jnp.concatenate([m_a[:, 0:hq], jnp.maximum(m_a[:, hq:tq], m_b)], axis=1))
            alpha = jnp.exp2(m_old - m_new)
            p_a = jnp.exp2(s_a - m_new)
            p_b = jnp.exp2(s_b - m_new[:, hq:tq])
            l_add = jnp.sum(p_a, axis=0, keepdims=True)
            l_b = jnp.sum(p_b, axis=0, keepdims=True)
            l_ref[slot] = alpha * l_ref[slot] + jnp.concatenate([l_add[:, 0:hq], l_add[:, hq:tq] + l_b], axis=1)
            m_ref[slot] = m_new
            pv_a = lax.dot_general(v_ref[0:hk, vl], p_a.astype(BF16), tn, preferred_element_type=F32)
            pv_b = lax.dot_general(v_ref[hk:tk, vl], p_b.astype(BF16), tn, preferred_element_type=F32)
            acc_ref[slot] = alpha * acc_ref[slot] + jnp.concatenate(
                [pv_a[:, 0:hq], pv_a[:, hq:tq] + pv_b], axis=1)

    pl.when(ki < qi)(full_step)
    pl.when(ki == qi)(diagonal_step)

    @pl.when(ki == nk - 1)
    def _():
        lo = lax.broadcasted_iota(jnp.int32, (LANE, tq), 0) < HD_F
        for blk in range(H_F // 2):
            a0 = acc_ref[2 * blk] / l_ref[2 * blk]
            a1 = acc_ref[2 * blk + 1] / l_ref[2 * blk + 1]
            o_ref[:, blk * LANE:(blk + 1) * LANE] = jnp.where(lo, a0, a1).T.astype(o_ref.dtype)


def _attn(qb, fq, kcat, vb, bsz, s_len, tq):
    tk = tq
    nq = s_len // tq
    kern = functools.partial(_attn_kernel, tq=tq, tk=tk)
    kvw = KV_F * HD_F
    q_spec = lambda n: pl.BlockSpec((tq, n), lambda b, i, j: (b * nq + i, 0))
    k_spec = lambda n: pl.BlockSpec((tk, n), lambda b, i, j: (b * nq + jnp.minimum(i, j), 0))
    return pl.pallas_call(
        kern,
        grid=(bsz, nq, nq),
        in_specs=[q_spec(D_MODEL), q_spec(LANE), k_spec(2 * kvw), k_spec(kvw)],
        out_specs=q_spec(D_MODEL),
        out_shape=jax.ShapeDtypeStruct((bsz * s_len, D_MODEL), BF16),
        scratch_shapes=[pltpu.VMEM((H_F, tq, 2 * LANE), BF16),
                        pltpu.VMEM((H_F, 1, tq), F32), pltpu.VMEM((H_F, 1, tq), F32),
                        pltpu.VMEM((H_F, LANE, tq), F32)],
        compiler_params=_params("parallel", "parallel", "arbitrary"),
        name="attn",
    )(qb, fq, kcat, vb)


def _dec_kernel(pt_ref, q_ref, kn_ref, vn_ref, sm_ref, *rest, npp, tokens, valid):
    k_refs = rest[0:npp]
    v_refs = rest[npp:2 * npp]
    f_refs = rest[2 * npp:3 * npp]
    (et128_ref, tri_ref, slt_ref, o_ref, smp_ref, m_ref, l_ref, gq_ref, carry_ref, acc_ref) = rest[3 * npp:]
    g = pl.program_id(1)
    ng = pl.num_programs(1)
    kvw = KV_F * HD_F
    qbd = q_ref[...].astype(BF16)

    def update(scores, values_t):
        m_old = m_ref[...]
        m_new = jnp.maximum(m_old, jnp.max(functools.reduce(jnp.maximum, scores), axis=1, keepdims=True))
        alpha = jnp.exp(m_old - m_new)
        p_sum = jnp.zeros(scores[0].shape, F32)
        pv = jnp.zeros((LANE, kvw), F32)
        for s, vt in zip(scores, values_t):
            p = jnp.exp(s - m_new)
            p_sum = p_sum + p
            pv = pv + _dot_nt(p.astype(BF16), vt)
        l_ref[...] = alpha * l_ref[...] + jnp.sum(p_sum, axis=1, keepdims=True)
        m_ref[...] = m_new
        acc_ref[...] = alpha * acc_ref[...] + pv

    @pl.when(g == 0)
    def _():
        m_ref[...] = jnp.full_like(m_ref, NEG)
        l_ref[...] = jnp.zeros_like(l_ref)
        carry_ref[...] = jnp.zeros_like(carry_ref)
        acc_ref[...] = jnp.zeros_like(acc_ref)
        smp_ref[...] = jnp.zeros_like(smp_ref)
        smp_ref[0:tokens, :] = sm_ref[...]
        cum_t = _dot01_l(tri_ref[...], smp_ref[...]).T
        cum_e = _dot01_l(et128_ref[...], cum_t)
        key = lax.broadcasted_iota(jnp.int32, (LANE, PAGE), 1)
        qtok = lax.broadcasted_iota(jnp.int32, (LANE, PAGE), 0) % tokens
        g_q = jnp.sum(jnp.where(key == qtok, cum_e, 0.0), axis=1, keepdims=True)
        gq_ref[...] = g_q
        ok = jnp.logical_and(key <= qtok, key < valid)
        s = jnp.dot(qbd, kn_ref[...].astype(BF16), preferred_element_type=F32) + (g_q - cum_e)
        update([jnp.where(ok, s, NEG)], [vn_ref[...].astype(BF16)])

    lf = jnp.concatenate([f_refs[i][...] for i in range(npp)], axis=0)
    later = _dot01_r(lf, slt_ref[...])
    mass = later[:, 0:1] + lf[:, 0:1]
    carry = carry_ref[...]
    g_q = gq_ref[...]
    biases = [None] * npp
    for i in reversed(range(npp)):
        per_head = later[i * H_F:(i + 1) * H_F, :] + carry
        biases[i] = jnp.concatenate(
            [jnp.broadcast_to(per_head[h:h + 1, :], (tokens, PAGE)) for h in range(H_F)], axis=0) + g_q
        carry = carry + mass[i * H_F:(i + 1) * H_F, :]
    carry_ref[...] = carry
    group = 2 if npp % 2 == 0 else 1
    page_t = lambda refs, i: refs[i][...].reshape(kvw, PAGE).astype(BF16)
    side_by_side = lambda parts: parts[0] if len(parts) == 1 else jnp.concatenate(parts, axis=1)
    scores, values_t = [], []
    for i0 in range(0, npp, group):
        ids = range(i0, i0 + group)
        kt = side_by_side([page_t(k_refs, i) for i in ids])
        scores.append(jnp.dot(qbd, kt, preferred_element_type=F32) + side_by_side([biases[i] for i in ids]))
        values_t.append(side_by_side([page_t(v_refs, i) for i in ids]))
    update(scores, values_t)

    @pl.when(g == ng - 1)
    def _():
        o_ref[...] = acc_ref[...] / l_ref[...]


def _dec_attn(page_table, qbd, kn_t, vn_t, small, cache_kt, cache_vt, cache_ft, layer, dc, tokens, valid, npp):
    nseq, n_pages = page_table.shape
    ng = n_pages // npp
    kvw = KV_F * HD_F
    rows = H_F * tokens
    kern = functools.partial(_dec_kernel, npp=npp, tokens=tokens, valid=valid)
    per_seq = lambda r, c: pl.BlockSpec((None, r, c), lambda b, g, pt: (b, 0, 0))
    page_idx = lambda b, g, pt, i: pt[b, (ng - 1 - g) * npp + i]
    kv_page = lambda i: pl.BlockSpec((None, None, KV_F, HD_F, PAGE),
                                     lambda b, g, pt: (layer, page_idx(b, g, pt, i), 0, 0, 0))
    f_page = lambda i: pl.BlockSpec((None, None, H_F, PAGE), lambda b, g, pt: (layer, page_idx(b, g, pt, i), 0, 0))
    const = lambda a: _resident(a.shape, lambda b, g, pt: (0,) * a.ndim)
    cs = [dc["et128"], dc["tri"], dc["slt"]]
    in_specs = ([per_seq(rows, kvw), per_seq(kvw, PAGE), per_seq(kvw, PAGE),
                 pl.BlockSpec((tokens, LANE), lambda b, g, pt: (b, 0))]
                + [kv_page(i) for i in range(npp)] + [kv_page(i) for i in range(npp)]
                + [f_page(i) for i in range(npp)] + [const(a) for a in cs])
    col = pltpu.VMEM((rows, 1), F32)
    grid_spec = pltpu.PrefetchScalarGridSpec(
        num_scalar_prefetch=1,
        grid=(nseq, ng),
        in_specs=in_specs,
        out_specs=per_seq(rows, kvw),
        scratch_shapes=[pltpu.VMEM((PAGE, LANE), F32), col, col, col, pltpu.VMEM((H_F, 1), F32),
                        pltpu.VMEM((rows, kvw), F32)])
    return pl.pallas_call(
        kern,
        grid_spec=grid_spec,
        out_shape=jax.ShapeDtypeStruct((nseq, rows, kvw), F32),
        compiler_params=_params("parallel", "arbitrary"),
        name="dec_attn",
    )(page_table, qbd, kn_t, vn_t, small, *([cache_kt] * npp), *([cache_vt] * npp), *([cache_ft] * npp), *cs)


def _merge_kernel(h_ref, ya_ref, yb_ref, ga_ref, gb_ref, gt_ref, wa_ref, wb_ref, wo_ref, o_ref):
    tm = h_ref.shape[0]
    halves = [slice(0, tm // 2), slice(tm // 2, tm)] if tm % 16 == 0 else [slice(0, tm)]
    proj = [(jnp.dot(ya_ref[r, :].astype(BF16), wa_ref[...], preferred_element_type=F32),
             jnp.dot(yb_ref[r, :].astype(BF16), wb_ref[...], preferred_element_type=F32)) for r in halves]
    for r, (pa, pb) in zip(halves, proj):
        merged = _sigmoid(ga_ref[r, :]) * pa + _sigmoid(gb_ref[r, :]) * pb
        out = jnp.dot(merged.astype(BF16), wo_ref[...], preferred_element_type=F32)
        gate = gt_ref[...] if gt_ref.shape[0] == 1 else gt_ref[r, :]
        o_ref[r, :] = h_ref[r, :] + gate * out


def _merge(h, ya, yb, ga, gb, mod, wa, wb, wo, layer, tm, tiles_per_seq, mod_rows):
    m = h.shape[0]
    row = pl.BlockSpec((tm, D_MODEL), lambda i: (i, 0))
    return pl.pallas_call(
        _merge_kernel,
        grid=(m // tm,),
        in_specs=[row, row, row, row, row, _mod_spec(mod_rows, tm, tiles_per_seq, 5),
                  _layer_resident(wa, layer), _layer_resident(wb, layer), _layer_resident(wo, layer)],
        out_specs=row,
        out_shape=jax.ShapeDtypeStruct((m, D_MODEL), F32),
        compiler_params=_params("parallel"),
        name="merge",
    )(h, ya, yb, ga, gb, mod, wa, wb, wo)


def _constants():
    f = np.float32
    head = np.arange(D_MODEL) // HD_F
    g256 = (head[:256, None] == head[None, :256]).astype(f)
    tri128 = np.tril(np.ones((CHUNK, CHUNK), f))
    tri_f = np.tril(np.ones((FCUM_TILE, FCUM_TILE), f))
    striu = np.triu(np.ones((PAGE, PAGE), f), 1)
    expand = np.zeros((LANE, D_MODEL), f)
    expand[head, np.arange(D_MODEL)] = 1.0
    place_q = np.zeros((3, LANE, LANE), f)
    place_k = np.zeros((3, LANE, LANE), f)
    ones_q = np.zeros((1, LANE), f)
    ones_k = np.zeros((1, LANE), f)
    for h in range(H_F):
        for i in range(3):
            place_q[i, H_A + h, F_LANES * h + i] = 1.0
            place_k[i, H_A + h, F_LANES * h + 3 + i] = 1.0
            ones_q[0, F_LANES * h + 3 + i] = 1.0
            ones_k[0, F_LANES * h + i] = 1.0
    out = {k: jnp.asarray(v, BF16) for k, v in
           dict(g256=g256, tri128=tri128, tri_f=tri_f, striu=striu, expand=expand,
                place_q=place_q, place_k=place_k).items()}
    out.update(ones_q=jnp.asarray(ones_q), ones_k=jnp.asarray(ones_k))
    return out


def _dec_constants(tokens, consts):
    f = np.float32
    et128 = np.zeros((H_F * tokens, LANE), f)
    for hq in range(H_F):
        et128[hq * tokens:(hq + 1) * tokens, H_A + hq] = 1.0
    slt = np.tril(np.ones((PAGE, PAGE), f), -1)
    return dict(et128=jnp.asarray(et128, BF16), tri=consts["tri128"], slt=jnp.asarray(slt, BF16))


def _inproj_weights(w_in):
    depth = w_in.shape[0]
    sizes = (D_MODEL, CONV_DIM, H_A, H_F * HD_F, KV_F * HD_F, KV_F * HD_F, H_F, D_MODEL, D_MODEL)
    pts = np.cumsum(sizes)[:-1].tolist()
    wz, wx, wdt, wq, wk, wv, wf, wga, wgb = jnp.split(w_in.astype(BF16), pts, axis=-1)
    wq = wq.reshape(depth, D_MODEL, H_F, HD_F)[:, :, list(HEAD_PERM), :].reshape(depth, D_MODEL, H_F * HD_F)
    pad = jnp.zeros((depth, D_MODEL, LANE - H_A - H_F), BF16)
    return jnp.concatenate([wz, wx, wq, wk, wv, wga, wgb, wdt, wf, pad], axis=-1)


def _run_group(x2d, mod, mod_rows, tiles, conv_prev, ssm_prev, lw, consts, fox, nseq, nchunk, rows, valid,
               q_dtype, q_scale, y_dtype, kv_major):
    tm_ffn, tm_in, tm_mg, tps_ffn, tps_in, tps_mg = tiles
    h = x2d
    states = []
    depth = lw["w_in"].shape[0]
    kv_shape = ((depth, nseq, KV_F * HD_F, x2d.shape[0] // nseq) if kv_major
                else (depth, x2d.shape[0], KV_F * HD_F))
    k_all = jnp.zeros(kv_shape, F32)
    v_all = jnp.zeros(kv_shape, F32)
    for l in range(depth):
        ml = mod[l]
        h = _ffn(h, ml, 0, lw["nw"][l, 0:1], lw["wi1"], lw["wo1"], l, tm_ffn, tps_ffn, mod_rows)
        z, xbc, q, k_all, kb, v_all, vb, ga, gb, small = _inproj(
            h, ml, lw["nw"][l, 1:2], lw["w_in"], l, consts["g256"], lw["qnw"][l], lw["knw"][l], lw["sbias"][l],
            k_all, v_all, tm_in, tps_in, mod_rows, q_dtype, q_scale)
        cprev8 = jnp.pad(conv_prev[l], ((0, 0), (8 - (CONV_W - 1), 0), (0, 0)))
        ya, conv_new, ssm_new = _mamba(
            xbc, z, small, cprev8, ssm_prev, l, lw["cw8"][l], lw["cb"][l], lw["alog"][l], lw["dsk"][l], lw["gnw"][l],
            consts["tri128"], consts["expand"], nseq, nchunk, rows, valid, y_dtype)
        yb = fox(l, q, k_all, kb, v_all, vb, small)
        h = _merge(h, ya, yb, ga, gb, ml, lw["wa"], lw["wb"], lw["wout"], l, tm_mg, tps_mg, mod_rows)
        h = _ffn(h, ml, 6, lw["nw"][l, 2:3], lw["wi2"], lw["wo2"], l, tm_ffn, tps_ffn, mod_rows)
        states.append((small, conv_new, ssm_new))
    return h, k_all, v_all, states


def kernel(x_prompt, x_sample, cache_k, cache_v, cache_logf, state_conv, state_ssm, page_table,
           c_prompt, c_sample, w_ada, b_ada, norm_w, w_ffn1_in, w_ffn1_out, w_ffn2_in, w_ffn2_out,
           w_in, conv_w, conv_b, dt_bias, a_log, d_skip, gn_w, q_norm_w, k_norm_w, b_f,
           w_proj_a, w_proj_b, w_out):
    bp, s_len, d = x_prompt.shape
    db, t_len, _ = x_sample.shape
    depth = w_in.shape[0]
    n_pool = cache_k.shape[1]
    kvw = KV_F * HD_F
    assert d == D_MODEL and s_len % CHUNK == 0 and t_len <= 8 and cache_k.shape[2] == PAGE
    consts = _constants()

    pad_lanes = lambda a: jnp.pad(a, ((0, 0), (0, LANE - a.shape[1])))[:, None, :]
    lw = dict(
        nw=norm_w,
        wi1=w_ffn1_in.astype(BF16), wo1=w_ffn1_out.astype(BF16),
        wi2=w_ffn2_in.astype(BF16), wo2=w_ffn2_out.astype(BF16),
        w_in=_inproj_weights(w_in),
        qnw=jnp.tile(q_norm_w, (1, 256 // HD_F))[:, None, :],
        knw=jnp.tile(k_norm_w, (1, 256 // HD_F))[:, None, :],
        sbias=pad_lanes(jnp.concatenate([dt_bias, b_f], axis=1)),
        cw8=jnp.pad(conv_w, ((0, 0), (0, 8 - CONV_W), (0, 0))),
        cb=conv_b[:, None, :],
        alog=pad_lanes(a_log),
        dsk=jnp.repeat(d_skip, P_A, axis=1)[:, None, :],
        gnw=gn_w[:, None, :],
        wa=w_proj_a.astype(BF16),
        wb=w_proj_b.reshape(depth, H_F, HD_F, d)[:, list(HEAD_PERM)].reshape(depth, d, d).astype(BF16),
        wout=w_out.astype(BF16),
    )

    mod_all = _ada(jnp.concatenate([c_prompt, c_sample], axis=0), w_ada, b_ada)
    mod_all = mod_all.reshape(depth, bp + db, N_MOD, d)
    mod_p = mod_all[:, :bp, :, None, :]
    tok = 8
    mod_s = jnp.broadcast_to(mod_all[:, bp:, None], (depth, db, tok, N_MOD, d))
    mod_s = mod_s.reshape(depth, db * tok, N_MOD, d).transpose(0, 2, 1, 3)

    tq = min(512, s_len)

    def fox_p(l, q, k, kb, v, vb, small):
        fq, kcat = _fcum(small, kb, consts, bp, s_len)
        return _attn(q, fq, kcat, vb, bp, s_len, tq)

    tm = min(512, s_len)
    tm_in = min(512, s_len)
    tiles_p = (tm, tm_in, tm, s_len // tm, s_len // tm_in, s_len // tm)
    conv0 = jnp.zeros((depth, bp, CONV_W - 1, CONV_DIM), F32)
    ssm0 = jnp.zeros((depth, bp, D_MODEL, N_A), F32)
    y_p, kp_all, vp_all, st_p = _run_group(x_prompt.reshape(bp * s_len, d), mod_p, 1, tiles_p, conv0, ssm0, lw, consts, fox_p,
                           bp, s_len // CHUNK, CHUNK, CHUNK, BF16, ATTN_SCALE * LOG2E, BF16, True)

    dc = _dec_constants(tok, consts)
    npp = next(n for n in (16, 8, 4, 2, 1) if page_table.shape[1] % n == 0)
    head_inv = list(np.argsort(HEAD_PERM))

    cache_kt = jnp.transpose(cache_k, (0, 1, 3, 4, 2))
    cache_vt = jnp.transpose(cache_v, (0, 1, 3, 4, 2))
    cache_ft = jnp.transpose(cache_logf, (0, 1, 3, 2))
    kv_of_head = np.arange(H_F) // (H_F // KV_F)
    own_kv = jnp.asarray(kv_of_head[:, None] == np.arange(KV_F)[None, :], F32)
    assert H_F * tok == LANE

    def fox_s(l, q, k, kb, v, vb, small):
        q2 = q.reshape(db, tok, H_F, HD_F)[:, :, head_inv].transpose(0, 2, 1, 3)
        qbd = (q2[:, :, :, None, :] * own_kv[None, :, None, :, None]).reshape(db, H_F * tok, kvw)
        new_t = lambda a: jnp.pad(a.reshape(db, tok, kvw).transpose(0, 2, 1), ((0, 0), (0, 0), (0, PAGE - tok)))
        out = _dec_attn(page_table, qbd, new_t(k[l]), new_t(v[l]), small, cache_kt, cache_vt, cache_ft, l, dc,
                        tok, t_len, npp)
        out = out.reshape(db, H_F, tok, KV_F, HD_F)
        pick = jnp.broadcast_to(jnp.asarray(kv_of_head)[None, :, None, None, None], (db, H_F, tok, 1, HD_F))
        out = jnp.take_along_axis(out, pick, axis=3)[:, :, :, 0]
        return out[:, list(HEAD_PERM)].transpose(0, 2, 1, 3).reshape(db * tok, d)

    ms = db * tok
    tiles_s = (ms, ms, ms, 1, 1, 1)
    x_s = jnp.pad(x_sample, ((0, 0), (0, tok - t_len), (0, 0))).reshape(ms, d)
    y_s, ks_all, vs_all, st_s = _run_group(x_s, mod_s, ms, tiles_s, state_conv, state_ssm.reshape(depth, db, D_MODEL, N_A), lw, consts,
                           fox_s, db, 1, tok, t_len, F32, ATTN_SCALE, F32, False)

    def gather(k_all, v_all, states, nseq, length, keep):
        if k_all.ndim == 4:
            k = k_all.reshape(depth, nseq, KV_F, HD_F, length).transpose(0, 1, 4, 2, 3)
            v = v_all.reshape(depth, nseq, KV_F, HD_F, length).transpose(0, 1, 4, 2, 3)
        else:
            k = k_all.reshape(depth, nseq, length, KV_F, HD_F)[:, :, :keep]
            v = v_all.reshape(depth, nseq, length, KV_F, HD_F)[:, :, :keep]
        lf = jnp.stack([s[0][:, H_A:H_A + H_F] for s in states]).reshape(depth, nseq, length, H_F)[:, :, :keep]
        conv = jnp.stack([s[1] for s in states])
        ssm = jnp.stack([s[2] for s in states]).reshape(depth, nseq, H_A, P_A, N_A)
        return k, v, lf, conv, ssm

    k_p, v_p, lf_p, conv_p, ssm_p = gather(kp_all, vp_all, st_p, bp, s_len, s_len)
    k_s, v_s, lf_s, conv_s, ssm_s = gather(ks_all, vs_all, st_s, db, tok, t_len)
    y_prompt = y_p.reshape(bp, s_len, d)
    y_sample = y_s.reshape(db, tok, d)[:, :t_len]
    return (y_prompt, y_sample, k_p, v_p, lf_p, conv_p, ssm_p, k_s, v_s, lf_s, conv_s, ssm_s)
```

```python
import functools

import numpy as np
import jax
import jax.numpy as jnp
from jax import lax
from jax.experimental import pallas as pl
from jax.experimental.pallas import tpu as pltpu

F32 = jnp.float32
BF16 = jnp.bfloat16

D_MODEL = 1024
D_FF = 2816
P_A = 64
H_A = 16
G_A = 2
N_A = 128
CONV_W = 4
CONV_DIM = D_MODEL + 2 * G_A * N_A
HD_F = 64
H_F = 16
KV_F = 8
N_MOD = 9
RESID_HALF = 0.5
EPS = 1e-6
ATTN_SCALE = HD_F ** -0.5
PAGE = 128
CHUNK = 128
NEG = -1e30
LOG2E = 1.4426950408889634
F_LANES = 6
DEC_ROWS = PAGE * KV_F
FCUM_TILE = 512

LANE = 128
FF_TILE = 256
OFF_Z = 0
OFF_X = OFF_Z + D_MODEL
OFF_Q = OFF_X + CONV_DIM
OFF_K = OFF_Q + H_F * HD_F
OFF_V = OFF_K + KV_F * HD_F
OFF_GA = OFF_V + KV_F * HD_F
OFF_GB = OFF_GA + D_MODEL
OFF_S = OFF_GB + D_MODEL
IN_W = OFF_S + LANE
HEAD_PERM = tuple(4 * j + o for j in range(4) for o in (0, 2, 1, 3))
VMEM_LIMIT = 56 * 1024 * 1024


def _params(*sem):
    return pltpu.CompilerParams(dimension_semantics=sem, vmem_limit_bytes=VMEM_LIMIT)


def _resident(shape, index_map):
    return pl.BlockSpec(shape, index_map, pipeline_mode=pl.Buffered(1))


def _sigmoid(x):
    return 1.0 / (1.0 + jnp.exp(-x))


def _norm_mod(x, nw, shift, scale):
    y = x * lax.rsqrt(jnp.mean(x * x, axis=-1, keepdims=True) + EPS) * nw
    return y * (1.0 + scale) + shift


def _split3(x):
    hi = x.astype(BF16)
    r = x - hi.astype(F32)
    mid = r.astype(BF16)
    lo = (r - mid.astype(F32)).astype(BF16)
    return hi, mid, lo


def _dot01_l(m01, x):
    hi, mid, lo = _split3(x)
    d = lambda p: jnp.dot(m01, p, preferred_element_type=F32)
    return d(hi) + d(mid) + d(lo)


def _dot01_r(x, m01):
    hi, mid, lo = _split3(x)
    d = lambda p: jnp.dot(p, m01, preferred_element_type=F32)
    return d(hi) + d(mid) + d(lo)


def _dot_nt(a, b):
    return lax.dot_general(a, b, (((1,), (1,)), ((), ())), preferred_element_type=F32)


def _ada_kernel(c_ref, w_ref, b_ref, o_ref):
    c = c_ref[...]
    a = (c * _sigmoid(c)).astype(BF16)
    o_ref[...] = jnp.dot(a, w_ref[...].astype(BF16), preferred_element_type=F32) + b_ref[...]


def _ada(c_all, w_ada, b_ada):
    depth, d, n = w_ada.shape
    rows = c_all.shape[0]
    tn = 1024
    return pl.pallas_call(
        _ada_kernel,
        grid=(depth, n // tn),
        in_specs=[pl.BlockSpec((rows, d), lambda l, j: (0, 0)),
                  pl.BlockSpec((None, d, tn), lambda l, j: (l, 0, j)),
                  pl.BlockSpec((None, 1, tn), lambda l, j: (l, 0, j))],
        out_specs=pl.BlockSpec((None, rows, tn), lambda l, j: (l, 0, j)),
        out_shape=jax.ShapeDtypeStruct((depth, rows, n), F32),
        compiler_params=_params("parallel", "parallel"),
        name="ada",
    )(c_all, w_ada, b_ada.reshape(depth, 1, n))


def _ffn_kernel(x_ref, nw_ref, sh_ref, sc_ref, gt_ref, wi_ref, wo_ref, o_ref, u_ref, acc_ref):
    x = x_ref[...]
    u_ref[...] = _norm_mod(x, nw_ref[...], sh_ref[...], sc_ref[...]).astype(BF16)
    for c in range(D_FF // FF_TILE):
        u = u_ref[...]
        lo = c * FF_TILE
        g = jnp.dot(u, wi_ref[:, lo:lo + FF_TILE], preferred_element_type=F32)
        up = jnp.dot(u, wi_ref[:, D_FF + lo:D_FF + lo + FF_TILE], preferred_element_type=F32)
        act = ((g * _sigmoid(g)) * up).astype(BF16)
        part = jnp.dot(act, wo_ref[lo:lo + FF_TILE, :], preferred_element_type=F32)
        if c == 0:
            acc_ref[...] = part
        else:
            acc_ref[...] += part
    o_ref[...] = x + (RESID_HALF * gt_ref[...]) * acc_ref[...]


def _mod_spec(mod_rows, tm, tiles_per_seq, k):
    if mod_rows == 1:
        return pl.BlockSpec((None, None, 1, D_MODEL), lambda i: (i // tiles_per_seq, k, 0, 0))
    return pl.BlockSpec((None, tm, D_MODEL), lambda i: (k, i, 0))


def _layer_resident(a, layer):
    return _resident((None,) + a.shape[1:], lambda *_: (layer,) + (0,) * (a.ndim - 1))


def _ffn(h, mod, kmod, nw, wi, wo, layer, tm, tiles_per_seq, mod_rows):
    m = h.shape[0]
    row = pl.BlockSpec((tm, D_MODEL), lambda i: (i, 0))
    return pl.pallas_call(
        _ffn_kernel,
        grid=(m // tm,),
        in_specs=[row,
                  pl.BlockSpec((1, D_MODEL), lambda i: (0, 0)),
                  _mod_spec(mod_rows, tm, tiles_per_seq, kmod),
                  _mod_spec(mod_rows, tm, tiles_per_seq, kmod + 1),
                  _mod_spec(mod_rows, tm, tiles_per_seq, kmod + 2),
                  _layer_resident(wi, layer), _layer_resident(wo, layer)],
        out_specs=row,
        out_shape=jax.ShapeDtypeStruct((m, D_MODEL), F32),
        scratch_shapes=[pltpu.VMEM((tm, D_MODEL), BF16), pltpu.VMEM((tm, D_MODEL), F32)],
        compiler_params=_params("parallel"),
        name="ffn",
    )(h, nw, mod, mod, mod, wi, wo)


def _head_norm(x, g256, w):
    sq = x * x
    hi = sq.astype(BF16)
    lo = (sq - hi.astype(F32)).astype(BF16)
    ss = jnp.dot(hi, g256, preferred_element_type=F32) + jnp.dot(lo, g256, preferred_element_type=F32)
    return x * lax.rsqrt(ss * (1.0 / HD_F) + EPS) * w


def _inproj_kernel(x_ref, nw_ref, sh_ref, sc_ref, w_ref, g_ref, qnw_ref, knw_ref, sb_ref, k_all_ref, v_all_ref,
                   z_ref, xbc_ref, q_ref, k_ref, kb_ref, v_ref, vb_ref, ga_ref, gb_ref, sm_ref, u_ref, *, q_scale,
                   kv_major):
    u_ref[...] = _norm_mod(x_ref[...], nw_ref[...], sh_ref[...], sc_ref[...]).astype(BF16)

    def mm(lo, n):
        return jnp.dot(u_ref[...], w_ref[:, lo:lo + n], preferred_element_type=F32)

    t = 256

    def plain(ref):
        def write(c, y):
            ref[:, c * t:(c + 1) * t] = y
        return write

    def write_q(c, y):
        qn = _head_norm(y, g_ref[...], qnw_ref[...])
        q_ref[:, c * t:(c + 1) * t] = (qn * q_scale).astype(q_ref.dtype)

    def write_kv(ref, ref_b, norm):
        def write(c, y):
            if norm:
                y = _head_norm(y, g_ref[...], knw_ref[...])
            ref_b[:, c * t:(c + 1) * t] = y.astype(BF16)
            if kv_major:
                ref[c * t:(c + 1) * t, :] = y.T
            else:
                ref[:, c * t:(c + 1) * t] = y
        return write

    def write_small(c, y):
        raw = y + sb_ref[...]
        tail = jnp.log1p(jnp.exp(-jnp.abs(raw)))
        lane = lax.broadcasted_iota(jnp.int32, raw.shape, 1)
        softplus = jnp.maximum(raw, 0.0) + tail
        log_sig = jnp.minimum(raw, 0.0) - tail
        sm_ref[...] = jnp.where(lane < H_A, softplus, jnp.where(lane < H_A + H_F, log_sig, 0.0))

    sections = [(OFF_Z, D_MODEL, plain(z_ref)), (OFF_GA, D_MODEL, plain(ga_ref)), (OFF_GB, D_MODEL, plain(gb_ref)),
                (OFF_Q, H_F * HD_F, write_q), (OFF_X, CONV_DIM, plain(xbc_ref)),
                (OFF_K, KV_F * HD_F, write_kv(k_ref, kb_ref, True)),
                (OFF_V, KV_F * HD_F, write_kv(v_ref, vb_ref, False))]
    work = [(off + c * t, t, c, fn) for off, width, fn in sections for c in range(width // t)]
    work.append((OFF_S, LANE, 0, write_small))
    depth_ahead = 1
    pending = [mm(off, n) for off, n, _, _ in work[:depth_ahead]]
    for i, (_, _, c, fn) in enumerate(work):
        y = pending.pop(0)
        if i + depth_ahead < len(work):
            pending.append(mm(work[i + depth_ahead][0], work[i + depth_ahead][1]))
        fn(c, y)


def _inproj(h, mod, nw, w, layer, g256, qnw, knw, sbias, k_all, v_all, tm, tiles_per_seq, mod_rows, q_dtype,
            q_scale):
    m = h.shape[0]
    row = lambda n: pl.BlockSpec((tm, n), lambda i: (i, 0))
    const = lambda a: _resident(a.shape, lambda i: (0,) * a.ndim)
    kv = KV_F * HD_F
    kv_major = k_all.ndim == 4
    if kv_major:
        stacked = pl.BlockSpec((None, None, kv, tm), lambda i: (layer, i // tiles_per_seq, 0, i % tiles_per_seq))
    else:
        stacked = pl.BlockSpec((None, tm, kv), lambda i: (layer, i, 0))
    in_place = pl.BlockSpec(memory_space=pl.ANY)
    outs = [(D_MODEL, F32), (CONV_DIM, F32), (H_F * HD_F, q_dtype), None, (kv, BF16),
            None, (kv, BF16), (D_MODEL, F32), (D_MODEL, F32), (LANE, F32)]
    return pl.pallas_call(
        functools.partial(_inproj_kernel, q_scale=q_scale, kv_major=kv_major),
        grid=(m // tm,),
        in_specs=[row(D_MODEL),
                  pl.BlockSpec((1, D_MODEL), lambda i: (0, 0)),
                  _mod_spec(mod_rows, tm, tiles_per_seq, 3),
                  _mod_spec(mod_rows, tm, tiles_per_seq, 4),
                  _layer_resident(w, layer), const(g256), const(qnw), const(knw), const(sbias),
                  in_place, in_place],
        out_specs=[stacked if o is None else row(o[0]) for o in outs],
        out_shape=[jax.ShapeDtypeStruct(k_all.shape, F32) if o is None else jax.ShapeDtypeStruct((m, o[0]), o[1])
                   for o in outs],
        input_output_aliases={9: 3, 10: 5},
        scratch_shapes=[pltpu.VMEM((tm, D_MODEL), BF16)],
        compiler_params=_params("parallel"),
        name="inproj",
    )(h, nw, mod, mod, w, g256, qnw, knw, sbias, k_all, v_all)


def _fcum_kernel(sm_ref, kb_ref, tri_ref, pq_ref, pk_ref, oq_ref, ok_ref, fq_ref, kcat_ref, carry_ref):
    @pl.when(pl.program_id(1) == 0)
    def _():
        carry_ref[...] = jnp.zeros_like(carry_ref)

    cum = _dot01_l(tri_ref[...], sm_ref[...]) + carry_ref[0:1, :]
    carry_ref[0:1, :] = cum[cum.shape[0] - 1:, :]
    pieces = _split3(cum * LOG2E)

    def place(p_ref):
        return sum(jnp.dot(pieces[i], p_ref[i], preferred_element_type=F32) for i in range(3))

    fq_ref[...] = (place(pq_ref) + oq_ref[...]).astype(BF16)
    fk = (ok_ref[...] - place(pk_ref)).astype(BF16)
    for pair in range(KV_F // 2):
        kcat_ref[:, 2 * pair * LANE:(2 * pair + 1) * LANE] = kb_ref[:, pair * LANE:(pair + 1) * LANE]
        kcat_ref[:, (2 * pair + 1) * LANE:(2 * pair + 2) * LANE] = fk


def _fcum(small, kb, consts, bsz, s_len):
    tri = consts["tri_f"]
    t = tri.shape[0]
    nt = s_len // t
    kvw = KV_F * HD_F
    row = lambda n: pl.BlockSpec((t, n), lambda b, i: (b * nt + i, 0))
    const = lambda a: _resident(a.shape, lambda b, i: (0,) * a.ndim)
    cs = [tri, consts["place_q"], consts["place_k"], consts["ones_q"], consts["ones_k"]]
    return pl.pallas_call(
        _fcum_kernel,
        grid=(bsz, nt),
        in_specs=[row(LANE), row(kvw)] + [const(a) for a in cs],
        out_specs=[row(LANE), row(2 * kvw)],
        out_shape=[jax.ShapeDtypeStruct((bsz * s_len, LANE), BF16),
                   jax.ShapeDtypeStruct((bsz * s_len, 2 * kvw), BF16)],
        scratch_shapes=[pltpu.VMEM((8, LANE), F32)],
        compiler_params=_params("parallel", "arbitrary"),
        name="fcum",
    )(small, kb, *cs)


def _mamba_kernel(xbc_ref, z_ref, sm_ref, cprev_ref, h0_ref, cw_ref, cb_ref, alog_ref, dsk_ref, gnw_ref,
                  tri_ref, e_ref, ya_ref, cnew_ref, ssm_ref, hist_ref, dt_ref, *, rows, valid):
    q = CHUNK
    c = pl.program_id(1)
    di = D_MODEL

    @pl.when(c == 0)
    def _():
        hist_ref[0:8, :] = cprev_ref[...]
        ssm_ref[...] = h0_ref[...]
        if rows < q:
            hist_ref[8 + rows:, :] = jnp.zeros((q - rows, CONV_DIM), F32)
            dt_ref[...] = jnp.zeros_like(dt_ref)

    hist_ref[8:8 + rows, :] = xbc_ref[...]
    ext = hist_ref[...]
    conv = cb_ref[...]
    for j in range(CONV_W):
        back = CONV_W - 1 - j
        tap = ext if back == 0 else pltpu.roll(ext, back, axis=0)
        conv = conv + tap[8:8 + q, :] * cw_ref[j:j + 1, :]
    cnew_ref[...] = hist_ref[5 + valid:8 + valid, :]
    hist_ref[5:8, :] = hist_ref[5 + q:8 + q, :]
    act = conv * _sigmoid(conv)
    xs = act[:, :di]
    bmat = [act[:, di + g * N_A:di + (g + 1) * N_A].astype(BF16) for g in range(G_A)]
    cmat = [act[:, di + (G_A + g) * N_A:di + (G_A + g + 1) * N_A].astype(BF16) for g in range(G_A)]

    lane = lax.broadcasted_iota(jnp.int32, (1, LANE), 1)
    a_row = jnp.where(lane < H_A, -jnp.exp(alog_ref[...]), 0.0)
    if rows < q:
        rid = lax.broadcasted_iota(jnp.int32, (rows, LANE), 0)
        dt_ref[0:rows, :] = jnp.where(rid < valid, sm_ref[...], 0.0)
        dt = dt_ref[...]
    else:
        dt = sm_ref[...]
    tri = tri_ref[...]
    acum = _dot01_l(tri, dt * a_row)
    acum_t = acum.T
    expand = e_ref[...]
    dt_e = _dot01_r(dt, expand)
    acum_e = _dot01_r(acum, expand)
    xdt = xs * dt_e

    rq = lax.broadcasted_iota(jnp.int32, (q, q), 0)
    rs = lax.broadcasted_iota(jnp.int32, (q, q), 1)
    causal = rs <= rq
    half = lax.broadcasted_iota(jnp.int32, (q, LANE), 1) < P_A
    cb = [_dot_nt(cmat[g], bmat[g]) for g in range(G_A)]
    xdt_b = xdt.astype(BF16)
    y_parts = []
    for pair in range(H_A // 2):
        res = []
        for o in range(2):
            h = 2 * pair + o
            seg = acum[:, h:h + 1] - acum_t[h:h + 1, :]
            mix = (cb[h // (H_A // G_A)] * jnp.where(causal, jnp.exp(seg), 0.0)).astype(BF16)
            res.append(jnp.dot(mix, xdt_b[:, pair * LANE:(pair + 1) * LANE], preferred_element_type=F32))
        y_parts.append(jnp.where(half, res[0], res[1]))
    y = jnp.concatenate(y_parts, axis=1)

    hg = di // G_A
    state = ssm_ref[...]
    state_b = state.astype(BF16)
    y_off = jnp.concatenate([_dot_nt(cmat[g], state_b[g * hg:(g + 1) * hg, :]) for g in range(G_A)], axis=1)
    y = y + y_off * jnp.exp(acum_e)
    xdt_t = xdt.T
    acum_et = acum_e.T
    a_end = acum_et[:, q - 1:q]
    xdtd_t = (xdt_t * jnp.exp(a_end - acum_et)).astype(BF16)
    upd = jnp.concatenate(
        [jnp.dot(xdtd_t[g * hg:(g + 1) * hg, :], bmat[g], preferred_element_type=F32) for g in range(G_A)], axis=0)
    ssm_ref[...] = jnp.exp(a_end) * state + upd

    y = (y + dsk_ref[...] * xs)[0:rows, :]
    zz = z_ref[...]
    gated = y * (zz * _sigmoid(zz))
    out = gated * lax.rsqrt(jnp.mean(gated * gated, axis=-1, keepdims=True) + EPS) * gnw_ref[...]
    ya_ref[...] = out.astype(ya_ref.dtype)


def _mamba(xbc, z, small, cprev8, h0, layer, cw8, cb, alog, dsk, gnw, tri, expand, nseq, nchunk, rows, valid,
           ya_dtype):
    m = xbc.shape[0]
    row = lambda n: pl.BlockSpec((rows, n), lambda b, c: (b * nchunk + c, 0))
    const = lambda a: _resident(a.shape, lambda b, c: (0,) * a.ndim)
    di = D_MODEL
    kern = functools.partial(_mamba_kernel, rows=rows, valid=valid)
    return pl.pallas_call(
        kern,
        grid=(nseq, nchunk),
        in_specs=[row(CONV_DIM), row(di), row(LANE),
                  pl.BlockSpec((None, 8, CONV_DIM), lambda b, c: (b, 0, 0)),
                  pl.BlockSpec((None, None, di, N_A), lambda b, c: (layer, b, 0, 0)),
                  const(cw8), const(cb), const(alog), const(dsk), const(gnw), const(tri), const(expand)],
        out_specs=[row(di),
                   pl.BlockSpec((None, CONV_W - 1, CONV_DIM), lambda b, c: (b, 0, 0)),
                   pl.BlockSpec((None, di, N_A), lambda b, c: (b, 0, 0))],
        out_shape=[jax.ShapeDtypeStruct((m, di), ya_dtype),
                   jax.ShapeDtypeStruct((nseq, CONV_W - 1, CONV_DIM), F32),
                   jax.ShapeDtypeStruct((nseq, di, N_A), F32)],
        scratch_shapes=[pltpu.VMEM((8 + CHUNK, CONV_DIM), F32), pltpu.VMEM((CHUNK, LANE), F32)],
        compiler_params=_params("parallel", "arbitrary"),
        name="mamba",
    )(xbc, z, small, cprev8, h0, cw8, cb, alog, dsk, gnw, tri, expand)


def _attn_kernel(q_ref, fq_ref, k_ref, v_ref, o_ref, qcat_ref, m_ref, l_ref, acc_ref, *, tq, tk):
    qi = pl.program_id(1)
    ki = pl.program_id(2)
    nk = pl.num_programs(2)

    @pl.when(ki == 0)
    def _():
        m_ref[...] = jnp.full_like(m_ref, NEG)
        l_ref[...] = jnp.zeros_like(l_ref)
        acc_ref[...] = jnp.zeros_like(acc_ref)
        lane = lax.broadcasted_iota(jnp.int32, (tq, LANE), 1)
        lo = lane < HD_F
        fq = fq_ref[...]
        for blk in range(H_F // 2):
            qb = q_ref[:, blk * LANE:(blk + 1) * LANE]
            for o in range(2):
                slot = 2 * blk + o
                hq = HEAD_PERM[slot]
                own = jnp.logical_and(lane >= F_LANES * hq, lane < F_LANES * (hq + 1))
                qcat_ref[slot, :, 0:LANE] = jnp.where(lo if o == 0 else jnp.logical_not(lo), qb, jnp.zeros_like(qb))
                qcat_ref[slot, :, LANE:2 * LANE] = jnp.where(own, fq, jnp.zeros_like(fq))

    tn = (((0,), (0,)), ((), ()))

    def full_step():
        def scores(slot):
            pair = slot // 4
            return _dot_nt(k_ref[:, 2 * pair * LANE:(2 * pair + 2) * LANE], qcat_ref[slot])

        ahead = [scores(0), scores(1)]
        for slot in range(H_F):
            pair = slot // 4
            s = ahead.pop(0)
            if slot + 2 < H_F:
                ahead.append(scores(slot + 2))
            m_old = m_ref[slot]
            m_new = jnp.maximum(m_old, jnp.max(s, axis=0, keepdims=True))
            alpha = jnp.exp2(m_old - m_new)
            p = jnp.exp2(s - m_new)
            l_ref[slot] = alpha * l_ref[slot] + jnp.sum(p, axis=0, keepdims=True)
            m_ref[slot] = m_new
            pv = lax.dot_general(v_ref[:, pair * LANE:(pair + 1) * LANE], p.astype(BF16), tn,
                                 preferred_element_type=F32)
            acc_ref[slot] = alpha * acc_ref[slot] + pv

    def diagonal_step():
        hk, hq = tk // 2, tq // 2
        causal_a = (lax.broadcasted_iota(jnp.int32, (hk, tq), 0) <= lax.broadcasted_iota(jnp.int32, (hk, tq), 1))
        causal_b = (lax.broadcasted_iota(jnp.int32, (hk, hq), 0) <= lax.broadcasted_iota(jnp.int32, (hk, hq), 1))
        def scores(slot):
            kl = slice(2 * (slot // 4) * LANE, (2 * (slot // 4) + 2) * LANE)
            return (_dot_nt(k_ref[0:hk, kl], qcat_ref[slot]),
                    _dot_nt(k_ref[hk:tk, kl], qcat_ref[slot, hq:tq, :]))

        ahead = [scores(0), scores(1)]
        for slot in range(H_F):
            pair = slot // 4
            vl = slice(pair * LANE, (pair + 1) * LANE)
            s_a, s_b = ahead.pop(0)
            if slot + 2 < H_F:
                ahead.append(scores(slot + 2))
            s_a = jnp.where(causal_a, s_a, NEG)
            s_b = jnp.where(causal_b, s_b, NEG)
            m_old = m_ref[slot]
            m_a = jnp.max(s_a, axis=0, keepdims=True)
            m_b = jnp.max(s_b, axis=0, keepdims=True)
            m_new = jnp.maximum(m_old, jnp.concatenate([m_a[:, 0:hq], jnp.maximum(m_a[:, hq:tq], m_b)], axis=1))
            alpha = jnp.exp2(m_old - m_new)
            p_a = jnp.exp2(s_a - m_new)
            p_b = jnp.exp2(s_b - m_new[:, hq:tq])
            l_add = jnp.sum(p_a, axis=0, keepdims=True)
            l_b = jnp.sum(p_b, axis=0, keepdims=True)
            l_ref[slot] = alpha * l_ref[slot] + jnp.concatenate([l_add[:, 0:hq], l_add[:, hq:tq] + l_b], axis=1)
            m_ref[slot] = m_new
            pv_a = lax.dot_general(v_ref[0:hk, vl], p_a.astype(BF16), tn, preferred_element_type=F32)
            pv_b = lax.dot_general(v_ref[hk:tk, vl], p_b.astype(BF16), tn, preferred_element_type=F32)
            acc_ref[slot] = alpha * acc_ref[slot] + jnp.concatenate(
                [pv_a[:, 0:hq], pv_a[:, hq:tq] + pv_b], axis=1)

    pl.when(ki < qi)(full_step)
    pl.when(ki == qi)(diagonal_step)

    @pl.when(ki == nk - 1)
    def _():
        lo = lax.broadcasted_iota(jnp.int32, (LANE, tq), 0) < HD_F
        for blk in range(H_F // 2):
            a0 = acc_ref[2 * blk] / l_ref[2 * blk]
            a1 = acc_ref[2 * blk + 1] / l_ref[2 * blk + 1]
            o_ref[:, blk * LANE:(blk + 1) * LANE] = jnp.where(lo, a0, a1).T.astype(o_ref.dtype)


def _attn(qb, fq, kcat, vb, bsz, s_len, tq):
    tk = tq
    nq = s_len // tq
    kern = functools.partial(_attn_kernel, tq=tq, tk=tk)
    kvw = KV_F * HD_F
    q_spec = lambda n: pl.BlockSpec((tq, n), lambda b, i, j: (b * nq + i, 0))
    k_spec = lambda n: pl.BlockSpec((tk, n), lambda b, i, j: (b * nq + jnp.minimum(i, j), 0))
    return pl.pallas_call(
        kern,
        grid=(bsz, nq, nq),
        in_specs=[q_spec(D_MODEL), q_spec(LANE), k_spec(2 * kvw), k_spec(kvw)],
        out_specs=q_spec(D_MODEL),
        out_shape=jax.ShapeDtypeStruct((bsz * s_len, D_MODEL), BF16),
        scratch_shapes=[pltpu.VMEM((H_F, tq, 2 * LANE), BF16),
                        pltpu.VMEM((H_F, 1, tq), F32), pltpu.VMEM((H_F, 1, tq), F32),
                        pltpu.VMEM((H_F, LANE, tq), F32)],
        compiler_params=_params("parallel", "parallel", "arbitrary"),
        name="attn",
    )(qb, fq, kcat, vb)


def _dec_kernel(pt_ref, q_ref, kn_ref, vn_ref, sm_ref, *rest, npp, tokens, valid):
    k_refs = rest[0:npp]
    v_refs = rest[npp:2 * npp]
    f_refs = rest[2 * npp:3 * npp]
    (et128_ref, tri_ref, slt_ref, o_ref, smp_ref, m_ref, l_ref, gq_ref, carry_ref, acc_ref) = rest[3 * npp:]
    g = pl.program_id(1)
    ng = pl.num_programs(1)
    kvw = KV_F * HD_F
    qbd = q_ref[...].astype(BF16)

    def update(scores, values_t):
        m_old = m_ref[...]
        m_new = jnp.maximum(m_old, jnp.max(functools.reduce(jnp.maximum, scores), axis=1, keepdims=True))
        alpha = jnp.exp(m_old - m_new)
        p_sum = jnp.zeros(scores[0].shape, F32)
        pv = jnp.zeros((LANE, kvw), F32)
        for s, vt in zip(scores, values_t):
            p = jnp.exp(s - m_new)
            p_sum = p_sum + p
            pv = pv + _dot_nt(p.astype(BF16), vt)
        l_ref[...] = alpha * l_ref[...] + jnp.sum(p_sum, axis=1, keepdims=True)
        m_ref[...] = m_new
        acc_ref[...] = alpha * acc_ref[...] + pv

    @pl.when(g == 0)
    def _():
        m_ref[...] = jnp.full_like(m_ref, NEG)
        l_ref[...] = jnp.zeros_like(l_ref)
        carry_ref[...] = jnp.zeros_like(carry_ref)
        acc_ref[...] = jnp.zeros_like(acc_ref)
        smp_ref[...] = jnp.zeros_like(smp_ref)
        smp_ref[0:tokens, :] = sm_ref[...]
        cum_t = _dot01_l(tri_ref[...], smp_ref[...]).T
        cum_e = _dot01_l(et128_ref[...], cum_t)
        key = lax.broadcasted_iota(jnp.int32, (LANE, PAGE), 1)
        qtok = lax.broadcasted_iota(jnp.int32, (LANE, PAGE), 0) % tokens
        g_q = jnp.sum(jnp.where(key == qtok, cum_e, 0.0), axis=1, keepdims=True)
        gq_ref[...] = g_q
        ok = jnp.logical_and(key <= qtok, key < valid)
        s = jnp.dot(qbd, kn_ref[...].astype(BF16), preferred_element_type=F32) + (g_q - cum_e)
        update([jnp.where(ok, s, NEG)], [vn_ref[...].astype(BF16)])

    lf = jnp.concatenate([f_refs[i][...] for i in range(npp)], axis=0)
    later = _dot01_r(lf, slt_ref[...])
    mass = later[:, 0:1] + lf[:, 0:1]
    carry = carry_ref[...]
    g_q = gq_ref[...]
    biases = [None] * npp
    for i in reversed(range(npp)):
        per_head = later[i * H_F:(i + 1) * H_F, :] + carry
        biases[i] = jnp.concatenate(
            [jnp.broadcast_to(per_head[h:h + 1, :], (tokens, PAGE)) for h in range(H_F)], axis=0) + g_q
        carry = carry + mass[i * H_F:(i + 1) * H_F, :]
    carry_ref[...] = carry
    group = 2 if npp % 2 == 0 else 1
    page_t = lambda refs, i: refs[i][...].reshape(kvw, PAGE).astype(BF16)
    side_by_side = lambda parts: parts[0] if len(parts) == 1 else jnp.concatenate(parts, axis=1)
    scores, values_t = [], []
    for i0 in range(0, npp, group):
        ids = range(i0, i0 + group)
        kt = side_by_side([page_t(k_refs, i) for i in ids])
        scores.append(jnp.dot(qbd, kt, preferred_element_type=F32) + side_by_side([biases[i] for i in ids]))
        values_t.append(side_by_side([page_t(v_refs, i) for i in ids]))
    update(scores, values_t)

    @pl.when(g == ng - 1)
    def _():
        o_ref[...] = acc_ref[...] / l_ref[...]


def _dec_attn(page_table, qbd, kn_t, vn_t, small, cache_kt, cache_vt, cache_ft, layer, dc, tokens, valid, npp):
    nseq, n_pages = page_table.shape
    ng = n_pages // npp
    kvw = KV_F * HD_F
    rows = H_F * tokens
    kern = functools.partial(_dec_kernel, npp=npp, tokens=tokens, valid=valid)
    per_seq = lambda r, c: pl.BlockSpec((None, r, c), lambda b, g, pt: (b, 0, 0))
    page_idx = lambda b, g, pt, i: pt[b, (ng - 1 - g) * npp + i]
    kv_page = lambda i: pl.BlockSpec((None, None, KV_F, HD_F, PAGE),
                                     lambda b, g, pt: (layer, page_idx(b, g, pt, i), 0, 0, 0))
    f_page = lambda i: pl.BlockSpec((None, None, H_F, PAGE), lambda b, g, pt: (layer, page_idx(b, g, pt, i), 0, 0))
    const = lambda a: _resident(a.shape, lambda b, g, pt: (0,) * a.ndim)
    cs = [dc["et128"], dc["tri"], dc["slt"]]
    in_specs = ([per_seq(rows, kvw), per_seq(kvw, PAGE), per_seq(kvw, PAGE),
                 pl.BlockSpec((tokens, LANE), lambda b, g, pt: (b, 0))]
                + [kv_page(i) for i in range(npp)] + [kv_page(i) for i in range(npp)]
                + [f_page(i) for i in range(npp)] + [const(a) for a in cs])
    col = pltpu.VMEM((rows, 1), F32)
    grid_spec = pltpu.PrefetchScalarGridSpec(
        num_scalar_prefetch=1,
        grid=(nseq, ng),
        in_specs=in_specs,
        out_specs=per_seq(rows, kvw),
        scratch_shapes=[pltpu.VMEM((PAGE, LANE), F32), col, col, col, pltpu.VMEM((H_F, 1), F32),
                        pltpu.VMEM((rows, kvw), F32)])
    return pl.pallas_call(
        kern,
        grid_spec=grid_spec,
        out_shape=jax.ShapeDtypeStruct((nseq, rows, kvw), F32),
        compiler_params=_params("parallel", "arbitrary"),
        name="dec_attn",
    )(page_table, qbd, kn_t, vn_t, small, *([cache_kt] * npp), *([cache_vt] * npp), *([cache_ft] * npp), *cs)


def _merge_kernel(h_ref, ya_ref, yb_ref, ga_ref, gb_ref, gt_ref, wa_ref, wb_ref, wo_ref, o_ref):
    tm = h_ref.shape[0]
    halves = [slice(0, tm // 2), slice(tm // 2, tm)] if tm % 16 == 0 else [slice(0, tm)]
    proj = [(jnp.dot(ya_ref[r, :].astype(BF16), wa_ref[...], preferred_element_type=F32),
             jnp.dot(yb_ref[r, :].astype(BF16), wb_ref[...], preferred_element_type=F32)) for r in halves]
    for r, (pa, pb) in zip(halves, proj):
        merged = _sigmoid(ga_ref[r, :]) * pa + _sigmoid(gb_ref[r, :]) * pb
        out = jnp.dot(merged.astype(BF16), wo_ref[...], preferred_element_type=F32)
        gate = gt_ref[...] if gt_ref.shape[0] == 1 else gt_ref[r, :]
        o_ref[r, :] = h_ref[r, :] + gate * out


def _merge(h, ya, yb, ga, gb, mod, wa, wb, wo, layer, tm, tiles_per_seq, mod_rows):
    m = h.shape[0]
    row = pl.BlockSpec((tm, D_MODEL), lambda i: (i, 0))
    return pl.pallas_call(
        _merge_kernel,
        grid=(m // tm,),
        in_specs=[row, row, row, row, row, _mod_spec(mod_rows, tm, tiles_per_seq, 5),
                  _layer_resident(wa, layer), _layer_resident(wb, layer), _layer_resident(wo, layer)],
        out_specs=row,
        out_shape=jax.ShapeDtypeStruct((m, D_MODEL), F32),
        compiler_params=_params("parallel"),
        name="merge",
    )(h, ya, yb, ga, gb, mod, wa, wb, wo)


def _constants():
    f = np.float32
    head = np.arange(D_MODEL) // HD_F
    g256 = (head[:256, None] == head[None, :256]).astype(f)
    tri128 = np.tril(np.ones((CHUNK, CHUNK), f))
    tri_f = np.tril(np.ones((FCUM_TILE, FCUM_TILE), f))
    striu = np.triu(np.ones((PAGE, PAGE), f), 1)
    expand = np.zeros((LANE, D_MODEL), f)
    expand[head, np.arange(D_MODEL)] = 1.0
    place_q = np.zeros((3, LANE, LANE), f)
    place_k = np.zeros((3, LANE, LANE), f)
    ones_q = np.zeros((1, LANE), f)
    ones_k = np.zeros((1, LANE), f)
    for h in range(H_F):
        for i in range(3):
            place_q[i, H_A + h, F_LANES * h + i] = 1.0
            place_k[i, H_A + h, F_LANES * h + 3 + i] = 1.0
            ones_q[0, F_LANES * h + 3 + i] = 1.0
            ones_k[0, F_LANES * h + i] = 1.0
    out = {k: jnp.asarray(v, BF16) for k, v in
           dict(g256=g256, tri128=tri128, tri_f=tri_f, striu=striu, expand=expand,
                place_q=place_q, place_k=place_k).items()}
    out.update(ones_q=jnp.asarray(ones_q), ones_k=jnp.asarray(ones_k))
    return out


def _dec_constants(tokens, consts):
    f = np.float32
    et128 = np.zeros((H_F * tokens, LANE), f)
    for hq in range(H_F):
        et128[hq * tokens:(hq + 1) * tokens, H_A + hq] = 1.0
    slt = np.tril(np.ones((PAGE, PAGE), f), -1)
    return dict(et128=jnp.asarray(et128, BF16), tri=consts["tri128"], slt=jnp.asarray(slt, BF16))


def _inproj_weights(w_in):
    depth = w_in.shape[0]
    sizes = (D_MODEL, CONV_DIM, H_A, H_F * HD_F, KV_F * HD_F, KV_F * HD_F, H_F, D_MODEL, D_MODEL)
    pts = np.cumsum(sizes)[:-1].tolist()
    wz, wx, wdt, wq, wk, wv, wf, wga, wgb = jnp.split(w_in.astype(BF16), pts, axis=-1)
    wq = wq.reshape(depth, D_MODEL, H_F, HD_F)[:, :, list(HEAD_PERM), :].reshape(depth, D_MODEL, H_F * HD_F)
    pad = jnp.zeros((depth, D_MODEL, LANE - H_A - H_F), BF16)
    return jnp.concatenate([wz, wx, wq, wk, wv, wga, wgb, wdt, wf, pad], axis=-1)


def _run_group(x2d, mod, mod_rows, tiles, conv_prev, ssm_prev, lw, consts, fox, nseq, nchunk, rows, valid,
               q_dtype, q_scale, y_dtype, kv_major):
    tm_ffn, tm_in, tm_mg, tps_ffn, tps_in, tps_mg = tiles
    h = x2d
    states = []
    depth = lw["w_in"].shape[0]
    kv_shape = ((depth, nseq, KV_F * HD_F, x2d.shape[0] // nseq) if kv_major
                else (depth, x2d.shape[0], KV_F * HD_F))
    k_all = jnp.zeros(kv_shape, F32)
    v_all = jnp.zeros(kv_shape, F32)
    for l in range(depth):
        ml = mod[l]
        h = _ffn(h, ml, 0, lw["nw"][l, 0:1], lw["wi1"], lw["wo1"], l, tm_ffn, tps_ffn, mod_rows)
        z, xbc, q, k_all, kb, v_all, vb, ga, gb, small = _inproj(
            h, ml, lw["nw"][l, 1:2], lw["w_in"], l, consts["g256"], lw["qnw"][l], lw["knw"][l], lw["sbias"][l],
            k_all, v_all, tm_in, tps_in, mod_rows, q_dtype, q_scale)
        cprev8 = jnp.pad(conv_prev[l], ((0, 0), (8 - (CONV_W - 1), 0), (0, 0)))
        ya, conv_new, ssm_new = _mamba(
            xbc, z, small, cprev8, ssm_prev, l, lw["cw8"][l], lw["cb"][l], lw["alog"][l], lw["dsk"][l], lw["gnw"][l],
            consts["tri128"], consts["expand"], nseq, nchunk, rows, valid, y_dtype)
        yb = fox(l, q, k_all, kb, v_all, vb, small)
        h = _merge(h, ya, yb, ga, gb, ml, lw["wa"], lw["wb"], lw["wout"], l, tm_mg, tps_mg, mod_rows)
        h = _ffn(h, ml, 6, lw["nw"][l, 2:3], lw["wi2"], lw["wo2"], l, tm_ffn, tps_ffn, mod_rows)
        states.append((small, conv_new, ssm_new))
    return h, k_all, v_all, states


def kernel(x_prompt, x_sample, cache_k, cache_v, cache_logf, state_conv, state_ssm, page_table,
           c_prompt, c_sample, w_ada, b_ada, norm_w, w_ffn1_in, w_ffn1_out, w_ffn2_in, w_ffn2_out,
           w_in, conv_w, conv_b, dt_bias, a_log, d_skip, gn_w, q_norm_w, k_norm_w, b_f,
           w_proj_a, w_proj_b, w_out):
    bp, s_len, d = x_prompt.shape
    db, t_len, _ = x_sample.shape
    depth = w_in.shape[0]
    n_pool = cache_k.shape[1]
    kvw = KV_F * HD_F
    assert d == D_MODEL and s_len % CHUNK == 0 and t_len <= 8 and cache_k.shape[2] == PAGE
    consts = _constants()

    pad_lanes = lambda a: jnp.pad(a, ((0, 0), (0, LANE - a.shape[1])))[:, None, :]
    lw = dict(
        nw=norm_w,
        wi1=w_ffn1_in.astype(BF16), wo1=w_ffn1_out.astype(BF16),
        wi2=w_ffn2_in.astype(BF16), wo2=w_ffn2_out.astype(BF16),
        w_in=_inproj_weights(w_in),
        qnw=jnp.tile(q_norm_w, (1, 256 // HD_F))[:, None, :],
        knw=jnp.tile(k_norm_w, (1, 256 // HD_F))[:, None, :],
        sbias=pad_lanes(jnp.concatenate([dt_bias, b_f], axis=1)),
        cw8=jnp.pad(conv_w, ((0, 0), (0, 8 - CONV_W), (0, 0))),
        cb=conv_b[:, None, :],
        alog=pad_lanes(a_log),
        dsk=jnp.repeat(d_skip, P_A, axis=1)[:, None, :],
        gnw=gn_w[:, None, :],
        wa=w_proj_a.astype(BF16),
        wb=w_proj_b.reshape(depth, H_F, HD_F, d)[:, list(HEAD_PERM)].reshape(depth, d, d).astype(BF16),
        wout=w_out.astype(BF16),
    )

    mod_all = _ada(jnp.concatenate([c_prompt, c_sample], axis=0), w_ada, b_ada)
    mod_all = mod_all.reshape(depth, bp + db, N_MOD, d)
    mod_p = mod_all[:, :bp, :, None, :]
    tok = 8
    mod_s = jnp.broadcast_to(mod_all[:, bp:, None], (depth, db, tok, N_MOD, d))
    mod_s = mod_s.reshape(depth, db * tok, N_MOD, d).transpose(0, 2, 1, 3)

    tq = min(512, s_len)

    def fox_p(l, q, k, kb, v, vb, small):
        fq, kcat = _fcum(small, kb, consts, bp, s_len)
        return _attn(q, fq, kcat, vb, bp, s_len, tq)

    tm = min(512, s_len)
    tm_in = min(512, s_len)
    tiles_p = (tm, tm_in, tm, s_len // tm, s_len // tm_in, s_len // tm)
    conv0 = jnp.zeros((depth, bp, CONV_W - 1, CONV_DIM), F32)
    ssm0 = jnp.zeros((depth, bp, D_MODEL, N_A), F32)
    y_p, kp_all, vp_all, st_p = _run_group(x_prompt.reshape(bp * s_len, d), mod_p, 1, tiles_p, conv0, ssm0, lw, consts, fox_p,
                           bp, s_len // CHUNK, CHUNK, CHUNK, BF16, ATTN_SCALE * LOG2E, BF16, True)

    dc = _dec_constants(tok, consts)
    npp = next(n for n in (16, 8, 4, 2, 1) if page_table.shape[1] % n == 0)
    head_inv = list(np.argsort(HEAD_PERM))

    cache_kt = jnp.transpose(cache_k, (0, 1, 3, 4, 2))
    cache_vt = jnp.transpose(cache_v, (0, 1, 3, 4, 2))
    cache_ft = jnp.transpose(cache_logf, (0, 1, 3, 2))
    kv_of_head = np.arange(H_F) // (H_F // KV_F)
    own_kv = jnp.asarray(kv_of_head[:, None] == np.arange(KV_F)[None, :], F32)
    assert H_F * tok == LANE

    def fox_s(l, q, k, kb, v, vb, small):
        q2 = q.reshape(db, tok, H_F, HD_F)[:, :, head_inv].transpose(0, 2, 1, 3)
        qbd = (q2[:, :, :, None, :] * own_kv[None, :, None, :, None]).reshape(db, H_F * tok, kvw)
        new_t = lambda a: jnp.pad(a.reshape(db, tok, kvw).transpose(0, 2, 1), ((0, 0), (0, 0), (0, PAGE - tok)))
        out = _dec_attn(page_table, qbd, new_t(k[l]), new_t(v[l]), small, cache_kt, cache_vt, cache_ft, l, dc,
                        tok, t_len, npp)
        out = out.reshape(db, H_F, tok, KV_F, HD_F)
        pick = jnp.broadcast_to(jnp.asarray(kv_of_head)[None, :, None, None, None], (db, H_F, tok, 1, HD_F))
        out = jnp.take_along_axis(out, pick, axis=3)[:, :, :, 0]
        return out[:, list(HEAD_PERM)].transpose(0, 2, 1, 3).reshape(db * tok, d)

    ms = db * tok
    tiles_s = (ms, ms, ms, 1, 1, 1)
    x_s = jnp.pad(x_sample, ((0, 0), (0, tok - t_len), (0, 0))).reshape(ms, d)
    y_s, ks_all, vs_all, st_s = _run_group(x_s, mod_s, ms, tiles_s, state_conv, state_ssm.reshape(depth, db, D_MODEL, N_A), lw, consts,
                           fox_s, db, 1, tok, t_len, F32, ATTN_SCALE, F32, False)

    def gather(k_all, v_all, states, nseq, length, keep):
        if k_all.ndim == 4:
            k = k_all.reshape(depth, nseq, KV_F, HD_F, length).transpose(0, 1, 4, 2, 3)
            v = v_all.reshape(depth, nseq, KV_F, HD_F, length).transpose(0, 1, 4, 2, 3)
        else:
            k = k_all.reshape(depth, nseq, length, KV_F, HD_F)[:, :, :keep]
            v = v_all.reshape(depth, nseq, length, KV_F, HD_F)[:, :, :keep]
        lf = jnp.stack([s[0][:, H_A:H_A + H_F] for s in states]).reshape(depth, nseq, length, H_F)[:, :, :keep]
        conv = jnp.stack([s[1] for s in states])
        ssm = jnp.stack([s[2] for s in states]).reshape(depth, nseq, H_A, P_A, N_A)
        return k, v, lf, conv, ssm

    k_p, v_p, lf_p, conv_p, ssm_p = gather(kp_all, vp_all, st_p, bp, s_len, s_len)
    k_s, v_s, lf_s, conv_s, ssm_s = gather(ks_all, vs_all, st_s, db, tok, t_len)
    y_prompt = y_p.reshape(bp, s_len, d)
    y_sample = y_s.reshape(db, tok, d)[:, :t_len]
    return (y_prompt, y_sample, k_p, v_p, lf_p, conv_p, ssm_p, k_s, v_s, lf_s, conv_s, ssm_s)
```

```python
import functools

import numpy as np
import jax
import jax.numpy as jnp
from jax import lax
from jax.experimental import pallas as pl
from jax.experimental.pallas import tpu as pltpu

F32 = jnp.float32
BF16 = jnp.bfloat16

D_MODEL = 1024
D_FF = 2816
P_A = 64
H_A = 16
G_A = 2
N_A = 128
CONV_W = 4
CONV_DIM = D_MODEL + 2 * G_A * N_A
HD_F = 64
H_F = 16
KV_F = 8
N_MOD = 9
RESID_HALF = 0.5
EPS = 1e-6
ATTN_SCALE = HD_F ** -0.5
PAGE = 128
CHUNK = 128
NEG = -1e30
LOG2E = 1.4426950408889634
F_LANES = 6
DEC_ROWS = PAGE * KV_F
FCUM_TILE = 512

LANE = 128
FF_TILE = 256
OFF_Z = 0
OFF_X = OFF_Z + D_MODEL
OFF_Q = OFF_X + CONV_DIM
OFF_K = OFF_Q + H_F * HD_F
OFF_V = OFF_K + KV_F * HD_F
OFF_GA = OFF_V + KV_F * HD_F
OFF_GB = OFF_GA + D_MODEL
OFF_S = OFF_GB + D_MODEL
IN_W = OFF_S + LANE
HEAD_PERM = tuple(4 * j + o for j in range(4) for o in (0, 2, 1, 3))
VMEM_LIMIT = 56 * 1024 * 1024


def _params(*sem):
    return pltpu.CompilerParams(dimension_semantics=sem, vmem_limit_bytes=VMEM_LIMIT)


def _resident(shape, index_map):
    return pl.BlockSpec(shape, index_map, pipeline_mode=pl.Buffered(1))


def _sigmoid(x):
    return 1.0 / (1.0 + jnp.exp(-x))


def _norm_mod(x, nw, shift, scale):
    y = x * lax.rsqrt(jnp.mean(x * x, axis=-1, keepdims=True) + EPS) * nw
    return y * (1.0 + scale) + shift


def _split3(x):
    hi = x.astype(BF16)
    r = x - hi.astype(F32)
    mid = r.astype(BF16)
    lo = (r - mid.astype(F32)).astype(BF16)
    return hi, mid, lo


def _dot01_l(m01, x):
    hi, mid, lo = _split3(x)
    d = lambda p: jnp.dot(m01, p, preferred_element_type=F32)
    return d(hi) + d(mid) + d(lo)


def _dot01_r(x, m01):
    hi, mid, lo = _split3(x)
    d = lambda p: jnp.dot(p, m01, preferred_element_type=F32)
    return d(hi) + d(mid) + d(lo)


def _dot_nt(a, b):
    return lax.dot_general(a, b, (((1,), (1,)), ((), ())), preferred_element_type=F32)


def _ada_kernel(c_ref, w_ref, b_ref, o_ref):
    c = c_ref[...]
    a = (c * _sigmoid(c)).astype(BF16)
    o_ref[...] = jnp.dot(a, w_ref[...].astype(BF16), preferred_element_type=F32) + b_ref[...]


def _ada(c_all, w_ada, b_ada):
    depth, d, n = w_ada.shape
    rows = c_all.shape[0]
    tn = 1024
    return pl.pallas_call(
        _ada_kernel,
        grid=(depth, n // tn),
        in_specs=[pl.BlockSpec((rows, d), lambda l, j: (0, 0)),
                  pl.BlockSpec((None, d, tn), lambda l, j: (l, 0, j)),
                  pl.BlockSpec((None, 1, tn), lambda l, j: (l, 0, j))],
        out_specs=pl.BlockSpec((None, rows, tn), lambda l, j: (l, 0, j)),
        out_shape=jax.ShapeDtypeStruct((depth, rows, n), F32),
        compiler_params=_params("parallel", "parallel"),
        name="ada",
    )(c_all, w_ada, b_ada.reshape(depth, 1, n))


def _ffn_kernel(x_ref, nw_ref, sh_ref, sc_ref, gt_ref, wi_ref, wo_ref, o_ref, u_ref, acc_ref):
    x = x_ref[...]
    u_ref[...] = _norm_mod(x, nw_ref[...], sh_ref[...], sc_ref[...]).astype(BF16)
    for c in range(D_FF // FF_TILE):
        u = u_ref[...]
        lo = c * FF_TILE
        g = jnp.dot(u, wi_ref[:, lo:lo + FF_TILE], preferred_element_type=F32)
        up = jnp.dot(u, wi_ref[:, D_FF + lo:D_FF + lo + FF_TILE], preferred_element_type=F32)
        act = ((g * _sigmoid(g)) * up).astype(BF16)
        part = jnp.dot(act, wo_ref[lo:lo + FF_TILE, :], preferred_element_type=F32)
        if c == 0:
            acc_ref[...] = part
        else:
            acc_ref[...] += part
    o_ref[...] = x + (RESID_HALF * gt_ref[...]) * acc_ref[...]


def _mod_spec(mod_rows, tm, tiles_per_seq, k):
    if mod_rows == 1:
        return pl.BlockSpec((None, None, 1, D_MODEL), lambda i: (i // tiles_per_seq, k, 0, 0))
    return pl.BlockSpec((None, tm, D_MODEL), lambda i: (k, i, 0))


def _layer_resident(a, layer):
    return _resident((None,) + a.shape[1:], lambda *_: (layer,) + (0,) * (a.ndim - 1))


def _ffn(h, mod, kmod, nw, wi, wo, layer, tm, tiles_per_seq, mod_rows):
    m = h.shape[0]
    row = pl.BlockSpec((tm, D_MODEL), lambda i: (i, 0))
    return pl.pallas_call(
        _ffn_kernel,
        grid=(m // tm,),
        in_specs=[row,
                  pl.BlockSpec((1, D_MODEL), lambda i: (0, 0)),
                  _mod_spec(mod_rows, tm, tiles_per_seq, kmod),
                  _mod_spec(mod_rows, tm, tiles_per_seq, kmod + 1),
                  _mod_spec(mod_rows, tm, tiles_per_seq, kmod + 2),
                  _layer_resident(wi, layer), _layer_resident(wo, layer)],
        out_specs=row,
        out_shape=jax.ShapeDtypeStruct((m, D_MODEL), F32),
        scratch_shapes=[pltpu.VMEM((tm, D_MODEL), BF16), pltpu.VMEM((tm, D_MODEL), F32)],
        compiler_params=_params("parallel"),
        name="ffn",
    )(h, nw, mod, mod, mod, wi, wo)


def _head_norm(x, g256, w):
    sq = x * x
    hi = sq.astype(BF16)
    lo = (sq - hi.astype(F32)).astype(BF16)
    ss = jnp.dot(hi, g256, preferred_element_type=F32) + jnp.dot(lo, g256, preferred_element_type=F32)
    return x * lax.rsqrt(ss * (1.0 / HD_F) + EPS) * w


def _inproj_kernel(x_ref, nw_ref, sh_ref, sc_ref, w_ref, g_ref, qnw_ref, knw_ref, sb_ref, k_all_ref, v_all_ref,
                   z_ref, xbc_ref, q_ref, k_ref, kb_ref, v_ref, vb_ref, ga_ref, gb_ref, sm_ref, u_ref, *, q_scale,
                   kv_major):
    u_ref[...] = _norm_mod(x_ref[...], nw_ref[...], sh_ref[...], sc_ref[...]).astype(BF16)

    def mm(lo, n):
        return jnp.dot(u_ref[...], w_ref[:, lo:lo + n], preferred_element_type=F32)

    t = 256

    def plain(ref):
        def write(c, y):
            ref[:, c * t:(c + 1) * t] = y
        return write

    def write_q(c, y):
        qn = _head_norm(y, g_ref[...], qnw_ref[...])
        q_ref[:, c * t:(c + 1) * t] = (qn * q_scale).astype(q_ref.dtype)

    def write_kv(ref, ref_b, norm):
        def write(c, y):
            if norm:
                y = _head_norm(y, g_ref[...], knw_ref[...])
            ref_b[:, c * t:(c + 1) * t] = y.astype(BF16)
            if kv_major:
                ref[c * t:(c + 1) * t, :] = y.T
            else:
                ref[:, c * t:(c + 1) * t] = y
        return write

    def write_small(c, y):
        raw = y + sb_ref[...]
        tail = jnp.log1p(jnp.exp(-jnp.abs(raw)))
        lane = lax.broadcasted_iota(jnp.int32, raw.shape, 1)
        softplus = jnp.maximum(raw, 0.0) + tail
        log_sig = jnp.minimum(raw, 0.0) - tail
        sm_ref[...] = jnp.where(lane < H_A, softplus, jnp.where(lane < H_A + H_F, log_sig, 0.0))

    sections = [(OFF_Z, D_MODEL, plain(z_ref)), (OFF_GA, D_MODEL, plain(ga_ref)), (OFF_GB, D_MODEL, plain(gb_ref)),
                (OFF_Q, H_F * HD_F, write_q), (OFF_X, CONV_DIM, plain(xbc_ref)),
                (OFF_K, KV_F * HD_F, write_kv(k_ref, kb_ref, True)),
                (OFF_V, KV_F * HD_F, write_kv(v_ref, vb_ref, False))]
    work = [(off + c * t, t, c, fn) for off, width, fn in sections for c in range(width // t)]
    work.append((OFF_S, LANE, 0, write_small))
    depth_ahead = 1
    pending = [mm(off, n) for off, n, _, _ in work[:depth_ahead]]
    for i, (_, _, c, fn) in enumerate(work):
        y = pending.pop(0)
        if i + depth_ahead < len(work):
            pending.append(mm(work[i + depth_ahead][0], work[i + depth_ahead][1]))
        fn(c, y)


def _inproj(h, mod, nw, w, layer, g256, qnw, knw, sbias, k_all, v_all, tm, tiles_per_seq, mod_rows, q_dtype,
            q_scale):
    m = h.shape[0]
    row = lambda n: pl.BlockSpec((tm, n), lambda i: (i, 0))
    const = lambda a: _resident(a.shape, lambda i: (0,) * a.ndim)
    kv = KV_F * HD_F
    kv_major = k_all.ndim == 4
    if kv_major:
        stacked = pl.BlockSpec((None, None, kv, tm), lambda i: (layer, i // tiles_per_seq, 0, i % tiles_per_seq))
    else:
        stacked = pl.BlockSpec((None, tm, kv), lambda i: (layer, i, 0))
    in_place = pl.BlockSpec(memory_space=pl.ANY)
    outs = [(D_MODEL, F32), (CONV_DIM, F32), (H_F * HD_F, q_dtype), None, (kv, BF16),
            None, (kv, BF16), (D_MODEL, F32), (D_MODEL, F32), (LANE, F32)]
    return pl.pallas_call(
        functools.partial(_inproj_kernel, q_scale=q_scale, kv_major=kv_major),
        grid=(m // tm,),
        in_specs=[row(D_MODEL),
                  pl.BlockSpec((1, D_MODEL), lambda i: (0, 0)),
                  _mod_spec(mod_rows, tm, tiles_per_seq, 3),
                  _mod_spec(mod_rows, tm, tiles_per_seq, 4),
                  _layer_resident(w, layer), const(g256), const(qnw), const(knw), const(sbias),
                  in_place, in_place],
        out_specs=[stacked if o is None else row(o[0]) for o in outs],
        out_shape=[jax.ShapeDtypeStruct(k_all.shape, F32) if o is None else jax.ShapeDtypeStruct((m, o[0]), o[1])
                   for o in outs],
        input_output_aliases={9: 3, 10: 5},
        scratch_shapes=[pltpu.VMEM((tm, D_MODEL), BF16)],
        compiler_params=_params("parallel"),
        name="inproj",
    )(h, nw, mod, mod, w, g256, qnw, knw, sbias, k_all, v_all)


def _fcum_kernel(sm_ref, kb_ref, tri_ref, pq_ref, pk_ref, oq_ref, ok_ref, fq_ref, kcat_ref, carry_ref):
    @pl.when(pl.program_id(1) == 0)
    def _():
        carry_ref[...] = jnp.zeros_like(carry_ref)

    cum = _dot01_l(tri_ref[...], sm_ref[...]) + carry_ref[0:1, :]
    carry_ref[0:1, :] = cum[cum.shape[0] - 1:, :]
    pieces = _split3(cum * LOG2E)

    def place(p_ref):
        return sum(jnp.dot(pieces[i], p_ref[i], preferred_element_type=F32) for i in range(3))

    fq_ref[...] = (place(pq_ref) + oq_ref[...]).astype(BF16)
    fk = (ok_ref[...] - place(pk_ref)).astype(BF16)
    for pair in range(KV_F // 2):
        kcat_ref[:, 2 * pair * LANE:(2 * pair + 1) * LANE] = kb_ref[:, pair * LANE:(pair + 1) * LANE]
        kcat_ref[:, (2 * pair + 1) * LANE:(2 * pair + 2) * LANE] = fk


def _fcum(small, kb, consts, bsz, s_len):
    tri = consts["tri_f"]
    t = tri.shape[0]
    nt = s_len // t
    kvw = KV_F * HD_F
    row = lambda n: pl.BlockSpec((t, n), lambda b, i: (b * nt + i, 0))
    const = lambda a: _resident(a.shape, lambda b, i: (0,) * a.ndim)
    cs = [tri, consts["place_q"], consts["place_k"], consts["ones_q"], consts["ones_k"]]
    return pl.pallas_call(
        _fcum_kernel,
        grid=(bsz, nt),
        in_specs=[row(LANE), row(kvw)] + [const(a) for a in cs],
        out_specs=[row(LANE), row(2 * kvw)],
        out_shape=[jax.ShapeDtypeStruct((bsz * s_len, LANE), BF16),
                   jax.ShapeDtypeStruct((bsz * s_len, 2 * kvw), BF16)],
        scratch_shapes=[pltpu.VMEM((8, LANE), F32)],
        compiler_params=_params("parallel", "arbitrary"),
        name="fcum",
    )(small, kb, *cs)


def _mamba_kernel(xbc_ref, z_ref, sm_ref, cprev_ref, h0_ref, cw_ref, cb_ref, alog_ref, dsk_ref, gnw_ref,
                  tri_ref, e_ref, ya_ref, cnew_ref, ssm_ref, hist_ref, dt_ref, *, rows, valid):
    q = CHUNK
    c = pl.program_id(1)
    di = D_MODEL

    @pl.when(c == 0)
    def _():
        hist_ref[0:8, :] = cprev_ref[...]
        ssm_ref[...] = h0_ref[...]
        if rows < q:
            hist_ref[8 + rows:, :] = jnp.zeros((q - rows, CONV_DIM), F32)
            dt_ref[...] = jnp.zeros_like(dt_ref)

    hist_ref[8:8 + rows, :] = xbc_ref[...]
    ext = hist_ref[...]
    conv = cb_ref[...]
    for j in range(CONV_W):
        back = CONV_W - 1 - j
        tap = ext if back == 0 else pltpu.roll(ext, back, axis=0)
        conv = conv + tap[8:8 + q, :] * cw_ref[j:j + 1, :]
    cnew_ref[...] = hist_ref[5 + valid:8 + valid, :]
    hist_ref[5:8, :] = hist_ref[5 + q:8 + q, :]
    act = conv * _sigmoid(conv)
    xs = act[:, :di]
    bmat = [act[:, di + g * N_A:di + (g + 1) * N_A].astype(BF16) for g in range(G_A)]
    cmat = [act[:, di + (G_A + g) * N_A:di + (G_A + g + 1) * N_A].astype(BF16) for g in range(G_A)]

    lane = lax.broadcasted_iota(jnp.int32, (1, LANE), 1)
    a_row = jnp.where(lane < H_A, -jnp.exp(alog_ref[...]), 0.0)
    if rows < q:
        rid = lax.broadcasted_iota(jnp.int32, (rows, LANE), 0)
        dt_ref[0:rows, :] = jnp.where(rid < valid, sm_ref[...], 0.0)
        dt = dt_ref[...]
    else:
        dt = sm_ref[...]
    tri = tri_ref[...]
    acum = _dot01_l(tri, dt * a_row)
    acum_t = acum.T
    expand = e_ref[...]
    dt_e = _dot01_r(dt, expand)
    acum_e = _dot01_r(acum, expand)
    xdt = xs * dt_e

    rq = lax.broadcasted_iota(jnp.int32, (q, q), 0)
    rs = lax.broadcasted_iota(jnp.int32, (q, q), 1)
    causal = rs <= rq
    half = lax.broadcasted_iota(jnp.int32, (q, LANE), 1) < P_A
    cb = [_dot_nt(cmat[g], bmat[g]) for g in range(G_A)]
    xdt_b = xdt.astype(BF16)
    y_parts = []
    for pair in range(H_A // 2):
        res = []
        for o in range(2):
            h = 2 * pair + o
            seg = acum[:, h:h + 1] - acum_t[h:h + 1, :]
            mix = (cb[h // (H_A // G_A)] * jnp.where(causal, jnp.exp(seg), 0.0)).astype(BF16)
            res.append(jnp.dot(mix, xdt_b[:, pair * LANE:(pair + 1) * LANE], preferred_element_type=F32))
        y_parts.append(jnp.where(half, res[0], res[1]))
    y = jnp.concatenate(y_parts, axis=1)

    hg = di // G_A
    state = ssm_ref[...]
    state_b = state.astype(BF16)
    y_off = jnp.concatenate([_dot_nt(cmat[g], state_b[g * hg:(g + 1) * hg, :]) for g in range(G_A)], axis=1)
    y = y + y_off * jnp.exp(acum_e)
    xdt_t = xdt.T
    acum_et = acum_e.T
    a_end = acum_et[:, q - 1:q]
    xdtd_t = (xdt_t * jnp.exp(a_end - acum_et)).astype(BF16)
    upd = jnp.concatenate(
        [jnp.dot(xdtd_t[g * hg:(g + 1) * hg, :], bmat[g], preferred_element_type=F32) for g in range(G_A)], axis=0)
    ssm_ref[...] = jnp.exp(a_end) * state + upd

    y = (y + dsk_ref[...] * xs)[0:rows, :]
    zz = z_ref[...]
    gated = y * (zz * _sigmoid(zz))
    out = gated * lax.rsqrt(jnp.mean(gated * gated, axis=-1, keepdims=True) + EPS) * gnw_ref[...]
    ya_ref[...] = out.astype(ya_ref.dtype)


def _mamba(xbc, z, small, cprev8, h0, layer, cw8, cb, alog, dsk, gnw, tri, expand, nseq, nchunk, rows, valid,
           ya_dtype):
    m = xbc.shape[0]
    row = lambda n: pl.BlockSpec((rows, n), lambda b, c: (b * nchunk + c, 0))
    const = lambda a: _resident(a.shape, lambda b, c: (0,) * a.ndim)
    di = D_MODEL
    kern = functools.partial(_mamba_kernel, rows=rows, valid=valid)
    return pl.pallas_call(
        kern,
        grid=(nseq, nchunk),
        in_specs=[row(CONV_DIM), row(di), row(LANE),
                  pl.BlockSpec((None, 8, CONV_DIM), lambda b, c: (b, 0, 0)),
                  pl.BlockSpec((None, None, di, N_A), lambda b, c: (layer, b, 0, 0)),
                  const(cw8), const(cb), const(alog), const(dsk), const(gnw), const(tri), const(expand)],
        out_specs=[row(di),
                   pl.BlockSpec((None, CONV_W - 1, CONV_DIM), lambda b, c: (b, 0, 0)),
                   pl.BlockSpec((None, di, N_A), lambda b, c: (b, 0, 0))],
        out_shape=[jax.ShapeDtypeStruct((m, di), ya_dtype),
                   jax.ShapeDtypeStruct((nseq, CONV_W - 1, CONV_DIM), F32),
                   jax.ShapeDtypeStruct((nseq, di, N_A), F32)],
        scratch_shapes=[pltpu.VMEM((8 + CHUNK, CONV_DIM), F32), pltpu.VMEM((CHUNK, LANE), F32)],
        compiler_params=_params("parallel", "arbitrary"),
        name="mamba",
    )(xbc, z, small, cprev8, h0, cw8, cb, alog, dsk, gnw, tri, expand)


def _attn_kernel(q_ref, fq_ref, k_ref, v_ref, o_ref, qcat_ref, m_ref, l_ref, acc_ref, *, tq, tk):
    qi = pl.program_id(1)
    ki = pl.program_id(2)
    nk = pl.num_programs(2)

    @pl.when(ki == 0)
    def _():
        m_ref[...] = jnp.full_like(m_ref, NEG)
        l_ref[...] = jnp.zeros_like(l_ref)
        acc_ref[...] = jnp.zeros_like(acc_ref)
        lane = lax.broadcasted_iota(jnp.int32, (tq, LANE), 1)
        lo = lane < HD_F
        fq = fq_ref[...]
        for blk in range(H_F // 2):
            qb = q_ref[:, blk * LANE:(blk + 1) * LANE]
            for o in range(2):
                slot = 2 * blk + o
                hq = HEAD_PERM[slot]
                own = jnp.logical_and(lane >= F_LANES * hq, lane < F_LANES * (hq + 1))
                qcat_ref[slot, :, 0:LANE] = jnp.where(lo if o == 0 else jnp.logical_not(lo), qb, jnp.zeros_like(qb))
                qcat_ref[slot, :, LANE:2 * LANE] = jnp.where(own, fq, jnp.zeros_like(fq))

    tn = (((0,), (0,)), ((), ()))

    def full_step():
        def scores(slot):
            pair = slot // 4
            return _dot_nt(k_ref[:, 2 * pair * LANE:(2 * pair + 2) * LANE], qcat_ref[slot])

        ahead = [scores(0), scores(1)]
        for slot in range(H_F):
            pair = slot // 4
            s = ahead.pop(0)
            if slot + 2 < H_F:
                ahead.append(scores(slot + 2))
            m_old = m_ref[slot]
            m_new = jnp.maximum(m_old, jnp.max(s, axis=0, keepdims=True))
            alpha = jnp.exp2(m_old - m_new)
            p = jnp.exp2(s - m_new)
            l_ref[slot] = alpha * l_ref[slot] + jnp.sum(p, axis=0, keepdims=True)
            m_ref[slot] = m_new
            pv = lax.dot_general(v_ref[:, pair * LANE:(pair + 1) * LANE], p.astype(BF16), tn,
                                 preferred_element_type=F32)
            acc_ref[slot] = alpha * acc_ref[slot] + pv

    def diagonal_step():
        hk, hq = tk // 2, tq // 2
        causal_a = (lax.broadcasted_iota(jnp.int32, (hk, tq), 0) <= lax.broadcasted_iota(jnp.int32, (hk, tq), 1))
        causal_b = (lax.broadcasted_iota(jnp.int32, (hk, hq), 0) <= lax.broadcasted_iota(jnp.int32, (hk, hq), 1))
        def scores(slot):
            kl = slice(2 * (slot // 4) * LANE, (2 * (slot // 4) + 2) * LANE)
            return (_dot_nt(k_ref[0:hk, kl], qcat_ref[slot]),
                    _dot_nt(k_ref[hk:tk, kl], qcat_ref[slot, hq:tq, :]))

        ahead = [scores(0), scores(1)]
        for slot in range(H_F):
            pair = slot // 4
            vl = slice(pair * LANE, (pair + 1) * LANE)
            s_a, s_b = ahead.pop(0)
            if slot + 2 < H_F:
                ahead.append(scores(slot + 2))
            s_a = jnp.where(causal_a, s_a, NEG)
            s_b = jnp.where(causal_b, s_b, NEG)
            m_old = m_ref[slot]
            m_a = jnp.max(s_a, axis=0, keepdims=True)
            m_b = jnp.max(s_b, axis=0, keepdims=True)
            m_new = jnp.maximum(m_old, jnp.concatenate([m_a[:, 0:hq], jnp.maximum(m_a[:, hq:tq], m_b)], axis=1))
            alpha = jnp.exp2(m_old - m_new)
            p_a = jnp.exp2(s_a - m_new)
            p_b = jnp.exp2(s_b - m_new[:, hq:tq])
            l_add = jnp.sum(p_a, axis=0, keepdims=True)
            l_b = jnp.sum(p_b, axis=0, keepdims=True)
            l_ref[slot] = alpha * l_ref[slot] + jnp.concatenate([l_add[:, 0:hq], l_add[:, hq:tq] + l_b], axis=1)
            m_ref[slot] = m_new
            pv_a = lax.dot_general(v_ref[0:hk, vl], p_a.astype(BF16), tn, preferred_element_type=F32)
            pv_b = lax.dot_general(v_ref[hk:tk, vl], p_b.astype(BF16), tn, preferred_element_type=F32)
            acc_ref[slot] = alpha * acc_ref[slot] + jnp.concatenate(
                [pv_a[:, 0:hq], pv_a[:, hq:tq] + pv_b], axis=1)

    pl.when(ki < qi)(full_step)
    pl.when(ki == qi)(diagonal_step)

    @pl.when(ki == nk - 1)
    def _():
        lo = lax.broadcasted_iota(jnp.int32, (LANE, tq), 0) < HD_F
        for blk in range(H_F // 2):
            a0 = acc_ref[2 * blk] / l_ref[2 * blk]
            a1 = acc_ref[2 * blk + 1] / l_ref[2 * blk + 1]
            o_ref[:, blk * LANE:(blk + 1) * LANE] = jnp.where(lo, a0, a1).T.astype(o_ref.dtype)


def _attn(qb, fq, kcat, vb, bsz, s_len, tq):
    tk = tq
    nq = s_len // tq
    kern = functools.partial(_attn_kernel, tq=tq, tk=tk)
    kvw = KV_F * HD_F
    q_spec = lambda n: pl.BlockSpec((tq, n), lambda b, i, j: (b * nq + i, 0))
    k_spec = lambda n: pl.BlockSpec((tk, n), lambda b, i, j: (b * nq + jnp.minimum(i, j), 0))
    return pl.pallas_call(
        kern,
        grid=(bsz, nq, nq),
        in_specs=[q_spec(D_MODEL), q_spec(LANE), k_spec(2 * kvw), k_spec(kvw)],
        out_specs=q_spec(D_MODEL),
        out_shape=jax.ShapeDtypeStruct((bsz * s_len, D_MODEL), BF16),
        scratch_shapes=[pltpu.VMEM((H_F, tq, 2 * LANE), BF16),
                        pltpu.VMEM((H_F, 1, tq), F32), pltpu.VMEM((H_F, 1, tq), F32),
                        pltpu.VMEM((H_F, LANE, tq), F32)],
        compiler_params=_params("parallel", "parallel", "arbitrary"),
        name="attn",
    )(qb, fq, kcat, vb)


def _dec_kernel(pt_ref, q_ref, kn_ref, vn_ref, sm_ref, *rest, npp, tokens, valid):
    k_refs = rest[0:npp]
    v_refs = rest[npp:2 * npp]
    f_refs = rest[2 * npp:3 * npp]
    (et128_ref, tri_ref, slt_ref, o_ref, smp_ref, m_ref, l_ref, gq_ref, carry_ref, acc_ref) = rest[3 * npp:]
    g = pl.program_id(1)
    ng = pl.num_programs(1)
    kvw = KV_F * HD_F
    qbd = q_ref[...].astype(BF16)

    def update(scores, values_t):
        m_old = m_ref[...]
        m_new = jnp.maximum(m_old, jnp.max(functools.reduce(jnp.maximum, scores), axis=1, keepdims=True))
        alpha = jnp.exp(m_old - m_new)
        p_sum = jnp.zeros(scores[0].shape, F32)
        pv = jnp.zeros((LANE, kvw), F32)
        for s, vt in zip(scores, values_t):
            p = jnp.exp(s - m_new)
            p_sum = p_sum + p
            pv = pv + _dot_nt(p.astype(BF16), vt)
        l_ref[...] = alpha * l_ref[...] + jnp.sum(p_sum, axis=1, keepdims=True)
        m_ref[...] = m_new
        acc_ref[...] = alpha * acc_ref[...] + pv

    @pl.when(g == 0)
    def _():
        m_ref[...] = jnp.full_like(m_ref, NEG)
        l_ref[...] = jnp.zeros_like(l_ref)
        carry_ref[...] = jnp.zeros_like(carry_ref)
        acc_ref[...] = jnp.zeros_like(acc_ref)
        smp_ref[...] = jnp.zeros_like(smp_ref)
        smp_ref[0:tokens, :] = sm_ref[...]
        cum_t = _dot01_l(tri_ref[...], smp_ref[...]).T
        cum_e = _dot01_l(et128_ref[...], cum_t)
        key = lax.broadcasted_iota(jnp.int32, (LANE, PAGE), 1)
        qtok = lax.broadcasted_iota(jnp.int32, (LANE, PAGE), 0) % tokens
        g_q = jnp.sum(jnp.where(key == qtok, cum_e, 0.0), axis=1, keepdims=True)
        gq_ref[...] = g_q
        ok = jnp.logical_and(key <= qtok, key < valid)
        s = jnp.dot(qbd, kn_ref[...].astype(BF16), preferred_element_type=F32) + (g_q - cum_e)
        update([jnp.where(ok, s, NEG)], [vn_ref[...].astype(BF16)])

    lf = jnp.concatenate([f_refs[i][...] for i in range(npp)], axis=0)
    later = _dot01_r(lf, slt_ref[...])
    mass = later[:, 0:1] + lf[:, 0:1]
    carry = carry_ref[...]
    g_q = gq_ref[...]
    biases = [None] * npp
    for i in reversed(range(npp)):
        per_head = later[i * H_F:(i + 1) * H_F, :] + carry
        biases[i] = jnp.concatenate(
            [jnp.broadcast_to(per_head[h:h + 1, :], (tokens, PAGE)) for h in range(H_F)], axis=0) + g_q
        carry = carry + mass[i * H_F:(i + 1) * H_F, :]
    carry_ref[...] = carry
    group = 2 if npp % 2 == 0 else 1
    page_t = lambda refs, i: refs[i][...].reshape(kvw, PAGE).astype(BF16)
    side_by_side = lambda parts: parts[0] if len(parts) == 1 else jnp.concatenate(parts, axis=1)
    scores, values_t = [], []
    for i0 in range(0, npp, group):
        ids = range(i0, i0 + group)
        kt = side_by_side([page_t(k_refs, i) for i in ids])
        scores.append(jnp.dot(qbd, kt, preferred_element_type=F32) + side_by_side([biases[i] for i in ids]))
        values_t.append(side_by_side([page_t(v_refs, i) for i in ids]))
    update(scores, values_t)

    @pl.when(g == ng - 1)
    def _():
        o_ref[...] = acc_ref[...] / l_ref[...]


def _dec_attn(page_table, qbd, kn_t, vn_t, small, cache_kt, cache_vt, cache_ft, layer, dc, tokens, valid, npp):
    nseq, n_pages = page_table.shape
    ng = n_pages // npp
    kvw = KV_F * HD_F
    rows = H_F * tokens
    kern = functools.partial(_dec_kernel, npp=npp, tokens=tokens, valid=valid)
    per_seq = lambda r, c: pl.BlockSpec((None, r, c), lambda b, g, pt: (b, 0, 0))
    page_idx = lambda b, g, pt, i: pt[b, (ng - 1 - g) * npp + i]
    kv_page = lambda i: pl.BlockSpec((None, None, KV_F, HD_F, PAGE),
                                     lambda b, g, pt: (layer, page_idx(b, g, pt, i), 0, 0, 0))
    f_page = lambda i: pl.BlockSpec((None, None, H_F, PAGE), lambda b, g, pt: (layer, page_idx(b, g, pt, i), 0, 0))
    const = lambda a: _resident(a.shape, lambda b, g, pt: (0,) * a.ndim)
    cs = [dc["et128"], dc["tri"], dc["slt"]]
    in_specs = ([per_seq(rows, kvw), per_seq(kvw, PAGE), per_seq(kvw, PAGE),
                 pl.BlockSpec((tokens, LANE), lambda b, g, pt: (b, 0))]
                + [kv_page(i) for i in range(npp)] + [kv_page(i) for i in range(npp)]
                + [f_page(i) for i in range(npp)] + [const(a) for a in cs])
    col = pltpu.VMEM((rows, 1), F32)
    grid_spec = pltpu.PrefetchScalarGridSpec(
        num_scalar_prefetch=1,
        grid=(nseq, ng),
        in_specs=in_specs,
        out_specs=per_seq(rows, kvw),
        scratch_shapes=[pltpu.VMEM((PAGE, LANE), F32), col, col, col, pltpu.VMEM((H_F, 1), F32),
                        pltpu.VMEM((rows, kvw), F32)])
    return pl.pallas_call(
        kern,
        grid_spec=grid_spec,
        out_shape=jax.ShapeDtypeStruct((nseq, rows, kvw), F32),
        compiler_params=_params("parallel", "arbitrary"),
        name="dec_attn",
    )(page_table, qbd, kn_t, vn_t, small, *([cache_kt] * npp), *([cache_vt] * npp), *([cache_ft] * npp), *cs)


def _merge_kernel(h_ref, ya_ref, yb_ref, ga_ref, gb_ref, gt_ref, wa_ref, wb_ref, wo_ref, o_ref):
    tm = h_ref.shape[0]
    halves = [slice(0, tm // 2), slice(tm // 2, tm)] if tm % 16 == 0 else [slice(0, tm)]
    proj = [(jnp.dot(ya_ref[r, :].astype(BF16), wa_ref[...], preferred_element_type=F32),
             jnp.dot(yb_ref[r, :].astype(BF16), wb_ref[...], preferred_element_type=F32)) for r in halves]
    for r, (pa, pb) in zip(halves, proj):
        merged = _sigmoid(ga_ref[r, :]) * pa + _sigmoid(gb_ref[r, :]) * pb
        out = jnp.dot(merged.astype(BF16), wo_ref[...], preferred_element_type=F32)
        gate = gt_ref[...] if gt_ref.shape[0] == 1 else gt_ref[r, :]
        o_ref[r, :] = h_ref[r, :] + gate * out


def _merge(h, ya, yb, ga, gb, mod, wa, wb, wo, layer, tm, tiles_per_seq, mod_rows):
    m = h.shape[0]
    row = pl.BlockSpec((tm, D_MODEL), lambda i: (i, 0))
    return pl.pallas_call(
        _merge_kernel,
        grid=(m // tm,),
        in_specs=[row, row, row, row, row, _mod_spec(mod_rows, tm, tiles_per_seq, 5),
                  _layer_resident(wa, layer), _layer_resident(wb, layer), _layer_resident(wo, layer)],
        out_specs=row,
        out_shape=jax.ShapeDtypeStruct((m, D_MODEL), F32),
        compiler_params=_params("parallel"),
        name="merge",
    )(h, ya, yb, ga, gb, mod, wa, wb, wo)


def _constants():
    f = np.float32
    head = np.arange(D_MODEL) // HD_F
    g256 = (head[:256, None] == head[None, :256]).astype(f)
    tri128 = np.tril(np.ones((CHUNK, CHUNK), f))
    tri_f = np.tril(np.ones((FCUM_TILE, FCUM_TILE), f))
    striu = np.triu(np.ones((PAGE, PAGE), f), 1)
    expand = np.zeros((LANE, D_MODEL), f)
    expand[head, np.arange(D_MODEL)] = 1.0
    place_q = np.zeros((3, LANE, LANE), f)
    place_k = np.zeros((3, LANE, LANE), f)
    ones_q = np.zeros((1, LANE), f)
    ones_k = np.zeros((1, LANE), f)
    for h in range(H_F):
        for i in range(3):
            place_q[i, H_A + h, F_LANES * h + i] = 1.0
            place_k[i, H_A + h, F_LANES * h + 3 + i] = 1.0
            ones_q[0, F_LANES * h + 3 + i] = 1.0
            ones_k[0, F_LANES * h + i] = 1.0
    out = {k: jnp.asarray(v, BF16) for k, v in
           dict(g256=g256, tri128=tri128, tri_f=tri_f, striu=striu, expand=expand,
                place_q=place_q, place_k=place_k).items()}
    out.update(ones_q=jnp.asarray(ones_q), ones_k=jnp.asarray(ones_k))
    return out


def _dec_constants(tokens, consts):
    f = np.float32
    et128 = np.zeros((H_F * tokens, LANE), f)
    for hq in range(H_F):
        et128[hq * tokens:(hq + 1) * tokens, H_A + hq] = 1.0
    slt = np.tril(np.ones((PAGE, PAGE), f), -1)
    return dict(et128=jnp.asarray(et128, BF16), tri=consts["tri128"], slt=jnp.asarray(slt, BF16))


def _inproj_weights(w_in):
    depth = w_in.shape[0]
    sizes = (D_MODEL, CONV_DIM, H_A, H_F * HD_F, KV_F * HD_F, KV_F * HD_F, H_F, D_MODEL, D_MODEL)
    pts = np.cumsum(sizes)[:-1].tolist()
    wz, wx, wdt, wq, wk, wv, wf, wga, wgb = jnp.split(w_in.astype(BF16), pts, axis=-1)
    wq = wq.reshape(depth, D_MODEL, H_F, HD_F)[:, :, list(HEAD_PERM), :].reshape(depth, D_MODEL, H_F * HD_F)
    pad = jnp.zeros((depth, D_MODEL, LANE - H_A - H_F), BF16)
    return jnp.concatenate([wz, wx, wq, wk, wv, wga, wgb, wdt, wf, pad], axis=-1)


def _run_group(x2d, mod, mod_rows, tiles, conv_prev, ssm_prev, lw, consts, fox, nseq, nchunk, rows, valid,
               q_dtype, q_scale, y_dtype, kv_major):
    tm_ffn, tm_in, tm_mg, tps_ffn, tps_in, tps_mg = tiles
    h = x2d
    states = []
    depth = lw["w_in"].shape[0]
    kv_shape = ((depth, nseq, KV_F * HD_F, x2d.shape[0] // nseq) if kv_major
                else (depth, x2d.shape[0], KV_F * HD_F))
    k_all = jnp.zeros(kv_shape, F32)
    v_all = jnp.zeros(kv_shape, F32)
    for l in range(depth):
        ml = mod[l]
        h = _ffn(h, ml, 0, lw["nw"][l, 0:1], lw["wi1"], lw["wo1"], l, tm_ffn, tps_ffn, mod_rows)
        z, xbc, q, k_all, kb, v_all, vb, ga, gb, small = _inproj(
            h, ml, lw["nw"][l, 1:2], lw["w_in"], l, consts["g256"], lw["qnw"][l], lw["knw"][l], lw["sbias"][l],
            k_all, v_all, tm_in, tps_in, mod_rows, q_dtype, q_scale)
        cprev8 = jnp.pad(conv_prev[l], ((0, 0), (8 - (CONV_W - 1), 0), (0, 0)))
        ya, conv_new, ssm_new = _mamba(
            xbc, z, small, cprev8, ssm_prev, l, lw["cw8"][l], lw["cb"][l], lw["alog"][l], lw["dsk"][l], lw["gnw"][l],
            consts["tri128"], consts["expand"], nseq, nchunk, rows, valid, y_dtype)
        yb = fox(l, q, k_all, kb, v_all, vb, small)
        h = _merge(h, ya, yb, ga, gb, ml, lw["wa"], lw["wb"], lw["wout"], l, tm_mg, tps_mg, mod_rows)
        h = _ffn(h, ml, 6, lw["nw"][l, 2:3], lw["wi2"], lw["wo2"], l, tm_ffn, tps_ffn, mod_rows)
        states.append((small, conv_new, ssm_new))
    return h, k_all, v_all, states


def kernel(x_prompt, x_sample, cache_k, cache_v, cache_logf, state_conv, state_ssm, page_table,
           c_prompt, c_sample, w_ada, b_ada, norm_w, w_ffn1_in, w_ffn1_out, w_ffn2_in, w_ffn2_out,
           w_in, conv_w, conv_b, dt_bias, a_log, d_skip, gn_w, q_norm_w, k_norm_w, b_f,
           w_proj_a, w_proj_b, w_out):
    bp, s_len, d = x_prompt.shape
    db, t_len, _ = x_sample.shape
    depth = w_in.shape[0]
    n_pool = cache_k.shape[1]
    kvw = KV_F * HD_F
    assert d == D_MODEL and s_len % CHUNK == 0 and t_len <= 8 and cache_k.shape[2] == PAGE
    consts = _constants()

    pad_lanes = lambda a: jnp.pad(a, ((0, 0), (0, LANE - a.shape[1])))[:, None, :]
    lw = dict(
        nw=norm_w,
        wi1=w_ffn1_in.astype(BF16), wo1=w_ffn1_out.astype(BF16),
        wi2=w_ffn2_in.astype(BF16), wo2=w_ffn2_out.astype(BF16),
        w_in=_inproj_weights(w_in),
        qnw=jnp.tile(q_norm_w, (1, 256 // HD_F))[:, None, :],
        knw=jnp.tile(k_norm_w, (1, 256 // HD_F))[:, None, :],
        sbias=pad_lanes(jnp.concatenate([dt_bias, b_f], axis=1)),
        cw8=jnp.pad(conv_w, ((0, 0), (0, 8 - CONV_W), (0, 0))),
        cb=conv_b[:, None, :],
        alog=pad_lanes(a_log),
        dsk=jnp.repeat(d_skip, P_A, axis=1)[:, None, :],
        gnw=gn_w[:, None, :],
        wa=w_proj_a.astype(BF16),
        wb=w_proj_b.reshape(depth, H_F, HD_F, d)[:, list(HEAD_PERM)].reshape(depth, d, d).astype(BF16),
        wout=w_out.astype(BF16),
    )

    mod_all = _ada(jnp.concatenate([c_prompt, c_sample], axis=0), w_ada, b_ada)
    mod_all = mod_all.reshape(depth, bp + db, N_MOD, d)
    mod_p = mod_all[:, :bp, :, None, :]
    tok = 8
    mod_s = jnp.broadcast_to(mod_all[:, bp:, None], (depth, db, tok, N_MOD, d))
    mod_s = mod_s.reshape(depth, db * tok, N_MOD, d).transpose(0, 2, 1, 3)

    tq = min(512, s_len)

    def fox_p(l, q, k, kb, v, vb, small):
        fq, kcat = _fcum(small, kb, consts, bp, s_len)
        return _attn(q, fq, kcat, vb, bp, s_len, tq)

    tm = min(512, s_len)
    tm_in = min(512, s_len)
    tiles_p = (tm, tm_in, tm, s_len // tm, s_len // tm_in, s_len // tm)
    conv0 = jnp.zeros((depth, bp, CONV_W - 1, CONV_DIM), F32)
    ssm0 = jnp.zeros((depth, bp, D_MODEL, N_A), F32)
    y_p, kp_all, vp_all, st_p = _run_group(x_prompt.reshape(bp * s_len, d), mod_p, 1, tiles_p, conv0, ssm0, lw, consts, fox_p,
                           bp, s_len // CHUNK, CHUNK, CHUNK, BF16, ATTN_SCALE * LOG2E, BF16, True)

    dc = _dec_constants(tok, consts)
    npp = next(n for n in (32, 16, 8, 4, 2, 1) if page_table.shape[1] % n == 0)
    head_inv = list(np.argsort(HEAD_PERM))

    cache_kt = jnp.transpose(cache_k, (0, 1, 3, 4, 2))
    cache_vt = jnp.transpose(cache_v, (0, 1, 3, 4, 2))
    cache_ft = jnp.transpose(cache_logf, (0, 1, 3, 2))
    kv_of_head = np.arange(H_F) // (H_F // KV_F)
    own_kv = jnp.asarray(kv_of_head[:, None] == np.arange(KV_F)[None, :], F32)
    assert H_F * tok == LANE

    def fox_s(l, q, k, kb, v, vb, small):
        q2 = q.reshape(db, tok, H_F, HD_F)[:, :, head_inv].transpose(0, 2, 1, 3)
        qbd = (q2[:, :, :, None, :] * own_kv[None, :, None, :, None]).reshape(db, H_F * tok, kvw)
        new_t = lambda a: jnp.pad(a.reshape(db, tok, kvw).transpose(0, 2, 1), ((0, 0), (0, 0), (0, PAGE - tok)))
        out = _dec_attn(page_table, qbd, new_t(k[l]), new_t(v[l]), small, cache_kt, cache_vt, cache_ft, l, dc,
                        tok, t_len, npp)
        out = out.reshape(db, H_F, tok, KV_F, HD_F)
        pick = jnp.broadcast_to(jnp.asarray(kv_of_head)[None, :, None, None, None], (db, H_F, tok, 1, HD_F))
        out = jnp.take_along_axis(out, pick, axis=3)[:, :, :, 0]
        return out[:, list(HEAD_PERM)].transpose(0, 2, 1, 3).reshape(db * tok, d)

    ms = db * tok
    tiles_s = (ms, ms, ms, 1, 1, 1)
    x_s = jnp.pad(x_sample, ((0, 0), (0, tok - t_len), (0, 0))).reshape(ms, d)
    y_s, ks_all, vs_all, st_s = _run_group(x_s, mod_s, ms, tiles_s, state_conv, state_ssm.reshape(depth, db, D_MODEL, N_A), lw, consts,
                           fox_s, db, 1, tok, t_len, F32, ATTN_SCALE, F32, False)

    def gather(k_all, v_all, states, nseq, length, keep):
        if k_all.ndim == 4:
            k = k_all.reshape(depth, nseq, KV_F, HD_F, length).transpose(0, 1, 4, 2, 3)
            v = v_all.reshape(depth, nseq, KV_F, HD_F, length).transpose(0, 1, 4, 2, 3)
        else:
            k = k_all.reshape(depth, nseq, length, KV_F, HD_F)[:, :, :keep]
            v = v_all.reshape(depth, nseq, length, KV_F, HD_F)[:, :, :keep]
        lf = jnp.stack([s[0][:, H_A:H_A + H_F] for s in states]).reshape(depth, nseq, length, H_F)[:, :, :keep]
        conv = jnp.stack([s[1] for s in states])
        ssm = jnp.stack([s[2] for s in states]).reshape(depth, nseq, H_A, P_A, N_A)
        return k, v, lf, conv, ssm

    k_p, v_p, lf_p, conv_p, ssm_p = gather(kp_all, vp_all, st_p, bp, s_len, s_len)
    k_s, v_s, lf_s, conv_s, ssm_s = gather(ks_all, vs_all, st_s, db, tok, t_len)
    y_prompt = y_p.reshape(bp, s_len, d)
    y_sample = y_s.reshape(db, tok, d)[:, :t_len]
    return (y_prompt, y_sample, k_p, v_p, lf_p, conv_p, ssm_p, k_s, v_s, lf_s, conv_s, ssm_s)
```
